```python
import math
import jax
import jax.numpy as jnp
from jax import lax
import numpy as np

D_MODEL = 1024
BATCH = 8
SEQ = 2048
DEPTH = 2

GRID_W = 64
CTX_LEN = 256
NORM_EPS = 1e-6
ROPE_BASE = 10000.0

RW_HEADS = 6
RW_HEAD_DIM = 64
RW_WIDTH = RW_HEADS * RW_HEAD_DIM
RW_DECAY_LORA = 64
RW_ICLR_LORA = 64
RW_GATE_LORA = 128
RW_GN_EPS = 64e-5
RW_IN = 3 * RW_WIDTH + 2 * RW_DECAY_LORA + 2 * RW_ICLR_LORA + RW_GATE_LORA
RW_SPLITS = (RW_WIDTH, 2 * RW_WIDTH, 3 * RW_WIDTH, 3 * RW_WIDTH + 2 * RW_DECAY_LORA,
             3 * RW_WIDTH + 2 * RW_DECAY_LORA + 2 * RW_ICLR_LORA)

HY_WIDTH = 256
HY_ORDER = 2
HY_SHORT_K = 3
HY_EMB_DIM = 33
HY_FILTER_HIDDEN = 64
HY_DECAY_TARGET = 1e-2
HY_FAST_DECAY = 0.3
HY_SLOW_DECAY = 1.5
HY_IN = (HY_ORDER + 1) * HY_WIDTH

NA_HEADS = 6
NA_HEAD_DIM = 64
NA_WIDTH = NA_HEADS * NA_HEAD_DIM
NA_WIN_ROWS = 8
NA_WIN_COLS = 16
NA_IN = 3 * NA_WIDTH

N_BRANCH = 3
GATE_IN = N_BRANCH * D_MODEL
P_IN = RW_IN + HY_IN + NA_IN + GATE_IN
IN_SPLITS = (RW_IN, RW_IN + HY_IN, RW_IN + HY_IN + NA_IN)

FF_DENSE = 2816
N_EXPERTS = 8
TOP_K = 2
FF_EXPERT = 3584
N_DENSE_LAYERS = (DEPTH + 1) // 2
N_MOE_LAYERS = DEPTH // 2

kernel_name = 'hybrid_rwkv7_hyena_natten_moe_dit'


def rms_norm(x, g, eps=NORM_EPS):
    xf = x.astype(jnp.float32)
    y = xf * lax.rsqrt(jnp.mean(xf * xf, axis=-1, keepdims=True) + eps)
    return (y * g.astype(jnp.float32)).astype(x.dtype)


def modulate(h, shift, scale):
    return h * (1 + scale) + shift


def centred_token_shift(u, mu):
    zero = jnp.zeros_like(u[:, :1])
    prev = jnp.concatenate([zero, u[:, :-1]], axis=1)
    nxt = jnp.concatenate([u[:, 1:], zero], axis=1)
    return u + mu[0] * (prev - u) + mu[1] * (nxt - u)


def centred_depthwise_conv(u, w, b):
    k = w.shape[0]
    pad = k // 2
    n = u.shape[1]
    up = jnp.pad(u, ((0, 0), (pad, pad), (0, 0)))
    out = b
    for j in range(k):
        out = out + w[j] * up[:, j:j + n]
    return out


def axial_rope(u):
    n, dh = u.shape[1], u.shape[-1]
    half = dh // 2
    nf = half // 2
    t = jnp.arange(n)
    inv = ROPE_BASE ** (-jnp.arange(nf, dtype=jnp.float32) / nf)
    uf = u.astype(jnp.float32)

    def rot(z, pos):
        ang = pos.astype(jnp.float32)[:, None] * inv
        cos = jnp.cos(ang)[None, :, None, :]
        sin = jnp.sin(ang)[None, :, None, :]
        z1, z2 = z[..., :nf], z[..., nf:]
        return jnp.concatenate([z1 * cos - z2 * sin, z1 * sin + z2 * cos], axis=-1)

    out = jnp.concatenate([rot(uf[..., :half], t // GRID_W), rot(uf[..., half:], t % GRID_W)], axis=-1)
    return out.astype(u.dtype)


def rwkv7_scan(r, w, k, v, a, b, s0):
    def step(S, inp):
        r_t, w_t, k_t, v_t, a_t, b_t = inp
        sa = jnp.einsum('bhvk,bhk->bhv', S, a_t)
        S = S * w_t[:, :, None, :] + sa[..., None] * b_t[:, :, None, :] + v_t[..., None] * k_t[:, :, None, :]
        return S, jnp.einsum('bhvk,bhk->bhv', S, r_t)

    xs = tuple(jnp.swapaxes(z, 0, 1) for z in (r, w, k, v, a, b))
    s_fin, ys = lax.scan(step, s0, xs)
    return jnp.swapaxes(ys, 0, 1), s_fin


def rwkv7_features(u, shift_mu, w0, w2, a0, a2, g2, k_k, k_a, use_rope):
    bsz, n, _ = u.shape
    heads = lambda z: z.reshape(bsz, n, RW_HEADS, RW_HEAD_DIM)
    u = centred_token_shift(u.astype(jnp.float32), shift_mu)
    r, k, v, lw, la, lg = jnp.split(u, RW_SPLITS, axis=-1)
    r, k, v = heads(r), heads(k), heads(v)
    if use_rope:
        r, k = axial_rope(r), axial_rope(k)
    g = jax.nn.sigmoid(lg) @ g2
    kk = k * k_k.reshape(RW_HEADS, RW_HEAD_DIM)
    kk = kk / jnp.maximum(jnp.sqrt(jnp.sum(kk * kk, axis=-1, keepdims=True)), 1e-12)
    lw = lw.reshape(bsz, n, 2, RW_DECAY_LORA)
    la = la.reshape(bsz, n, 2, RW_ICLR_LORA)
    k_a = k_a.reshape(RW_HEADS, RW_HEAD_DIM)
    per_dir = []
    for d in range(2):
        w_log = -jax.nn.softplus(-(w0[d] + jnp.tanh(lw[:, :, d]) @ w2[d])) - 0.5
        iclr = heads(jax.nn.sigmoid(a0[d] + la[:, :, d] @ a2[d]))
        per_dir.append((heads(jnp.exp(-jnp.exp(w_log))), k * (1 + (iclr - 1) * k_a), kk * iclr))
    return r, v, kk, g, per_dir


def rwkv7_direction(feats, d, s0):
    r, v, kk, _, per_dir = feats
    decay, k_d, b_d = per_dir[d]
    seq = (r, decay, k_d, v, -kk, b_d)
    if d == 1:
        seq = tuple(jnp.flip(z, axis=1) for z in seq)
    y, s_fin = rwkv7_scan(*seq, s0)
    if d == 1:
        y = jnp.flip(y, axis=1)
    return y, s_fin


def rwkv7_readout(feats, y, r_k, gn_g, gn_b):
    r, v, _, g, per_dir = feats
    bsz, n = r.shape[:2]
    mu = jnp.mean(y, axis=-1, keepdims=True)
    var = jnp.mean(jnp.square(y - mu), axis=-1, keepdims=True)
    yn = ((y - mu) * lax.rsqrt(var + RW_GN_EPS)).reshape(bsz, n, RW_WIDTH) * gn_g + gn_b
    k_sum = per_dir[0][1] + per_dir[1][1]
    bonus = jnp.sum(r * k_sum * r_k, axis=-1, keepdims=True) * v
    return (yn + bonus.reshape(bsz, n, RW_WIDTH)) * g


def rwkv7_mixer(u_ctx, u_lat, shift_mu, w0, w2, a0, a2, g2, k_k, k_a, r_k, gn_g, gn_b, with_ctx):
    args = (shift_mu, w0, w2, a0, a2, g2, k_k, k_a)
    f_ctx = rwkv7_features(u_ctx, *args, use_rope=False)
    f_lat = rwkv7_features(u_lat, *args, use_rope=True)
    s0 = jnp.zeros((u_lat.shape[0], RW_HEADS, RW_HEAD_DIM, RW_HEAD_DIM), jnp.float32)
    y_ctx, y_lat = 0.0, 0.0
    for d in range(2):
        yc, s_ctx = rwkv7_direction(f_ctx, d, s0)
        yl, _ = rwkv7_direction(f_lat, d, s_ctx)
        y_ctx = y_ctx + yc
        y_lat = y_lat + yl
    out_lat = rwkv7_readout(f_lat, y_lat, r_k, gn_g, gn_b)
    out_ctx = rwkv7_readout(f_ctx, y_ctx, r_k, gn_g, gn_b) if with_ctx else None
    return out_ctx, out_lat


def hyena_filters(n, w1, b1, w2, b2, w3, freq):
    pos = jnp.arange(n, dtype=jnp.float32)
    t = pos / max(n - 1, 1)
    bands = (HY_EMB_DIM - 1) // 2
    f = jnp.linspace(1e-4, bands - 1, bands, dtype=jnp.float32)
    ang = (2 * math.pi / n) * pos[:, None] * f[None, :]
    z = jnp.concatenate([t[:, None], jnp.cos(ang), -jnp.sin(ang)], axis=-1)
    h = jnp.sin(freq * (z @ w1 + b1))
    h = jnp.sin(freq * (h @ w2 + b2))
    h = (h @ w3).astype(jnp.float32).reshape(n, HY_ORDER, 2, HY_WIDTH)
    max_decay = math.log(HY_DECAY_TARGET) / HY_FAST_DECAY
    min_decay = math.log(HY_DECAY_TARGET) / HY_SLOW_DECAY
    deltas = jnp.abs(jnp.linspace(min_decay, max_decay, HY_WIDTH, dtype=jnp.float32))
    h = h * jnp.exp(-t[:, None] * deltas)[:, None, None, :]
    h_fwd = h[:, :, 0]
    h_bwd = h[1:, :, 1]
    l1 = jnp.sum(jnp.abs(h_fwd), axis=0) + jnp.sum(jnp.abs(h_bwd), axis=0)
    filt = jnp.concatenate([h_fwd, jnp.zeros_like(h_fwd[:1]), h_bwd[::-1]], axis=0) / l1
    return jnp.moveaxis(filt, 1, 0)


def fft_long_conv(z, filt):
    n = z.shape[1]
    zf = jnp.fft.rfft(z, n=2 * n, axis=1)
    hf = jnp.fft.rfft(filt, n=2 * n, axis=0)
    return jnp.fft.irfft(zf * hf[None], n=2 * n, axis=1)[:, :n]


def hyena_sequence(u, conv_w, conv_b, w1, b1, w2, b2, w3, freq, skip):
    n = u.shape[1]
    u = centred_depthwise_conv(u, conv_w, conv_b).astype(jnp.float32)
    streams = jnp.split(u, HY_ORDER + 1, axis=-1)
    filt = hyena_filters(n, w1, b1, w2, b2, w3, freq)
    z = streams[0]
    for o in range(HY_ORDER):
        z = streams[o + 1] * (fft_long_conv(z, filt[o]) + skip[o] * z)
    return z


def neighbourhood_attention(u_ctx, u_lat, q_gain, k_gain, rpb, with_ctx):
    bsz, n, _ = u_lat.shape
    lc = u_ctx.shape[1]
    scale = NA_HEAD_DIM ** -0.5

    def qkv(u):
        q, k, v = (z.reshape(bsz, u.shape[1], NA_HEADS, NA_HEAD_DIM)
                   for z in jnp.split(u.astype(jnp.float32), 3, axis=-1))
        return rms_norm(q, q_gain), rms_norm(k, k_gain), v

    qc, kc, vc = qkv(u_ctx)
    ql, kl, vl = qkv(u_lat)
    out_ctx = None
    if with_ctx:
        p = jax.nn.softmax(jnp.einsum('bqhd,bkhd->bhqk', qc, kc) * scale, axis=-1)
        out_ctx = jnp.einsum('bhqk,bkhd->bqhd', p, vc).reshape(bsz, lc, NA_WIDTH)

    rows = n // GRID_W
    wr = min(NA_WIN_ROWS, rows)
    wc = NA_WIN_COLS
    grid = lambda z: z.reshape(bsz, rows, GRID_W, NA_HEADS, NA_HEAD_DIM)
    ql, kl, vl = grid(ql), grid(kl), grid(vl)
    r_ar = jnp.arange(rows)
    c_ar = jnp.arange(GRID_W)
    r0 = jnp.clip(r_ar - wr // 2, 0, rows - wr)
    c0 = jnp.clip(c_ar - wc // 2, 0, GRID_W - wc)
    row_idx = r0[:, None] + jnp.arange(wr)[None, :]
    kb = jnp.take(kl, row_idx, axis=1)
    vb = jnp.take(vl, row_idx, axis=1)
    s_loc = jnp.einsum('bijhd,birchd->bhijrc', ql, kb) * scale
    dr = row_idx - r_ar[:, None] + NA_WIN_ROWS - 1
    dc = jnp.clip(c_ar[None, :] - c_ar[:, None] + NA_WIN_COLS - 1, 0, 2 * NA_WIN_COLS - 2)
    bias = rpb[:, dr[:, None, :, None], dc[None, :, None, :]]
    in_win = (c_ar[None, :] >= c0[:, None]) & (c_ar[None, :] < c0[:, None] + wc)
    s_loc = jnp.where(in_win[:, None, :], s_loc + bias[None].astype(jnp.float32), -jnp.inf)
    s_ctx = jnp.einsum('bijhd,bkhd->bhijk', ql, kc) * scale
    nloc = wr * GRID_W
    s = jnp.concatenate([s_loc.reshape(bsz, NA_HEADS, rows, GRID_W, nloc), s_ctx], axis=-1)
    p = jax.nn.softmax(s, axis=-1)
    p_loc = p[..., :nloc].reshape(bsz, NA_HEADS, rows, GRID_W, wr, GRID_W)
    o = (jnp.einsum('bhijrc,birchd->bijhd', p_loc, vb)
         + jnp.einsum('bhijk,bkhd->bijhd', p[..., nloc:], vc))
    return out_ctx, o.reshape(bsz, n, NA_WIDTH)


def merge_branches(gate_logits, y_rw, y_hy, y_na, w_rw, w_hy, w_na, w_o):
    g_rw, g_hy, g_na = jnp.split(jax.nn.sigmoid(gate_logits.astype(jnp.float32)), N_BRANCH, axis=-1)
    m = g_rw * (y_rw @ w_rw) + g_hy * (y_hy @ w_hy) + g_na * (y_na @ w_na)
    return m @ w_o


def swiglu(h, w1, w3, w2):
    return (jax.nn.silu(h @ w1) * (h @ w3)) @ w2


def moe_swiglu(h, router, w1, w3, w2):
    logits = (h @ router).astype(jnp.float32)
    top_v, top_i = lax.top_k(logits, TOP_K)
    wts = jax.nn.softmax(top_v, axis=-1)
    gates = jnp.sum(jax.nn.one_hot(top_i, N_EXPERTS, dtype=jnp.float32) * wts[..., None], axis=-2)
    out = jnp.zeros(h.shape, jnp.float32)
    for e in range(N_EXPERTS):
        out = out + gates[..., e:e + 1] * swiglu(h, w1[e], w3[e], w2[e])
    return out.astype(h.dtype)


def setup_inputs(seed: int = 0) -> dict:
    key = jax.random.key(seed)
    keys = iter(jax.random.split(key, 64))
    f32 = jnp.float32
    L, D = DEPTH, D_MODEL

    def nrm(shape, scale):
        return jax.random.normal(next(keys), shape, f32) * scale

    def unif(shape, lo, hi):
        return jax.random.uniform(next(keys), shape, f32, lo, hi)

    return {
        'x': nrm((BATCH, SEQ, D), 1.0),
        'c': nrm((BATCH, D), 1.0),
        'ctx': nrm((BATCH, CTX_LEN, D), 1.0),
        'c_ctx': nrm((D,), 1.0),
        'ada_w': nrm((L, D, 6 * D), 0.5 * D ** -0.5),
        'ada_b': nrm((L, 6 * D), 0.02),
        'norm1_g': 1.0 + nrm((L, D), 0.05),
        'norm2_g': 1.0 + nrm((L, D), 0.05),
        'w_in': nrm((L, D, P_IN), D ** -0.5),
        'rw_shift': unif((L, 2, RW_IN), 0.0, 0.5),
        'rw_w0': unif((L, 2, RW_WIDTH), -6.0, -1.0),
        'rw_w2': nrm((L, 2, RW_DECAY_LORA, RW_WIDTH), 0.1),
        'rw_a0': nrm((L, 2, RW_WIDTH), 0.3),
        'rw_a2': nrm((L, 2, RW_ICLR_LORA, RW_WIDTH), 0.1),
        'rw_g2': nrm((L, RW_GATE_LORA, RW_WIDTH), RW_GATE_LORA ** -0.5),
        'rw_kk': 0.85 + nrm((L, RW_WIDTH), 0.05),
        'rw_ka': 1.0 + nrm((L, RW_WIDTH), 0.05),
        'rw_rk': nrm((L, RW_HEADS, RW_HEAD_DIM), 0.1),
        'rw_gn_g': 1.0 + nrm((L, RW_WIDTH), 0.05),
        'rw_gn_b': nrm((L, RW_WIDTH), 0.02),
        'hy_conv_w': nrm((L, HY_SHORT_K, HY_IN), 0.5),
        'hy_conv_b': nrm((L, HY_IN), 0.02),
        'hy_f_w1': nrm((L, HY_EMB_DIM, HY_FILTER_HIDDEN), HY_EMB_DIM ** -0.5),
        'hy_f_b1': nrm((L, HY_FILTER_HIDDEN), 0.1),
        'hy_f_w2': nrm((L, HY_FILTER_HIDDEN, HY_FILTER_HIDDEN), HY_FILTER_HIDDEN ** -0.5),
        'hy_f_b2': nrm((L, HY_FILTER_HIDDEN), 0.1),
        'hy_f_w3': nrm((L, HY_FILTER_HIDDEN, HY_ORDER * 2 * HY_WIDTH), HY_FILTER_HIDDEN ** -0.5),
        'hy_f_freq': 1.0 + nrm((L, HY_FILTER_HIDDEN), 0.05),
        'hy_skip': nrm((L, HY_ORDER, HY_WIDTH), 0.5),
        'na_q_gain': 1.0 + nrm((L, NA_HEAD_DIM), 0.05),
        'na_k_gain': 1.0 + nrm((L, NA_HEAD_DIM), 0.05),
        'na_rpb': nrm((L, NA_HEADS, 2 * NA_WIN_ROWS - 1, 2 * NA_WIN_COLS - 1), 0.1),
        'w_br_rw': nrm((L, RW_WIDTH, D), RW_WIDTH ** -0.5),
        'w_br_hy': nrm((L, HY_WIDTH, D), HY_WIDTH ** -0.5),
        'w_br_na': nrm((L, NA_WIDTH, D), NA_WIDTH ** -0.5),
        'w_out': nrm((L, D, D), D ** -0.5),
        'ff_w1': nrm((N_DENSE_LAYERS, D, FF_DENSE), D ** -0.5),
        'ff_w3': nrm((N_DENSE_LAYERS, D, FF_DENSE), D ** -0.5),
        'ff_w2': nrm((N_DENSE_LAYERS, FF_DENSE, D), FF_DENSE ** -0.5),
        'moe_router': nrm((N_MOE_LAYERS, D, N_EXPERTS), D ** -0.5),
        'moe_w1': nrm((N_MOE_LAYERS, N_EXPERTS, D, FF_EXPERT), D ** -0.5),
        'moe_w3': nrm((N_MOE_LAYERS, N_EXPERTS, D, FF_EXPERT), D ** -0.5),
        'moe_w2': nrm((N_MOE_LAYERS, N_EXPERTS, FF_EXPERT, D), FF_EXPERT ** -0.5),
    }


def reference(x, c, ctx, c_ctx, ada_w, ada_b, norm1_g, norm2_g, w_in,
              rw_shift, rw_w0, rw_w2, rw_a0, rw_a2, rw_g2, rw_kk, rw_ka, rw_rk, rw_gn_g, rw_gn_b,
              hy_conv_w, hy_conv_b, hy_f_w1, hy_f_b1, hy_f_w2, hy_f_b2, hy_f_w3, hy_f_freq, hy_skip,
              na_q_gain, na_k_gain, na_rpb, w_br_rw, w_br_hy, w_br_na, w_out,
              ff_w1, ff_w3, ff_w2, moe_router, moe_w1, moe_w3, moe_w2):
    lc = ctx.shape[1]
    act_lat = jax.nn.silu(c)
    act_ctx = jax.nn.silu(c_ctx)
    for li in range(DEPTH):
        last = li == DEPTH - 1
        mod_lat = jnp.split((act_lat @ ada_w[li] + ada_b[li])[:, None, :], 6, axis=-1)
        mod_ctx = jnp.split((act_ctx @ ada_w[li] + ada_b[li])[None, None, :], 6, axis=-1)

        h = jnp.concatenate([modulate(rms_norm(ctx, norm1_g[li]), mod_ctx[0], mod_ctx[1]),
                             modulate(rms_norm(x, norm1_g[li]), mod_lat[0], mod_lat[1])], axis=1)
        u_rw, u_hy, u_na, u_gate = jnp.split(h @ w_in[li], IN_SPLITS, axis=-1)
        rw_ctx, rw_lat = rwkv7_mixer(u_rw[:, :lc], u_rw[:, lc:], rw_shift[li], rw_w0[li], rw_w2[li],
                                     rw_a0[li], rw_a2[li], rw_g2[li], rw_kk[li], rw_ka[li], rw_rk[li],
                                     rw_gn_g[li], rw_gn_b[li], not last)
        hy_args = (hy_conv_w[li], hy_conv_b[li], hy_f_w1[li], hy_f_b1[li], hy_f_w2[li], hy_f_b2[li],
                   hy_f_w3[li], hy_f_freq[li], hy_skip[li])
        hy_lat = hyena_sequence(u_hy[:, lc:], *hy_args)
        na_ctx, na_lat = neighbourhood_attention(u_na[:, :lc], u_na[:, lc:], na_q_gain[li], na_k_gain[li],
                                                 na_rpb[li], not last)
        br_w = (w_br_rw[li], w_br_hy[li], w_br_na[li], w_out[li])
        x = x + mod_lat[2] * merge_branches(u_gate[:, lc:], rw_lat, hy_lat, na_lat, *br_w).astype(x.dtype)
        if not last:
            hy_ctx = hyena_sequence(u_hy[:, :lc], *hy_args)
            ctx = ctx + mod_ctx[2] * merge_branches(u_gate[:, :lc], rw_ctx, hy_ctx, na_ctx, *br_w).astype(ctx.dtype)

        h = modulate(rms_norm(x, norm2_g[li]), mod_lat[3], mod_lat[4])
        if not last:
            h = jnp.concatenate([modulate(rms_norm(ctx, norm2_g[li]), mod_ctx[3], mod_ctx[4]), h], axis=1)
        if li % 2 == 0:
            f = swiglu(h, ff_w1[li // 2], ff_w3[li // 2], ff_w2[li // 2])
        else:
            f = moe_swiglu(h, moe_router[li // 2], moe_w1[li // 2], moe_w3[li // 2], moe_w2[li // 2])
        if last:
            x = x + mod_lat[5] * f
        else:
            ctx = ctx + mod_ctx[5] * f[:, :lc]
            x = x + mod_lat[5] * f[:, lc:]
    return x
```

```python
import functools
import math

import numpy as np
import jax
import jax.numpy as jnp
from jax import lax
from jax.experimental import pallas as pl
from jax.experimental.pallas import tpu as pltpu

F32 = jnp.float32
BF16 = jnp.bfloat16

NORM_EPS = 1e-6
ROPE_BASE = 10000.0
GRID_W = 64
HEADS = 6
HEAD_DIM = 64
WIDTH = HEADS * HEAD_DIM
RW_LORA = 64
RW_GATE_LORA = 128
RW_GN_EPS = 64e-5
HY_WIDTH = 256
HY_ORDER = 2
HY_EMB_DIM = 33
HY_DECAY_TARGET = 1e-2
HY_FAST_DECAY = 0.3
HY_SLOW_DECAY = 1.5
NA_WIN_ROWS = 8
NA_WIN_COLS = 16
N_EXPERTS = 8
LANES = 128
SEQ_TILE = 256
HALO = 8
CHUNK = 64
VMEM_LIMIT = 56 * 1024 * 1024
NEG_BIG = -1e30


def _cparams(*sem):
    return pltpu.CompilerParams(dimension_semantics=sem, vmem_limit_bytes=VMEM_LIMIT)


def _dot(a, b):
    return jnp.dot(a.astype(BF16), b.astype(BF16), preferred_element_type=F32)


def _dot_nt(a, b):
    return lax.dot_general(a.astype(BF16), b.astype(BF16), (((1,), (1,)), ((), ())),
                           preferred_element_type=F32)


def _dot_tn(a, b):
    return lax.dot_general(a.astype(BF16), b.astype(BF16), (((0,), (0,)), ((), ())),
                           preferred_element_type=F32)


def _dot_split(a, b_exact):
    hi = a.astype(BF16)
    lo = (a - hi.astype(F32)).astype(BF16)
    return (jnp.dot(hi, b_exact, preferred_element_type=F32)
            + jnp.dot(lo, b_exact, preferred_element_type=F32))


def _sigmoid(x):
    return 1.0 / (1.0 + jnp.exp(-x))


def _silu(x):
    return x * _sigmoid(x)


def _head_ones():
    h = np.arange(WIDTH) // HEAD_DIM
    return jnp.asarray((h[:, None] == h[None, :]).astype(np.float32), dtype=BF16)


def _mm_kernel(x_ref, w_ref, o_ref, *, split):
    x = x_ref[...]
    w = w_ref[...]
    if split:
        xh = x.astype(BF16)
        xl = (x - xh.astype(F32)).astype(BF16)
        wh = w.astype(BF16)
        wl = (w - wh.astype(F32)).astype(BF16)
        o_ref[...] = (jnp.dot(xh, wh, preferred_element_type=F32)
                      + jnp.dot(xl, wh, preferred_element_type=F32)
                      + jnp.dot(xh, wl, preferred_element_type=F32))
    else:
        o_ref[...] = jnp.dot(x.astype(BF16), w.astype(BF16), preferred_element_type=F32)


def matmul(x, w, *, tn, split=False, name="matmul"):
    m, k = x.shape
    n = w.shape[1]
    assert n % tn == 0
    return pl.pallas_call(
        functools.partial(_mm_kernel, split=split),
        grid=(n // tn,),
        in_specs=[pl.BlockSpec((m, k), lambda j: (0, 0)),
                  pl.BlockSpec((k, tn), lambda j: (0, j))],
        out_specs=pl.BlockSpec((m, tn), lambda j: (0, j)),
        out_shape=jax.ShapeDtypeStruct((m, n), F32),
        compiler_params=_cparams("parallel"),
        name=name,
    )(x, w)


def _norm_mod(x, g, shift, scale):
    ms = jnp.mean(x * x, axis=-1, keepdims=True)
    h = x * lax.rsqrt(ms + NORM_EPS) * g
    return h * (1.0 + scale) + shift


def _nmm_kernel(x_ref, sh_ref, sc_ref, g_ref, w_ref, o_ref):
    h = _norm_mod(x_ref[0], g_ref[...], sh_ref[0], sc_ref[0])
    o_ref[0] = jnp.dot(h.astype(BF16), w_ref[...], preferred_element_type=F32)


def norm_mod_matmul(xs, shift, scale, g, w, *, tm=512, name="norm_mod_matmul"):
    G, L, D = xs.shape
    N = w.shape[1]
    return pl.pallas_call(
        _nmm_kernel,
        grid=(G, L // tm),
        in_specs=[pl.BlockSpec((1, tm, D), lambda g_, i: (g_, i, 0)),
                  pl.BlockSpec((1, 1, D), lambda g_, i: (g_, 0, 0)),
                  pl.BlockSpec((1, 1, D), lambda g_, i: (g_, 0, 0)),
                  pl.BlockSpec((1, D), lambda g_, i: (0, 0)),
                  pl.BlockSpec((D, N), lambda g_, i: (0, 0))],
        out_specs=pl.BlockSpec((1, tm, N), lambda g_, i: (g_, i, 0)),
        out_shape=jax.ShapeDtypeStruct((G, L, N), F32),
        compiler_params=_cparams("parallel", "parallel"),
        name=name,
    )(xs, shift, scale, g, w)


def _halo_specs(width, n_tiles):
    per = SEQ_TILE // HALO
    last = n_tiles * per - 1
    return [pl.BlockSpec((1, SEQ_TILE, width), lambda g_, i: (g_, i, 0)),
            pl.BlockSpec((1, HALO, width), lambda g_, i: (g_, jnp.maximum(i * per - 1, 0), 0)),
            pl.BlockSpec((1, HALO, width), lambda g_, i: (g_, jnp.minimum((i + 1) * per, last), 0))]


def _neighbours(u, prev_ref, next_ref):
    g_ = pl.program_id(0)
    i = pl.program_id(1)
    n_tiles = pl.num_programs(1)
    has_prev = jnp.logical_and(g_ > 0, i > 0)
    has_next = jnp.logical_and(g_ > 0, i < n_tiles - 1)
    row = lax.broadcasted_iota(jnp.int32, u.shape, 0)
    halo_p = jnp.where(has_prev, prev_ref[0, HALO - 1:HALO, :], 0.0)
    halo_n = jnp.where(has_next, next_ref[0, 0:1, :], 0.0)
    prev = jnp.where(row == 0, halo_p, pltpu.roll(u, 1, 0))
    nxt = jnp.where(row == SEQ_TILE - 1, halo_n, pltpu.roll(u, SEQ_TILE - 1, 0))
    return prev, nxt


def _rope(z, cos, sin_signed):
    lane = lax.broadcasted_iota(jnp.int32, z.shape, 1)
    partner = jnp.where(lane % 32 < 16, pltpu.roll(z, WIDTH - 16, 1), pltpu.roll(z, 16, 1))
    return z * cos + partner * sin_signed


def _rw_feat_kernel(u_ref, up_ref, un_ref, cos_ref, sin_ref, mu_ref, w0_ref, w2_ref, a0_ref, a2_ref,
                    g2_ref, kk_ref, ka_ref, ones_ref,
                    r_o, v_o, kkn_o, g_o, ld0_o, ld1_o, k0_o, k1_o, b0_o, b1_o):
    u = u_ref[0]
    prev, nxt = _neighbours(u, up_ref, un_ref)
    u = u + mu_ref[0:1, :] * (prev - u) + mu_ref[1:2, :] * (nxt - u)
    r = u[:, 0:WIDTH]
    k = u[:, WIDTH:2 * WIDTH]
    v = u[:, 2 * WIDTH:3 * WIDTH]
    o = 3 * WIDTH
    lw = u[:, o:o + 2 * RW_LORA]
    la = u[:, o + 2 * RW_LORA:o + 4 * RW_LORA]
    lg = u[:, o + 4 * RW_LORA:o + 4 * RW_LORA + RW_GATE_LORA]
    is_lat = pl.program_id(0) > 0
    cos = jnp.where(is_lat, cos_ref[...], 1.0)
    sin = jnp.where(is_lat, sin_ref[...], 0.0)
    r = _rope(r, cos, sin)
    k = _rope(k, cos, sin)
    g = _dot(_sigmoid(lg), g2_ref[...])
    kk = k * kk_ref[...]
    nrm = jnp.sqrt(_dot_split(kk * kk, ones_ref[...]))
    kk = kk / jnp.maximum(nrm, 1e-12)
    r_o[0] = r
    v_o[0] = v
    kkn_o[0] = kk
    g_o[0] = g
    tanh_lw = jnp.tanh(lw)
    for d, (ld_o, k_o, b_o) in enumerate(((ld0_o, k0_o, b0_o), (ld1_o, k1_o, b1_o))):
        zw = w0_ref[d:d + 1, :] + _dot(tanh_lw[:, d * RW_LORA:(d + 1) * RW_LORA], w2_ref[d])
        softplus = jnp.maximum(-zw, 0.0) + jnp.log(1.0 + jnp.exp(-jnp.abs(zw)))
        ld_o[0] = -jnp.exp(-softplus - 0.5)
        iclr = _sigmoid(a0_ref[d:d + 1, :] + _dot(la[:, d * RW_LORA:(d + 1) * RW_LORA], a2_ref[d]))
        k_o[0] = k * (1.0 + (iclr - 1.0) * ka_ref[...])
        b_o[0] = kk * iclr


def rwkv_features(u_rw, cos, sin, mu, w0, w2, a0, a2, g2, k_k, k_a, ones):
    G, L, W_IN = u_rw.shape
    n_tiles = L // SEQ_TILE
    full = lambda a: pl.BlockSpec(a.shape, lambda g_, i: (0,) * a.ndim)
    tab = pl.BlockSpec((SEQ_TILE, WIDTH), lambda g_, i: (i, 0))
    params = (mu, w0, w2, a0, a2, g2, k_k, k_a, ones)
    out = jax.ShapeDtypeStruct((G, L, WIDTH), F32)
    return pl.pallas_call(
        _rw_feat_kernel,
        grid=(G, n_tiles),
        in_specs=_halo_specs(W_IN, n_tiles) + [tab, tab] + [full(p) for p in params],
        out_specs=[pl.BlockSpec((1, SEQ_TILE, WIDTH), lambda g_, i: (g_, i, 0))] * 10,
        out_shape=[out] * 10,
        compiler_params=_cparams("parallel", "parallel"),
        name="rwkv_features",
    )(u_rw, u_rw, u_rw, cos, sin, *params)


def _rw_scan_kernel(r_ref, ld_ref, k_ref, v_ref, kk_ref, b_ref, y_ref, s_ref, *, reverse):
    C = CHUNK

    @pl.when(pl.program_id(1) == 0)
    def _():
        s_ref[...] = jnp.zeros_like(s_ref)

    row = lax.broadcasted_iota(jnp.int32, (C, C), 0)
    col = lax.broadcasted_iota(jnp.int32, (C, C), 1)
    if reverse:
        incl, strict = row <= col, row < col
    else:
        incl, strict = row >= col, row > col
    eye = row == col

    ld = ld_ref[0]
    ld_hi = ld.astype(BF16)
    ld_lo = (ld - ld_hi.astype(F32)).astype(BF16)
    tri = jnp.where(incl, 1.0, 0.0).astype(BF16)
    cum = (jnp.dot(tri, ld_hi, preferred_element_type=F32)
           + jnp.dot(tri, ld_lo, preferred_element_type=F32))
    tot = cum[0:1, :] if reverse else cum[C - 1:C, :]
    e_pos = jnp.exp(cum)
    e_neg = jnp.exp(-cum)
    e_rem = jnp.exp(tot - cum)
    kk = kk_ref[0]
    b = b_ref[0]
    k = k_ref[0]
    a_t = (-kk * jnp.exp(cum - ld)).astype(BF16)
    r_t = (r_ref[0] * e_pos).astype(BF16)
    b_t = (b * e_neg).astype(BF16)
    k_t = (k * e_neg).astype(BF16)
    b_p = (b * e_rem).astype(BF16)
    k_p = (k * e_rem).astype(BF16)
    p_c = jnp.exp(tot)
    v = v_ref[0].astype(BF16)

    for h in range(HEADS):
        sl = slice(h * HEAD_DIM, (h + 1) * HEAD_DIM)
        ar = jnp.concatenate([a_t[:, sl], r_t[:, sl]], axis=0)
        g_b = _dot_nt(ar, b_t[:, sl])
        g_k = _dot_nt(ar, k_t[:, sl])
        a_ab = jnp.where(strict, g_b[:C], 0.0)
        a_rb = jnp.where(incl, g_b[C:], 0.0)
        a_ak = jnp.where(strict, g_k[:C], 0.0)
        a_rk = jnp.where(incl, g_k[C:], 0.0)
        t_inv = jnp.where(eye, 1.0, a_ab)
        pw = a_ab
        for _ in range(int(math.log2(C)) - 1):
            pw = _dot(pw, pw)
            t_inv = t_inv + _dot(t_inv, pw)
        av = _dot(jnp.concatenate([a_ak, a_rk], axis=0), v[:, sl])
        w1 = _dot(t_inv, a_t[:, sl])
        w2 = _dot(t_inv, av[:C])
        q_p = r_t[:, sl].astype(F32) + _dot(a_rb, w1)
        y_p = av[C:] + _dot(a_rb, w2)
        m_mat = jnp.where(eye, p_c[:, sl], 0.0) + _dot_tn(w1, b_p[:, sl])
        n_mat = _dot_tn(w2, b_p[:, sl]) + _dot_tn(v[:, sl], k_p[:, sl])
        s0 = s_ref[h]
        y_ref[0, :, sl] = _dot_nt(q_p, s0) + y_p
        s_ref[h] = _dot(s0, m_mat) + n_mat


def rwkv_scan(r, ld, k, v, kk, b, *, batch, reverse):
    G, L, _ = r.shape
    lc = L // batch
    nc_ctx = lc // CHUNK
    nc_lat = L // CHUNK
    n_steps = nc_ctx + nc_lat

    def idx(bi, s):
        in_ctx = s < nc_ctx
        if reverse:
            c_ctx = nc_ctx - 1 - s
            c_lat = n_steps - 1 - s
        else:
            c_ctx = s
            c_lat = s - nc_ctx
        grp = jnp.where(in_ctx, 0, bi + 1)
        blk = jnp.where(in_ctx, bi * nc_ctx + c_ctx, c_lat)
        return (grp, blk, 0)

    spec = pl.BlockSpec((1, CHUNK, WIDTH), idx)
    return pl.pallas_call(
        functools.partial(_rw_scan_kernel, reverse=reverse),
        grid=(batch, n_steps),
        in_specs=[spec] * 6,
        out_specs=spec,
        out_shape=jax.ShapeDtypeStruct((G, L, WIDTH), F32),
        scratch_shapes=[pltpu.VMEM((HEADS, HEAD_DIM, HEAD_DIM), F32)],
        compiler_params=_cparams("parallel", "arbitrary"),
        name="rwkv_scan_rev" if reverse else "rwkv_scan_fwd",
    )(r, ld, k, v, kk, b)


def _rw_readout_kernel(y0_ref, y1_ref, r_ref, k0_ref, k1_ref, v_ref, g_ref, rk_ref, gg_ref, gb_ref,
                       ones_ref, o_ref):
    y = y0_ref[0] + y1_ref[0]
    ones = ones_ref[...]
    inv_n = 1.0 / HEAD_DIM
    mu = _dot_split(y, ones) * inv_n
    yc = y - mu
    var = _dot_split(yc * yc, ones) * inv_n
    yn = yc * lax.rsqrt(var + RW_GN_EPS) * gg_ref[...] + gb_ref[...]
    bonus = _dot_split(r_ref[0] * (k0_ref[0] + k1_ref[0]) * rk_ref[...], ones) * v_ref[0]
    o_ref[0] = (yn + bonus) * g_ref[0]


def rwkv_readout(y0, y1, r, k0, k1, v, g, r_k, gn_g, gn_b, ones, *, tm=512):
    G, L, _ = r.shape
    spec = pl.BlockSpec((1, tm, WIDTH), lambda g_, i: (g_, i, 0))
    full = lambda a: pl.BlockSpec(a.shape, lambda g_, i: (0,) * a.ndim)
    params = (r_k, gn_g, gn_b, ones)
    return pl.pallas_call(
        _rw_readout_kernel,
        grid=(G, L // tm),
        in_specs=[spec] * 7 + [full(p) for p in params],
        out_specs=spec,
        out_shape=jax.ShapeDtypeStruct((G, L, WIDTH), F32),
        compiler_params=_cparams("parallel", "parallel"),
        name="rwkv_readout",
    )(y0, y1, r, k0, k1, v, g, *params)


def _hy_short_kernel(u_ref, up_ref, un_ref, w_ref, b_ref, o_ref):
    u = u_ref[0]
    prev, nxt = _neighbours(u, up_ref, un_ref)
    o_ref[0] = b_ref[...] + w_ref[0:1, :] * prev + w_ref[1:2, :] * u + w_ref[2:3, :] * nxt


def hyena_short_conv(u_hy, w, b):
    G, L, Wd = u_hy.shape
    n_tiles = L // SEQ_TILE
    full = lambda a: pl.BlockSpec(a.shape, lambda g_, i: (0,) * a.ndim)
    return pl.pallas_call(
        _hy_short_kernel,
        grid=(G, n_tiles),
        in_specs=_halo_specs(Wd, n_tiles) + [full(w), full(b)],
        out_specs=pl.BlockSpec((1, SEQ_TILE, Wd), lambda g_, i: (g_, i, 0)),
        out_shape=jax.ShapeDtypeStruct((G, L, Wd), F32),
        compiler_params=_cparams("parallel", "parallel"),
        name="hyena_short_conv",
    )(u_hy, u_hy, u_hy, w, b)


@functools.lru_cache(maxsize=None)
def _dft_tables(n):
    nn = 2 * n
    nf = n + 256
    kf = np.arange(nf, dtype=np.int64)
    t = np.arange(n, dtype=np.int64)
    ang = 2.0 * np.pi * ((kf[:, None] * t[None, :]) % nn).astype(np.float64) / nn
    valid = (kf <= n)[:, None]
    cosm = np.where(valid, np.cos(ang), 0.0)
    sinm = np.where(valid, np.sin(ang), 0.0)
    ck = np.where((kf == 0) | (kf == n), 1.0, 2.0)[:, None] / nn
    fwd = np.stack([cosm, -sinm])
    inv = np.stack([(ck * cosm).T, (-ck * sinm).T])
    return fwd, inv, nf


def _hy_conv_kernel(z_ref, x_ref, fwd_ref, inv_ref, hre_ref, him_ref, skip_ref, o_ref, acc_ref):
    f = pl.program_id(1)

    @pl.when(f == 0)
    def _():
        acc_ref[...] = jnp.zeros_like(acc_ref)

    z = z_ref[0]
    zb = z.astype(BF16)
    zre = jnp.dot(fwd_ref[0], zb, preferred_element_type=F32)
    zim = jnp.dot(fwd_ref[1], zb, preferred_element_type=F32)
    hre = hre_ref[0]
    him = him_ref[0]
    yre = zre * hre - zim * him
    yim = zre * him + zim * hre
    acc_ref[...] += (jnp.dot(inv_ref[0], yre.astype(BF16), preferred_element_type=F32)
                     + jnp.dot(inv_ref[1], yim.astype(BF16), preferred_element_type=F32))

    @pl.when(f == pl.num_programs(1) - 1)
    def _():
        o_ref[0] = x_ref[0] * (acc_ref[...] + skip_ref[0] * z)


def hyena_long_conv(z_arr, z_blk, x_arr, x_blk, hre, him, skip, order, fwd, inv, *, n_seq, n, tf):
    nf = fwd.shape[1]
    zo, zc = z_blk
    xo, xc = x_blk
    return pl.pallas_call(
        _hy_conv_kernel,
        grid=(n_seq, nf // tf),
        in_specs=[pl.BlockSpec((1, n, HY_WIDTH), lambda s, f: (s + zo, 0, zc)),
                  pl.BlockSpec((1, n, HY_WIDTH), lambda s, f: (s + xo, 0, xc)),
                  pl.BlockSpec((2, tf, n), lambda s, f: (0, f, 0)),
                  pl.BlockSpec((2, n, tf), lambda s, f: (0, 0, f)),
                  pl.BlockSpec((1, tf, HY_WIDTH), lambda s, f: (order, f, 0)),
                  pl.BlockSpec((1, tf, HY_WIDTH), lambda s, f: (order, f, 0)),
                  pl.BlockSpec((1, 1, HY_WIDTH), lambda s, f: (order, 0, 0))],
        out_specs=pl.BlockSpec((1, n, HY_WIDTH), lambda s, f: (s, 0, 0)),
        out_shape=jax.ShapeDtypeStruct((n_seq, n, HY_WIDTH), F32),
        scratch_shapes=[pltpu.VMEM((n, HY_WIDTH), F32)],
        compiler_params=_cparams("parallel", "arbitrary"),
        name=f"hyena_long_conv_n{n}_o{order}",
    )(z_arr, x_arr, fwd, inv, hre, him, skip)


def hyena_filter_spectrum(n, w1, b1, w2, b2, w3, freq, fwd):
    hp = lax.Precision.HIGHEST
    pos = jnp.arange(n, dtype=F32)
    t = pos / max(n - 1, 1)
    bands = (HY_EMB_DIM - 1) // 2
    fr = jnp.linspace(1e-4, bands - 1, bands, dtype=F32)
    ang = (2 * math.pi / n) * pos[:, None] * fr[None, :]
    z = jnp.concatenate([t[:, None], jnp.cos(ang), -jnp.sin(ang)], axis=-1)
    h = jnp.sin(freq * (jnp.dot(z, w1, precision=hp) + b1))
    h = jnp.sin(freq * (jnp.dot(h, w2, precision=hp) + b2))
    h = matmul(h, w3, tn=w3.shape[1], split=True, name="hyena_filter_out")
    h = h.reshape(n, HY_ORDER, 2, HY_WIDTH)
    max_decay = math.log(HY_DECAY_TARGET) / HY_FAST_DECAY
    min_decay = math.log(HY_DECAY_TARGET) / HY_SLOW_DECAY
    deltas = jnp.abs(jnp.linspace(min_decay, max_decay, HY_WIDTH, dtype=F32))
    h = h * jnp.exp(-t[:, None] * deltas)[:, None, None, :]
    h_fwd = h[:, :, 0]
    h_bwd = h[:, :, 1] * (pos > 0).astype(F32)[:, None, None]
    l1 = jnp.sum(jnp.abs(h_fwd), axis=0) + jnp.sum(jnp.abs(h_bwd), axis=0)
    even = ((h_fwd + h_bwd) / l1).reshape(n, HY_ORDER * HY_WIDTH)
    odd = ((h_fwd - h_bwd) / l1).reshape(n, HY_ORDER * HY_WIDTH)
    nf = fwd.shape[1]
    hre = matmul(fwd[0], even, tn=HY_ORDER * HY_WIDTH, name="hyena_filter_dft_re")
    him = matmul(fwd[1], odd, tn=HY_ORDER * HY_WIDTH, name="hyena_filter_dft_im")
    to_ofc = lambda a: jnp.moveaxis(a.reshape(nf, HY_ORDER, HY_WIDTH), 1, 0)
    return to_ofc(hre), to_ofc(him)


def _na_prep_kernel(u_ref, qg_ref, kg_ref, ones_ref, q_o, k_o, v_o):
    u = u_ref[0]
    ones = ones_ref[...]
    q = u[:, 0:WIDTH]
    k = u[:, WIDTH:2 * WIDTH]
    inv_n = 1.0 / HEAD_DIM
    qn = q * lax.rsqrt(_dot_split(q * q, ones) * inv_n + NORM_EPS) * qg_ref[...]
    kn = k * lax.rsqrt(_dot_split(k * k, ones) * inv_n + NORM_EPS) * kg_ref[...]
    q_o[0] = (qn * (HEAD_DIM ** -0.5)).astype(BF16)
    k_o[0] = kn.astype(BF16)
    v_o[0] = u[:, 2 * WIDTH:3 * WIDTH].astype(BF16)


def na_prepare(u_na, q_gain, k_gain, ones, *, tm=512):
    G, L, Wd = u_na.shape
    full = lambda a: pl.BlockSpec(a.shape, lambda g_, i: (0,) * a.ndim)
    spec = pl.BlockSpec((1, tm, WIDTH), lambda g_, i: (g_, i, 0))
    out = jax.ShapeDtypeStruct((G, L, WIDTH), BF16)
    return pl.pallas_call(
        _na_prep_kernel,
        grid=(G, L // tm),
        in_specs=[pl.BlockSpec((1, tm, Wd), lambda g_, i: (g_, i, 0)), full(q_gain), full(k_gain), full(ones)],
        out_specs=[spec] * 3,
        out_shape=[out] * 3,
        compiler_params=_cparams("parallel", "parallel"),
        name="na_prepare",
    )(u_na, q_gain, k_gain, ones)


def _na_lat_kernel(q_ref, k_ref, v_ref, kc_ref, vc_ref, bias_ref, o_ref, *, rows, wr):
    i = pl.program_id(1)
    r0 = jnp.clip(i - wr // 2, 0, rows - wr)
    off = r0 - i + (NA_WIN_ROWS - 1)
    start = pl.multiple_of(r0 * GRID_W, GRID_W)
    q = q_ref[0]
    kw = k_ref[0, pl.ds(start, wr * GRID_W), :]
    vw = v_ref[0, pl.ds(start, wr * GRID_W), :]
    kc = kc_ref[0]
    vc = vc_ref[0]
    for h in range(HEADS):
        sl = slice(h * HEAD_DIM, (h + 1) * HEAD_DIM)
        s_loc = _dot_nt(q[:, sl], kw[:, sl]) + bias_ref[h, off]
        s_ctx = _dot_nt(q[:, sl], kc[:, sl])
        m = jnp.maximum(jnp.max(s_loc, axis=-1, keepdims=True), jnp.max(s_ctx, axis=-1, keepdims=True))
        p_loc = jnp.exp(s_loc - m)
        p_ctx = jnp.exp(s_ctx - m)
        den = jnp.sum(p_loc, axis=-1, keepdims=True) + jnp.sum(p_ctx, axis=-1, keepdims=True)
        o = _dot(p_loc, vw[:, sl]) + _dot(p_ctx, vc[:, sl])
        o_ref[0, :, sl] = o / den


def _na_bias_table(rpb, rows, wr):
    offs = np.arange(NA_WIN_ROWS) - (NA_WIN_ROWS - 1)
    dr = np.clip(offs[:, None] + np.arange(wr)[None, :] + NA_WIN_ROWS - 1, 0, 2 * NA_WIN_ROWS - 2)
    c_ar = np.arange(GRID_W)
    dc = np.clip(c_ar[None, :] - c_ar[:, None] + NA_WIN_COLS - 1, 0, 2 * NA_WIN_COLS - 2)
    c0 = np.clip(c_ar - NA_WIN_COLS // 2, 0, GRID_W - NA_WIN_COLS)
    in_win = (c_ar[None, :] >= c0[:, None]) & (c_ar[None, :] < c0[:, None] + NA_WIN_COLS)
    bias = rpb[:, dr[:, None, :, None], dc[None, :, None, :]]
    bias = jnp.where(in_win[None, None, :, None, :], bias, NEG_BIG)
    return bias.reshape(HEADS, NA_WIN_ROWS, GRID_W, wr * GRID_W).astype(F32)


def na_latent(qn, kn, vb, bias, *, batch):
    G, L, _ = qn.shape
    lc = L // batch
    rows = L // GRID_W
    wr = min(NA_WIN_ROWS, rows)
    qspec = pl.BlockSpec((1, GRID_W, WIDTH), lambda b_, i: (b_ + 1, i, 0))
    seq = pl.BlockSpec((1, L, WIDTH), lambda b_, i: (b_ + 1, 0, 0))
    ctx = pl.BlockSpec((1, lc, WIDTH), lambda b_, i: (0, b_, 0))
    return pl.pallas_call(
        functools.partial(_na_lat_kernel, rows=rows, wr=wr),
        grid=(batch, rows),
        in_specs=[qspec, seq, seq, ctx, ctx, pl.BlockSpec(bias.shape, lambda b_, i: (0, 0, 0, 0))],
        out_specs=pl.BlockSpec((1, GRID_W, WIDTH), lambda b_, i: (b_, i, 0)),
        out_shape=jax.ShapeDtypeStruct((batch, L, WIDTH), F32),
        compiler_params=_cparams("parallel", "arbitrary"),
        name="na_latent",
    )(qn, kn, vb, kn, vb, bias)


def _na_ctx_kernel(q_ref, k_ref, v_ref, o_ref):
    q = q_ref[0]
    k = k_ref[0]
    v = v_ref[0]
    for h in range(HEADS):
        sl = slice(h * HEAD_DIM, (h + 1) * HEAD_DIM)
        s = _dot_nt(q[:, sl], k[:, sl])
        p = jnp.exp(s - jnp.max(s, axis=-1, keepdims=True))
        o_ref[0, :, sl] = _dot(p, v[:, sl]) / jnp.sum(p, axis=-1, keepdims=True)


def na_context(qn, kn, vb, *, batch):
    G, L, _ = qn.shape
    lc = L // batch
    spec = pl.BlockSpec((1, lc, WIDTH), lambda b_: (0, b_, 0))
    return pl.pallas_call(
        _na_ctx_kernel,
        grid=(batch,),
        in_specs=[spec] * 3,
        out_specs=spec,
        out_shape=jax.ShapeDtypeStruct((1, L, WIDTH), F32),
        compiler_params=_cparams("parallel"),
        name="na_context",
    )(qn, kn, vb)


def _merge_kernel(x_ref, gl_ref, rw_ref, hy_ref, na_ref, g1_ref, wr_ref, wh_ref, wn_ref, wo_ref, o_ref):
    D = x_ref.shape[-1]
    gl = gl_ref[0]
    m = (_sigmoid(gl[:, 0:D]) * _dot(rw_ref[0], wr_ref[...])
         + _sigmoid(gl[:, D:2 * D]) * _dot(hy_ref[0], wh_ref[...])
         + _sigmoid(gl[:, 2 * D:3 * D]) * _dot(na_ref[0], wn_ref[...]))
    o_ref[0] = x_ref[0] + g1_ref[0] * _dot(m, wo_ref[...])


def merge_branches(xs, u_gate, y_rw, y_hy, y_na, gate1, w_rw, w_hy, w_na, w_o, *, g_off, tm=512):
    G, L, D = xs.shape
    tok = lambda width: pl.BlockSpec((1, tm, width), lambda g_, i: (g_ + g_off, i, 0))
    full = lambda a: pl.BlockSpec(a.shape, lambda g_, i: (0,) * a.ndim)
    weights = (w_rw, w_hy, w_na, w_o)
    return pl.pallas_call(
        _merge_kernel,
        grid=(G - g_off, L // tm),
        in_specs=[tok(D), tok(3 * D), tok(WIDTH), tok(HY_WIDTH), tok(WIDTH),
                  pl.BlockSpec((1, 1, D), lambda g_, i: (g_ + g_off, 0, 0))] + [full(w) for w in weights],
        out_specs=pl.BlockSpec((1, tm, D), lambda g_, i: (g_, i, 0)),
        out_shape=jax.ShapeDtypeStruct((G - g_off, L, D), F32),
        compiler_params=_cparams("parallel", "parallel"),
        name="merge_branches",
    )(xs, u_gate, y_rw, y_hy, y_na, gate1, *weights)


def _ffn_kernel(x_ref, sh_ref, sc_ref, g_ref, g2_ref, w1_ref, w3_ref, w2_ref, o_ref, h_ref, acc_ref):
    f = pl.program_id(2)

    @pl.when(f == 0)
    def _():
        h_ref[...] = _norm_mod(x_ref[0], g_ref[...], sh_ref[0], sc_ref[0]).astype(BF16)
        acc_ref[...] = jnp.zeros_like(acc_ref)

    h = h_ref[...]
    a = jnp.dot(h, w1_ref[...], preferred_element_type=F32)
    b = jnp.dot(h, w3_ref[...], preferred_element_type=F32)
    acc_ref[...] += _dot(_silu(a) * b, w2_ref[...])

    @pl.when(f == pl.num_programs(2) - 1)
    def _():
        o_ref[0] = x_ref[0] + g2_ref[0] * acc_ref[...]


def ffn_dense(xs, shift, scale, g, gate2, w1, w3, w2, *, tm=512, tf):
    G, L, D = xs.shape
    FF = w1.shape[1]
    mod = pl.BlockSpec((1, 1, D), lambda g_, i, f: (g_, 0, 0))
    return pl.pallas_call(
        _ffn_kernel,
        grid=(G, L // tm, FF // tf),
        in_specs=[pl.BlockSpec((1, tm, D), lambda g_, i, f: (g_, i, 0)), mod, mod,
                  pl.BlockSpec((1, D), lambda g_, i, f: (0, 0)), mod,
                  pl.BlockSpec((D, tf), lambda g_, i, f: (0, f)),
                  pl.BlockSpec((D, tf), lambda g_, i, f: (0, f)),
                  pl.BlockSpec((tf, D), lambda g_, i, f: (f, 0))],
        out_specs=pl.BlockSpec((1, tm, D), lambda g_, i, f: (g_, i, 0)),
        out_shape=jax.ShapeDtypeStruct((G, L, D), F32),
        scratch_shapes=[pltpu.VMEM((tm, D), BF16), pltpu.VMEM((tm, D), F32)],
        compiler_params=_cparams("parallel", "parallel", "arbitrary"),
        name="ffn_dense",
    )(xs, shift, scale, g, gate2, w1, w3, w2)


def _moe_kernel(x_ref, sh_ref, sc_ref, g_ref, g2_ref, rt_ref, w1_ref, w3_ref, w2_ref, o_ref,
                h_ref, gates_ref, acc_ref):
    e = pl.program_id(2)
    f = pl.program_id(3)

    @pl.when(jnp.logical_and(e == 0, f == 0))
    def _():
        h = _norm_mod(x_ref[0], g_ref[...], sh_ref[0], sc_ref[0])
        h_ref[...] = h.astype(BF16)
        acc_ref[...] = jnp.zeros_like(acc_ref)
        rt = rt_ref[...]
        hh = h.astype(BF16)
        hl = (h - hh.astype(F32)).astype(BF16)
        rh = rt.astype(BF16)
        rl = (rt - rh.astype(F32)).astype(BF16)
        logits = (jnp.dot(hh, rh, preferred_element_type=F32) + jnp.dot(hl, rh, preferred_element_type=F32)
                  + jnp.dot(hh, rl, preferred_element_type=F32))
        lane = lax.broadcasted_iota(jnp.int32, logits.shape, 1)
        logits = jnp.where(lane < N_EXPERTS, logits, NEG_BIG)
        m1 = jnp.max(logits, axis=-1, keepdims=True)
        i1 = jnp.min(jnp.where(logits == m1, lane, LANES), axis=-1, keepdims=True)
        rest = jnp.where(lane == i1, NEG_BIG, logits)
        m2 = jnp.max(rest, axis=-1, keepdims=True)
        i2 = jnp.min(jnp.where(rest == m2, lane, LANES), axis=-1, keepdims=True)
        e2 = jnp.exp(m2 - m1)
        gates_ref[...] = (jnp.where(lane == i1, 1.0 / (1.0 + e2), 0.0)
                          + jnp.where(lane == i2, e2 / (1.0 + e2), 0.0))

    h = h_ref[...]
    a = jnp.dot(h, w1_ref[0], preferred_element_type=F32)
    b = jnp.dot(h, w3_ref[0], preferred_element_type=F32)
    gates = gates_ref[...]
    lane = lax.broadcasted_iota(jnp.int32, gates.shape, 1)
    ge = jnp.sum(jnp.where(lane == e, gates, 0.0), axis=-1, keepdims=True)
    acc_ref[...] += _dot(_silu(a) * b * ge, w2_ref[0])

    @pl.when(jnp.logical_and(e == pl.num_programs(2) - 1, f == pl.num_programs(3) - 1))
    def _():
        o_ref[0] = x_ref[0] + g2_ref[0] * acc_ref[...]


def moe_ffn(xs, shift, scale, g, gate2, router, w1, w3, w2, *, g_off, tm=512, tf):
    G, L, D = xs.shape
    E, _, FF = w1.shape
    mod = pl.BlockSpec((1, 1, D), lambda g_, i, e, f: (g_ + g_off, 0, 0))
    return pl.pallas_call(
        _moe_kernel,
        grid=(G - g_off, L // tm, E, FF // tf),
        in_specs=[pl.BlockSpec((1, tm, D), lambda g_, i, e, f: (g_ + g_off, i, 0)), mod, mod,
                  pl.BlockSpec((1, D), lambda g_, i, e, f: (0, 0)), mod,
                  pl.BlockSpec((D, LANES), lambda g_, i, e, f: (0, 0)),
                  pl.BlockSpec((1, D, tf), lambda g_, i, e, f: (e, 0, f)),
                  pl.BlockSpec((1, D, tf), lambda g_, i, e, f: (e, 0, f)),
                  pl.BlockSpec((1, tf, D), lambda g_, i, e, f: (e, f, 0))],
        out_specs=pl.BlockSpec((1, tm, D), lambda g_, i, e, f: (g_, i, 0)),
        out_shape=jax.ShapeDtypeStruct((G - g_off, L, D), F32),
        scratch_shapes=[pltpu.VMEM((tm, D), BF16), pltpu.VMEM((tm, LANES), F32), pltpu.VMEM((tm, D), F32)],
        compiler_params=_cparams("parallel", "parallel", "arbitrary", "arbitrary"),
        name="moe_ffn",
    )(xs, shift, scale, g, gate2, router, w1, w3, w2)


def _rope_tables(n):
    half = HEAD_DIM // 2
    nf = half // 2
    t = np.arange(n)
    inv = ROPE_BASE ** (-np.arange(nf, dtype=np.float64) / nf)
    d = np.arange(HEAD_DIM)
    pos = np.where(d[None, :] < half, (t // GRID_W)[:, None], (t % GRID_W)[:, None]).astype(np.float64)
    ang = pos * inv[d % nf][None, :]
    sign = np.where(d % half < nf, -1.0, 1.0)
    cos = np.tile(np.cos(ang), (1, HEADS))
    sin = np.tile(np.sin(ang) * sign[None, :], (1, HEADS))
    return jnp.asarray(cos, dtype=F32), jnp.asarray(sin, dtype=F32)


def kernel(x, c, ctx, c_ctx, ada_w, ada_b, norm1_g, norm2_g, w_in, rw_shift, rw_w0, rw_w2, rw_a0, rw_a2, rw_g2, rw_kk, rw_ka, rw_rk, rw_gn_g, rw_gn_b, hy_conv_w, hy_conv_b, hy_f_w1, hy_f_b1, hy_f_w2, hy_f_b2, hy_f_w3, hy_f_freq, hy_skip, na_q_gain, na_k_gain, na_rpb, w_br_rw, w_br_hy, w_br_na, w_out, ff_w1, ff_w3, ff_w2, moe_router, moe_w1, moe_w3, moe_w2):
    B, L, D = x.shape
    lc = ctx.shape[1]
    depth = ada_w.shape[0]
    assert B * lc == L and lc == SEQ_TILE and L % GRID_W == 0
    G = B + 1
    rows = L // GRID_W
    wr = min(NA_WIN_ROWS, rows)
    rw_in = 3 * WIDTH + 4 * RW_LORA + RW_GATE_LORA
    hy_in = (HY_ORDER + 1) * HY_WIDTH
    na_in = 3 * WIDTH
    splits = (rw_in, rw_in + hy_in, rw_in + hy_in + na_in)

    xs = jnp.concatenate([ctx.reshape(1, L, D), x], axis=0)
    act = jnp.concatenate([c_ctx[None, :], c], axis=0)
    act = act * _sigmoid(act)
    act = jnp.pad(act, ((0, (-G) % 16), (0, 0)))
    ones = _head_ones()
    cos, sin = _rope_tables(L)
    fwd_lat, inv_lat, _ = _dft_tables(L)
    fwd_ctx, inv_ctx, _ = _dft_tables(lc)
    fwd_lat, inv_lat, fwd_ctx, inv_ctx = (jnp.asarray(a, dtype=F32).astype(BF16)
                                          for a in (fwd_lat, inv_lat, fwd_ctx, inv_ctx))
    row2 = lambda a: a.reshape(1, -1)

    for li in range(depth):
        last = li == depth - 1
        mods = matmul(act, ada_w[li], tn=1024, split=True, name="adaln")[:G] + ada_b[li]
        sh1, sc1, g1, sh2, sc2, g2 = (m.reshape(G, 1, D) for m in jnp.split(mods, 6, axis=-1))

        w_in_b = w_in[li].astype(BF16)
        n1 = row2(norm1_g[li])
        proj = lambda lo, hi, nm: norm_mod_matmul(xs, sh1, sc1, n1, w_in_b[:, lo:hi], name=nm)
        u_rw = proj(0, splits[0], "proj_rw")
        u_hy = proj(splits[0], splits[1], "proj_hy")
        u_na = proj(splits[1], splits[2], "proj_na")
        u_gate = proj(splits[2], w_in_b.shape[1], "proj_gate")

        r, v, kk, g, ld0, ld1, k0, k1, b0, b1 = rwkv_features(
            u_rw, cos, sin, rw_shift[li], rw_w0[li], rw_w2[li].astype(BF16), rw_a0[li], rw_a2[li].astype(BF16),
            rw_g2[li].astype(BF16), row2(rw_kk[li]), row2(rw_ka[li]), ones)
        y0 = rwkv_scan(r, ld0, k0, v, kk, b0, batch=B, reverse=False)
        y1 = rwkv_scan(r, ld1, k1, v, kk, b1, batch=B, reverse=True)
        y_rw = rwkv_readout(y0, y1, r, k0, k1, v, g, row2(rw_rk[li]), row2(rw_gn_g[li]), row2(rw_gn_b[li]), ones)

        hy = hyena_short_conv(u_hy, hy_conv_w[li], row2(hy_conv_b[li]))
        f_args = (hy_f_w1[li], hy_f_b1[li], hy_f_w2[li], hy_f_b2[li], hy_f_w3[li], hy_f_freq[li])
        skip = hy_skip[li].reshape(HY_ORDER, 1, HY_WIDTH)
        hre, him = hyena_filter_spectrum(L, *f_args, fwd_lat)
        conv = functools.partial(hyena_long_conv, skip=skip, fwd=fwd_lat, inv=inv_lat, n_seq=B, n=L, tf=384)
        z1 = conv(hy, (1, 0), hy, (1, 1), hre, him, order=0)
        y_hy = conv(z1, (0, 0), hy, (1, 2), hre, him, order=1)
        if last:
            y_hy = jnp.concatenate([jnp.zeros((1, L, HY_WIDTH), F32), y_hy], axis=0)
        else:
            hy_c = hy[0].reshape(B, lc, hy_in)
            hre_c, him_c = hyena_filter_spectrum(lc, *f_args, fwd_ctx)
            conv_c = functools.partial(hyena_long_conv, skip=skip, fwd=fwd_ctx, inv=inv_ctx, n_seq=B, n=lc,
                                       tf=fwd_ctx.shape[1])
            z1c = conv_c(hy_c, (0, 0), hy_c, (0, 1), hre_c, him_c, order=0)
            y_hy_c = conv_c(z1c, (0, 0), hy_c, (0, 2), hre_c, him_c, order=1)
            y_hy = jnp.concatenate([y_hy_c.reshape(1, L, HY_WIDTH), y_hy], axis=0)

        qn, kn, vb = na_prepare(u_na, row2(jnp.tile(na_q_gain[li], HEADS)), row2(jnp.tile(na_k_gain[li], HEADS)), ones)
        bias = _na_bias_table(na_rpb[li], rows, wr)
        y_na = na_latent(qn, kn, vb, bias, batch=B)
        y_na_c = jnp.zeros((1, L, WIDTH), F32) if last else na_context(qn, kn, vb, batch=B)
        y_na = jnp.concatenate([y_na_c, y_na], axis=0)

        g_off = 1 if last else 0
        xs = merge_branches(xs, u_gate, y_rw, y_hy, y_na, g1, w_br_rw[li].astype(BF16), w_br_hy[li].astype(BF16),
                            w_br_na[li].astype(BF16), w_out[li].astype(BF16), g_off=g_off)
        if last:
            sh2, sc2, g2 = sh2[1:], sc2[1:], g2[1:]

        n2 = row2(norm2_g[li])
        if li % 2 == 0:
            j = li // 2
            xs = ffn_dense(xs, sh2, sc2, n2, g2, ff_w1[j].astype(BF16), ff_w3[j].astype(BF16), ff_w2[j].astype(BF16),
                           tf=ff_w1.shape[2] // 2)
        else:
            j = li // 2
            router = jnp.pad(moe_router[j], ((0, 0), (0, LANES - N_EXPERTS)))
            xs = moe_ffn(xs, sh2, sc2, n2, g2, router, moe_w1[j].astype(BF16), moe_w3[j].astype(BF16),
                         moe_w2[j].astype(BF16), g_off=0, tf=moe_w1.shape[3] // 2)
    return xs if depth == 0 else xs[-B:]
```

```python
import functools
import math

import numpy as np
import jax
import jax.numpy as jnp
from jax import lax
from jax.experimental import pallas as pl
from jax.experimental.pallas import tpu as pltpu

F32 = jnp.float32
BF16 = jnp.bfloat16

NORM_EPS = 1e-6
ROPE_BASE = 10000.0
GRID_W = 64
HEADS = 6
HEAD_DIM = 64
WIDTH = HEADS * HEAD_DIM
RW_LORA = 64
RW_GATE_LORA = 128
RW_GN_EPS = 64e-5
HY_WIDTH = 256
HY_ORDER = 2
HY_EMB_DIM = 33
HY_DECAY_TARGET = 1e-2
HY_FAST_DECAY = 0.3
HY_SLOW_DECAY = 1.5
NA_WIN_ROWS = 8
NA_WIN_COLS = 16
N_EXPERTS = 8
LANES = 128
SEQ_TILE = 256
HALO = 8
CHUNK = 64
VMEM_LIMIT = 56 * 1024 * 1024
NEG_BIG = -1e30


def _cparams(*sem):
    return pltpu.CompilerParams(dimension_semantics=sem, vmem_limit_bytes=VMEM_LIMIT)


def _dot(a, b):
    return jnp.dot(a.astype(BF16), b.astype(BF16), preferred_element_type=F32)


def _dot_nt(a, b):
    return lax.dot_general(a.astype(BF16), b.astype(BF16), (((1,), (1,)), ((), ())),
                           preferred_element_type=F32)


def _dot_tn(a, b):
    return lax.dot_general(a.astype(BF16), b.astype(BF16), (((0,), (0,)), ((), ())),
                           preferred_element_type=F32)


def _dot_split(a, b_exact):
    hi = a.astype(BF16)
    lo = (a - hi.astype(F32)).astype(BF16)
    return (jnp.dot(hi, b_exact, preferred_element_type=F32)
            + jnp.dot(lo, b_exact, preferred_element_type=F32))


def _sigmoid(x):
    return 1.0 / (1.0 + jnp.exp(-x))


def _silu(x):
    return x * _sigmoid(x)


def _head_ones():
    h = np.arange(WIDTH) // HEAD_DIM
    return jnp.asarray((h[:, None] == h[None, :]).astype(np.float32), dtype=BF16)


def _mm_kernel(x_ref, w_ref, o_ref, *, split):
    x = x_ref[...]
    w = w_ref[...]
    if split:
        xh = x.astype(BF16)
        xl = (x - xh.astype(F32)).astype(BF16)
        wh = w.astype(BF16)
        wl = (w - wh.astype(F32)).astype(BF16)
        o_ref[...] = (jnp.dot(xh, wh, preferred_element_type=F32)
                      + jnp.dot(xl, wh, preferred_element_type=F32)
                      + jnp.dot(xh, wl, preferred_element_type=F32))
    else:
        o_ref[...] = jnp.dot(x.astype(BF16), w.astype(BF16), preferred_element_type=F32)


def matmul(x, w, *, tn, split=False, name="matmul"):
    m, k = x.shape
    n = w.shape[1]
    assert n % tn == 0
    return pl.pallas_call(
        functools.partial(_mm_kernel, split=split),
        grid=(n // tn,),
        in_specs=[pl.BlockSpec((m, k), lambda j: (0, 0)),
                  pl.BlockSpec((k, tn), lambda j: (0, j))],
        out_specs=pl.BlockSpec((m, tn), lambda j: (0, j)),
        out_shape=jax.ShapeDtypeStruct((m, n), F32),
        compiler_params=_cparams("parallel"),
        name=name,
    )(x, w)


def _norm_mod(x, g, shift, scale):
    ms = jnp.mean(x * x, axis=-1, keepdims=True)
    h = x * lax.rsqrt(ms + NORM_EPS) * g
    return h * (1.0 + scale) + shift


def _nmm_kernel(x_ref, sh_ref, sc_ref, g_ref, w_ref, o_ref):
    h = _norm_mod(x_ref[0], g_ref[...], sh_ref[0], sc_ref[0])
    o_ref[0] = jnp.dot(h.astype(BF16), w_ref[...], preferred_element_type=F32)


def norm_mod_matmul(xs, shift, scale, g, w, *, tm=512, name="norm_mod_matmul"):
    G, L, D = xs.shape
    N = w.shape[1]
    return pl.pallas_call(
        _nmm_kernel,
        grid=(G, L // tm),
        in_specs=[pl.BlockSpec((1, tm, D), lambda g_, i: (g_, i, 0)),
                  pl.BlockSpec((1, 1, D), lambda g_, i: (g_, 0, 0)),
                  pl.BlockSpec((1, 1, D), lambda g_, i: (g_, 0, 0)),
                  pl.BlockSpec((1, D), lambda g_, i: (0, 0)),
                  pl.BlockSpec((D, N), lambda g_, i: (0, 0))],
        out_specs=pl.BlockSpec((1, tm, N), lambda g_, i: (g_, i, 0)),
        out_shape=jax.ShapeDtypeStruct((G, L, N), F32),
        compiler_params=_cparams("parallel", "parallel"),
        name=name,
    )(xs, shift, scale, g, w)


def _halo_specs(width, n_tiles):
    per = SEQ_TILE // HALO
    last = n_tiles * per - 1
    return [pl.BlockSpec((1, SEQ_TILE, width), lambda g_, i: (g_, i, 0)),
            pl.BlockSpec((1, HALO, width), lambda g_, i: (g_, jnp.maximum(i * per - 1, 0), 0)),
            pl.BlockSpec((1, HALO, width), lambda g_, i: (g_, jnp.minimum((i + 1) * per, last), 0))]


def _neighbours(u, prev_ref, next_ref):
    g_ = pl.program_id(0)
    i = pl.program_id(1)
    n_tiles = pl.num_programs(1)
    has_prev = jnp.logical_and(g_ > 0, i > 0)
    has_next = jnp.logical_and(g_ > 0, i < n_tiles - 1)
    row = lax.broadcasted_iota(jnp.int32, u.shape, 0)
    halo_p = jnp.where(has_prev, prev_ref[0, HALO - 1:HALO, :], 0.0)
    halo_n = jnp.where(has_next, next_ref[0, 0:1, :], 0.0)
    prev = jnp.where(row == 0, halo_p, pltpu.roll(u, 1, 0))
    nxt = jnp.where(row == SEQ_TILE - 1, halo_n, pltpu.roll(u, SEQ_TILE - 1, 0))
    return prev, nxt


def _rope(z, cos, sin_signed):
    lane = lax.broadcasted_iota(jnp.int32, z.shape, 1)
    partner = jnp.where(lane % 32 < 16, pltpu.roll(z, WIDTH - 16, 1), pltpu.roll(z, 16, 1))
    return z * cos + partner * sin_signed


def _rw_feat_kernel(u_ref, up_ref, un_ref, cos_ref, sin_ref, mu_ref, w0_ref, w2_ref, a0_ref, a2_ref,
                    g2_ref, kk_ref, ka_ref, ones_ref,
                    r_o, v_o, kkn_o, g_o, ld0_o, ld1_o, k0_o, k1_o, b0_o, b1_o):
    u = u_ref[0]
    prev, nxt = _neighbours(u, up_ref, un_ref)
    u = u + mu_ref[0:1, :] * (prev - u) + mu_ref[1:2, :] * (nxt - u)
    r = u[:, 0:WIDTH]
    k = u[:, WIDTH:2 * WIDTH]
    v = u[:, 2 * WIDTH:3 * WIDTH]
    o = 3 * WIDTH
    lw = u[:, o:o + 2 * RW_LORA]
    la = u[:, o + 2 * RW_LORA:o + 4 * RW_LORA]
    lg = u[:, o + 4 * RW_LORA:o + 4 * RW_LORA + RW_GATE_LORA]
    is_lat = pl.program_id(0) > 0
    cos = jnp.where(is_lat, cos_ref[...], 1.0)
    sin = jnp.where(is_lat, sin_ref[...], 0.0)
    r = _rope(r, cos, sin)
    k = _rope(k, cos, sin)
    g = _dot(_sigmoid(lg), g2_ref[...])
    kk = k * kk_ref[...]
    nrm = jnp.sqrt(_dot_split(kk * kk, ones_ref[...]))
    kk = kk / jnp.maximum(nrm, 1e-12)
    r_o[0] = r
    v_o[0] = v
    kkn_o[0] = kk
    g_o[0] = g
    tanh_lw = jnp.tanh(lw)
    for d, (ld_o, k_o, b_o) in enumerate(((ld0_o, k0_o, b0_o), (ld1_o, k1_o, b1_o))):
        zw = w0_ref[d:d + 1, :] + _dot(tanh_lw[:, d * RW_LORA:(d + 1) * RW_LORA], w2_ref[d])
        softplus = jnp.maximum(-zw, 0.0) + jnp.log(1.0 + jnp.exp(-jnp.abs(zw)))
        ld_o[0] = -jnp.exp(-softplus - 0.5)
        iclr = _sigmoid(a0_ref[d:d + 1, :] + _dot(la[:, d * RW_LORA:(d + 1) * RW_LORA], a2_ref[d]))
        k_o[0] = k * (1.0 + (iclr - 1.0) * ka_ref[...])
        b_o[0] = kk * iclr


def rwkv_features(u_rw, cos, sin, mu, w0, w2, a0, a2, g2, k_k, k_a, ones):
    G, L, W_IN = u_rw.shape
    n_tiles = L // SEQ_TILE
    full = lambda a: pl.BlockSpec(a.shape, lambda g_, i: (0,) * a.ndim)
    tab = pl.BlockSpec((SEQ_TILE, WIDTH), lambda g_, i: (i, 0))
    params = (mu, w0, w2, a0, a2, g2, k_k, k_a, ones)
    out = jax.ShapeDtypeStruct((G, L, WIDTH), F32)
    return pl.pallas_call(
        _rw_feat_kernel,
        grid=(G, n_tiles),
        in_specs=_halo_specs(W_IN, n_tiles) + [tab, tab] + [full(p) for p in params],
        out_specs=[pl.BlockSpec((1, SEQ_TILE, WIDTH), lambda g_, i: (g_, i, 0))] * 10,
        out_shape=[out] * 10,
        compiler_params=_cparams("parallel", "parallel"),
        name="rwkv_features",
    )(u_rw, u_rw, u_rw, cos, sin, *params)


def _rw_chunk_operands(r_ref, ld_ref, k_ref, v_ref, kk_ref, b_ref, incl, reverse):
    C = CHUNK
    ld = ld_ref[0]
    ld_hi = ld.astype(BF16)
    ld_lo = (ld - ld_hi.astype(F32)).astype(BF16)
    tri = jnp.where(incl, 1.0, 0.0).astype(BF16)
    cum = (jnp.dot(tri, ld_hi, preferred_element_type=F32)
           + jnp.dot(tri, ld_lo, preferred_element_type=F32))
    tot = cum[0:1, :] if reverse else cum[C - 1:C, :]
    e_neg = jnp.exp(-cum)
    e_rem = jnp.exp(tot - cum)
    b = b_ref[0]
    k = k_ref[0]
    return dict(
        a_t=(-kk_ref[0] * jnp.exp(cum - ld)).astype(BF16),
        r_t=(r_ref[0] * jnp.exp(cum)).astype(BF16),
        b_t=(b * e_neg).astype(BF16),
        k_t=(k * e_neg).astype(BF16),
        b_p=(b * e_rem).astype(BF16),
        k_p=(k * e_rem).astype(BF16),
        p_c=jnp.exp(tot),
        v=v_ref[0].astype(BF16))


def _rw_scan_kernel(*refs):
    in_refs, y_refs, s_ref = refs[:12], refs[12:14], refs[14]
    C = CHUNK

    @pl.when(pl.program_id(1) == 0)
    def _():
        s_ref[...] = jnp.zeros_like(s_ref)

    row = lax.broadcasted_iota(jnp.int32, (C, C), 0)
    col = lax.broadcasted_iota(jnp.int32, (C, C), 1)
    eye = row == col
    chains = []
    for d in range(2):
        incl, strict = (row <= col, row < col) if d == 1 else (row >= col, row > col)
        ops = _rw_chunk_operands(*in_refs[6 * d:6 * d + 6], incl, d == 1)
        for h in range(HEADS):
            sl = slice(h * HEAD_DIM, (h + 1) * HEAD_DIM)
            ch = {name: val[:, sl] for name, val in ops.items()}
            ch.update(d=d, h=h, sl=sl, incl=incl, strict=strict)
            chains.append(ch)

    each = lambda fn: [fn(ch) for ch in chains]
    both = lambda fn, xs: [fn(ch, x) for ch, x in zip(chains, xs)]
    ar = each(lambda ch: jnp.concatenate([ch["a_t"], ch["r_t"]], axis=0))
    g_b = both(lambda ch, x: _dot_nt(x, ch["b_t"]), ar)
    g_k = both(lambda ch, x: _dot_nt(x, ch["k_t"]), ar)
    a_ab = both(lambda ch, g: jnp.where(ch["strict"], g[:C], 0.0), g_b)
    a_rb = both(lambda ch, g: jnp.where(ch["incl"], g[C:], 0.0), g_b)
    a_k = both(lambda ch, g: jnp.concatenate([jnp.where(ch["strict"], g[:C], 0.0),
                                              jnp.where(ch["incl"], g[C:], 0.0)], axis=0), g_k)
    av = both(lambda ch, a: _dot(a, ch["v"]), a_k)
    t_inv = [jnp.where(eye, 1.0, a) for a in a_ab]
    pw = a_ab
    for _ in range(int(math.log2(C)) - 1):
        pw = [_dot(p, p) for p in pw]
        t_inv = [t + _dot(t, p) for t, p in zip(t_inv, pw)]
    w1 = both(lambda ch, t: _dot(t, ch["a_t"]), t_inv)
    w2 = [_dot(t, x[:C]) for t, x in zip(t_inv, av)]
    q_p = [ch["r_t"].astype(F32) + _dot(a, w) for ch, a, w in zip(chains, a_rb, w1)]
    y_p = [x[C:] + _dot(a, w) for x, a, w in zip(av, a_rb, w2)]
    m_mat = both(lambda ch, w: jnp.where(eye, ch["p_c"], 0.0) + _dot_tn(w, ch["b_p"]), w1)
    n_mat = both(lambda ch, w: _dot_tn(w, ch["b_p"]) + _dot_tn(ch["v"], ch["k_p"]), w2)
    s0 = each(lambda ch: s_ref[ch["d"], ch["h"]])
    y = [_dot_nt(q, s) + yp for q, s, yp in zip(q_p, s0, y_p)]
    s1 = [_dot(s, m) + n for s, m, n in zip(s0, m_mat, n_mat)]
    for ch, y_c, s_c in zip(chains, y, s1):
        y_refs[ch["d"]][0, :, ch["sl"]] = y_c
        s_ref[ch["d"], ch["h"]] = s_c


def rwkv_scan(r, v, kk, ld0, k0, b0, ld1, k1, b1, *, batch):
    G, L, _ = r.shape
    lc = L // batch
    nc_ctx = lc // CHUNK
    nc_lat = L // CHUNK
    n_steps = nc_ctx + nc_lat

    def idx(reverse, bi, s):
        in_ctx = s < nc_ctx
        c_ctx = nc_ctx - 1 - s if reverse else s
        c_lat = n_steps - 1 - s if reverse else s - nc_ctx
        return (jnp.where(in_ctx, 0, bi + 1), jnp.where(in_ctx, bi * nc_ctx + c_ctx, c_lat), 0)

    fwd = pl.BlockSpec((1, CHUNK, WIDTH), functools.partial(idx, False))
    rev = pl.BlockSpec((1, CHUNK, WIDTH), functools.partial(idx, True))
    out = jax.ShapeDtypeStruct((G, L, WIDTH), F32)
    return pl.pallas_call(
        _rw_scan_kernel,
        grid=(batch, n_steps),
        in_specs=[fwd] * 6 + [rev] * 6,
        out_specs=[fwd, rev],
        out_shape=[out, out],
        scratch_shapes=[pltpu.VMEM((2, HEADS, HEAD_DIM, HEAD_DIM), F32)],
        compiler_params=_cparams("parallel", "arbitrary"),
        name="rwkv_scan",
    )(r, ld0, k0, v, kk, b0, r, ld1, k1, v, kk, b1)


def _rw_readout_kernel(y0_ref, y1_ref, r_ref, k0_ref, k1_ref, v_ref, g_ref, rk_ref, gg_ref, gb_ref,
                       ones_ref, o_ref):
    y = y0_ref[0] + y1_ref[0]
    ones = ones_ref[...]
    inv_n = 1.0 / HEAD_DIM
    mu = _dot_split(y, ones) * inv_n
    yc = y - mu
    var = _dot_split(yc * yc, ones) * inv_n
    yn = yc * lax.rsqrt(var + RW_GN_EPS) * gg_ref[...] + gb_ref[...]
    bonus = _dot_split(r_ref[0] * (k0_ref[0] + k1_ref[0]) * rk_ref[...], ones) * v_ref[0]
    o_ref[0] = (yn + bonus) * g_ref[0]


def rwkv_readout(y0, y1, r, k0, k1, v, g, r_k, gn_g, gn_b, ones, *, tm=512):
    G, L, _ = r.shape
    spec = pl.BlockSpec((1, tm, WIDTH), lambda g_, i: (g_, i, 0))
    full = lambda a: pl.BlockSpec(a.shape, lambda g_, i: (0,) * a.ndim)
    params = (r_k, gn_g, gn_b, ones)
    return pl.pallas_call(
        _rw_readout_kernel,
        grid=(G, L // tm),
        in_specs=[spec] * 7 + [full(p) for p in params],
        out_specs=spec,
        out_shape=jax.ShapeDtypeStruct((G, L, WIDTH), F32),
        compiler_params=_cparams("parallel", "parallel"),
        name="rwkv_readout",
    )(y0, y1, r, k0, k1, v, g, *params)


def _hy_short_kernel(u_ref, up_ref, un_ref, w_ref, b_ref, o_ref):
    u = u_ref[0]
    prev, nxt = _neighbours(u, up_ref, un_ref)
    o_ref[0] = b_ref[...] + w_ref[0:1, :] * prev + w_ref[1:2, :] * u + w_ref[2:3, :] * nxt


def hyena_short_conv(u_hy, w, b):
    G, L, Wd = u_hy.shape
    n_tiles = L // SEQ_TILE
    full = lambda a: pl.BlockSpec(a.shape, lambda g_, i: (0,) * a.ndim)
    return pl.pallas_call(
        _hy_short_kernel,
        grid=(G, n_tiles),
        in_specs=_halo_specs(Wd, n_tiles) + [full(w), full(b)],
        out_specs=pl.BlockSpec((1, SEQ_TILE, Wd), lambda g_, i: (g_, i, 0)),
        out_shape=jax.ShapeDtypeStruct((G, L, Wd), F32),
        compiler_params=_cparams("parallel", "parallel"),
        name="hyena_short_conv",
    )(u_hy, u_hy, u_hy, w, b)


@functools.lru_cache(maxsize=None)
def _dft_tables(n):
    nn = 2 * n
    nf = n + 256
    kf = np.arange(nf, dtype=np.int64)
    t = np.arange(n, dtype=np.int64)
    ang = 2.0 * np.pi * ((kf[:, None] * t[None, :]) % nn).astype(np.float64) / nn
    valid = (kf <= n)[:, None]
    cosm = np.where(valid, np.cos(ang), 0.0)
    sinm = np.where(valid, np.sin(ang), 0.0)
    ck = np.where((kf == 0) | (kf == n), 1.0, 2.0)[:, None] / nn
    fwd = np.stack([cosm, -sinm])
    inv = np.stack([(ck * cosm).T, (-ck * sinm).T])
    return fwd, inv, nf


def _hy_conv_kernel(z_ref, x_ref, fwd_ref, inv_ref, hre_ref, him_ref, skip_ref, o_ref, acc_ref):
    f = pl.program_id(1)

    @pl.when(f == 0)
    def _():
        acc_ref[...] = jnp.zeros_like(acc_ref)

    z = z_ref[0]
    zb = z.astype(BF16)
    zre = jnp.dot(fwd_ref[0], zb, preferred_element_type=F32)
    zim = jnp.dot(fwd_ref[1], zb, preferred_element_type=F32)
    hre = hre_ref[0]
    him = him_ref[0]
    yre = zre * hre - zim * him
    yim = zre * him + zim * hre
    acc_ref[...] += (jnp.dot(inv_ref[0], yre.astype(BF16), preferred_element_type=F32)
                     + jnp.dot(inv_ref[1], yim.astype(BF16), preferred_element_type=F32))

    @pl.when(f == pl.num_programs(1) - 1)
    def _():
        o_ref[0] = x_ref[0] * (acc_ref[...] + skip_ref[0] * z)


def hyena_long_conv(z_arr, z_blk, x_arr, x_blk, hre, him, skip, order, fwd, inv, *, n_seq, n, tf):
    nf = fwd.shape[1]
    zo, zc = z_blk
    xo, xc = x_blk
    return pl.pallas_call(
        _hy_conv_kernel,
        grid=(n_seq, nf // tf),
        in_specs=[pl.BlockSpec((1, n, HY_WIDTH), lambda s, f: (s + zo, 0, zc)),
                  pl.BlockSpec((1, n, HY_WIDTH), lambda s, f: (s + xo, 0, xc)),
                  pl.BlockSpec((2, tf, n), lambda s, f: (0, f, 0)),
                  pl.BlockSpec((2, n, tf), lambda s, f: (0, 0, f)),
                  pl.BlockSpec((1, tf, HY_WIDTH), lambda s, f: (order, f, 0)),
                  pl.BlockSpec((1, tf, HY_WIDTH), lambda s, f: (order, f, 0)),
                  pl.BlockSpec((1, 1, HY_WIDTH), lambda s, f: (order, 0, 0))],
        out_specs=pl.BlockSpec((1, n, HY_WIDTH), lambda s, f: (s, 0, 0)),
        out_shape=jax.ShapeDtypeStruct((n_seq, n, HY_WIDTH), F32),
        scratch_shapes=[pltpu.VMEM((n, HY_WIDTH), F32)],
        compiler_params=_cparams("parallel", "arbitrary"),
        name=f"hyena_long_conv_n{n}_o{order}",
    )(z_arr, x_arr, fwd, inv, hre, him, skip)


def hyena_filter_spectrum(n, w1, b1, w2, b2, w3, freq, fwd):
    hp = lax.Precision.HIGHEST
    pos = jnp.arange(n, dtype=F32)
    t = pos / max(n - 1, 1)
    bands = (HY_EMB_DIM - 1) // 2
    fr = jnp.linspace(1e-4, bands - 1, bands, dtype=F32)
    ang = (2 * math.pi / n) * pos[:, None] * fr[None, :]
    z = jnp.concatenate([t[:, None], jnp.cos(ang), -jnp.sin(ang)], axis=-1)
    h = jnp.sin(freq * (jnp.dot(z, w1, precision=hp) + b1))
    h = jnp.sin(freq * (jnp.dot(h, w2, precision=hp) + b2))
    h = matmul(h, w3, tn=w3.shape[1], split=True, name="hyena_filter_out")
    h = h.reshape(n, HY_ORDER, 2, HY_WIDTH)
    max_decay = math.log(HY_DECAY_TARGET) / HY_FAST_DECAY
    min_decay = math.log(HY_DECAY_TARGET) / HY_SLOW_DECAY
    deltas = jnp.abs(jnp.linspace(min_decay, max_decay, HY_WIDTH, dtype=F32))
    h = h * jnp.exp(-t[:, None] * deltas)[:, None, None, :]
    h_fwd = h[:, :, 0]
    h_bwd = h[:, :, 1] * (pos > 0).astype(F32)[:, None, None]
    l1 = jnp.sum(jnp.abs(h_fwd), axis=0) + jnp.sum(jnp.abs(h_bwd), axis=0)
    even = ((h_fwd + h_bwd) / l1).reshape(n, HY_ORDER * HY_WIDTH)
    odd = ((h_fwd - h_bwd) / l1).reshape(n, HY_ORDER * HY_WIDTH)
    nf = fwd.shape[1]
    hre = matmul(fwd[0], even, tn=HY_ORDER * HY_WIDTH, name="hyena_filter_dft_re")
    him = matmul(fwd[1], odd, tn=HY_ORDER * HY_WIDTH, name="hyena_filter_dft_im")
    to_ofc = lambda a: jnp.moveaxis(a.reshape(nf, HY_ORDER, HY_WIDTH), 1, 0)
    return to_ofc(hre), to_ofc(him)


def _na_prep_kernel(u_ref, qg_ref, kg_ref, ones_ref, q_o, k_o, v_o):
    u = u_ref[0]
    ones = ones_ref[...]
    q = u[:, 0:WIDTH]
    k = u[:, WIDTH:2 * WIDTH]
    inv_n = 1.0 / HEAD_DIM
    qn = q * lax.rsqrt(_dot_split(q * q, ones) * inv_n + NORM_EPS) * qg_ref[...]
    kn = k * lax.rsqrt(_dot_split(k * k, ones) * inv_n + NORM_EPS) * kg_ref[...]
    q_o[0] = (qn * (HEAD_DIM ** -0.5)).astype(BF16)
    k_o[0] = kn.astype(BF16)
    v_o[0] = u[:, 2 * WIDTH:3 * WIDTH].astype(BF16)


def na_prepare(u_na, q_gain, k_gain, ones, *, tm=512):
    G, L, Wd = u_na.shape
    full = lambda a: pl.BlockSpec(a.shape, lambda g_, i: (0,) * a.ndim)
    spec = pl.BlockSpec((1, tm, WIDTH), lambda g_, i: (g_, i, 0))
    out = jax.ShapeDtypeStruct((G, L, WIDTH), BF16)
    return pl.pallas_call(
        _na_prep_kernel,
        grid=(G, L // tm),
        in_specs=[pl.BlockSpec((1, tm, Wd), lambda g_, i: (g_, i, 0)), full(q_gain), full(k_gain), full(ones)],
        out_specs=[spec] * 3,
        out_shape=[out] * 3,
        compiler_params=_cparams("parallel", "parallel"),
        name="na_prepare",
    )(u_na, q_gain, k_gain, ones)


def _na_lat_kernel(q_ref, k_ref, v_ref, kc_ref, vc_ref, bias_ref, o_ref, *, rows, wr):
    i = pl.program_id(1)
    r0 = jnp.clip(i - wr // 2, 0, rows - wr)
    off = r0 - i + (NA_WIN_ROWS - 1)
    start = pl.multiple_of(r0 * GRID_W, GRID_W)
    q = q_ref[0]
    kw = k_ref[0, pl.ds(start, wr * GRID_W), :]
    vw = v_ref[0, pl.ds(start, wr * GRID_W), :]
    kc = kc_ref[0]
    vc = vc_ref[0]
    sls = [slice(h * HEAD_DIM, (h + 1) * HEAD_DIM) for h in range(HEADS)]
    s_loc = [_dot_nt(q[:, sl], kw[:, sl]) + bias_ref[h, off] for h, sl in enumerate(sls)]
    s_ctx = [_dot_nt(q[:, sl], kc[:, sl]) for sl in sls]
    m = [jnp.maximum(jnp.max(a, axis=-1, keepdims=True), jnp.max(b, axis=-1, keepdims=True))
         for a, b in zip(s_loc, s_ctx)]
    p_loc = [jnp.exp(a - mm) for a, mm in zip(s_loc, m)]
    p_ctx = [jnp.exp(b - mm) for b, mm in zip(s_ctx, m)]
    den = [jnp.sum(a, axis=-1, keepdims=True) + jnp.sum(b, axis=-1, keepdims=True) for a, b in zip(p_loc, p_ctx)]
    o = [_dot(a, vw[:, sl]) + _dot(b, vc[:, sl]) for a, b, sl in zip(p_loc, p_ctx, sls)]
    for sl, oo, dd in zip(sls, o, den):
        o_ref[0, :, sl] = oo / dd


def _na_bias_table(rpb, rows, wr):
    hp = lax.Precision.HIGHEST
    offs = np.arange(NA_WIN_ROWS) - (NA_WIN_ROWS - 1)
    dr = offs[:, None] + np.arange(wr)[None, :] + NA_WIN_ROWS - 1
    c_ar = np.arange(GRID_W)
    dc = np.clip(c_ar[None, :] - c_ar[:, None] + NA_WIN_COLS - 1, 0, 2 * NA_WIN_COLS - 2)
    c0 = np.clip(c_ar - NA_WIN_COLS // 2, 0, GRID_W - NA_WIN_COLS)
    in_win = (c_ar[None, :] >= c0[:, None]) & (c_ar[None, :] < c0[:, None] + NA_WIN_COLS)
    oh_r = jnp.asarray(dr[..., None] == np.arange(2 * NA_WIN_ROWS - 1), dtype=F32)
    oh_c = jnp.asarray(dc[..., None] == np.arange(2 * NA_WIN_COLS - 1), dtype=F32)
    by_row = jnp.einsum("hab,ora->horb", rpb.astype(F32), oh_r, precision=hp)
    bias = jnp.einsum("horb,qkb->hoqrk", by_row, oh_c, precision=hp)
    bias = jnp.where(in_win[None, None, :, None, :], bias, NEG_BIG)
    return bias.reshape(HEADS, NA_WIN_ROWS, GRID_W, wr * GRID_W)


def na_latent(qn, kn, vb, bias, *, batch):
    G, L, _ = qn.shape
    lc = L // batch
    rows = L // GRID_W
    wr = min(NA_WIN_ROWS, rows)
    qspec = pl.BlockSpec((1, GRID_W, WIDTH), lambda b_, i: (b_ + 1, i, 0))
    seq = pl.BlockSpec((1, L, WIDTH), lambda b_, i: (b_ + 1, 0, 0))
    ctx = pl.BlockSpec((1, lc, WIDTH), lambda b_, i: (0, b_, 0))
    return pl.pallas_call(
        functools.partial(_na_lat_kernel, rows=rows, wr=wr),
        grid=(batch, rows),
        in_specs=[qspec, seq, seq, ctx, ctx, pl.BlockSpec(bias.shape, lambda b_, i: (0, 0, 0, 0))],
        out_specs=pl.BlockSpec((1, GRID_W, WIDTH), lambda b_, i: (b_, i, 0)),
        out_shape=jax.ShapeDtypeStruct((batch, L, WIDTH), F32),
        compiler_params=_cparams("parallel", "arbitrary"),
        name="na_latent",
    )(qn, kn, vb, kn, vb, bias)


def _na_ctx_kernel(q_ref, k_ref, v_ref, o_ref):
    q = q_ref[0]
    k = k_ref[0]
    v = v_ref[0]
    for h in range(HEADS):
        sl = slice(h * HEAD_DIM, (h + 1) * HEAD_DIM)
        s = _dot_nt(q[:, sl], k[:, sl])
        p = jnp.exp(s - jnp.max(s, axis=-1, keepdims=True))
        o_ref[0, :, sl] = _dot(p, v[:, sl]) / jnp.sum(p, axis=-1, keepdims=True)


def na_context(qn, kn, vb, *, batch):
    G, L, _ = qn.shape
    lc = L // batch
    spec = pl.BlockSpec((1, lc, WIDTH), lambda b_: (0, b_, 0))
    return pl.pallas_call(
        _na_ctx_kernel,
        grid=(batch,),
        in_specs=[spec] * 3,
        out_specs=spec,
        out_shape=jax.ShapeDtypeStruct((1, L, WIDTH), F32),
        compiler_params=_cparams("parallel"),
        name="na_context",
    )(qn, kn, vb)


def _merge_kernel(x_ref, gl_ref, rw_ref, hy_ref, na_ref, g1_ref, wr_ref, wh_ref, wn_ref, wo_ref, o_ref):
    D = x_ref.shape[-1]
    gl = gl_ref[0]
    m = (_sigmoid(gl[:, 0:D]) * _dot(rw_ref[0], wr_ref[...])
         + _sigmoid(gl[:, D:2 * D]) * _dot(hy_ref[0], wh_ref[...])
         + _sigmoid(gl[:, 2 * D:3 * D]) * _dot(na_ref[0], wn_ref[...]))
    o_ref[0] = x_ref[0] + g1_ref[0] * _dot(m, wo_ref[...])


def merge_branches(xs, u_gate, y_rw, y_hy, y_na, gate1, w_rw, w_hy, w_na, w_o, *, g_off, tm=512):
    G, L, D = xs.shape
    tok = lambda width: pl.BlockSpec((1, tm, width), lambda g_, i: (g_ + g_off, i, 0))
    full = lambda a: pl.BlockSpec(a.shape, lambda g_, i: (0,) * a.ndim)
    weights = (w_rw, w_hy, w_na, w_o)
    return pl.pallas_call(
        _merge_kernel,
        grid=(G - g_off, L // tm),
        in_specs=[tok(D), tok(3 * D), tok(WIDTH), tok(HY_WIDTH), tok(WIDTH),
                  pl.BlockSpec((1, 1, D), lambda g_, i: (g_ + g_off, 0, 0))] + [full(w) for w in weights],
        out_specs=pl.BlockSpec((1, tm, D), lambda g_, i: (g_, i, 0)),
        out_shape=jax.ShapeDtypeStruct((G - g_off, L, D), F32),
        compiler_params=_cparams("parallel", "parallel"),
        name="merge_branches",
    )(xs, u_gate, y_rw, y_hy, y_na, gate1, *weights)


def _ffn_kernel(x_ref, sh_ref, sc_ref, g_ref, g2_ref, w1_ref, w3_ref, w2_ref, o_ref, h_ref, acc_ref):
    f = pl.program_id(2)

    @pl.when(f == 0)
    def _():
        h_ref[...] = _norm_mod(x_ref[0], g_ref[...], sh_ref[0], sc_ref[0]).astype(BF16)
        acc_ref[...] = jnp.zeros_like(acc_ref)

    h = h_ref[...]
    a = jnp.dot(h, w1_ref[...], preferred_element_type=F32)
    b = jnp.dot(h, w3_ref[...], preferred_element_type=F32)
    acc_ref[...] += _dot(_silu(a) * b, w2_ref[...])

    @pl.when(f == pl.num_programs(2) - 1)
    def _():
        o_ref[0] = x_ref[0] + g2_ref[0] * acc_ref[...]


def ffn_dense(xs, shift, scale, g, gate2, w1, w3, w2, *, tm=512, tf):
    G, L, D = xs.shape
    FF = w1.shape[1]
    mod = pl.BlockSpec((1, 1, D), lambda g_, i, f: (g_, 0, 0))
    return pl.pallas_call(
        _ffn_kernel,
        grid=(G, L // tm, FF // tf),
        in_specs=[pl.BlockSpec((1, tm, D), lambda g_, i, f: (g_, i, 0)), mod, mod,
                  pl.BlockSpec((1, D), lambda g_, i, f: (0, 0)), mod,
                  pl.BlockSpec((D, tf), lambda g_, i, f: (0, f)),
                  pl.BlockSpec((D, tf), lambda g_, i, f: (0, f)),
                  pl.BlockSpec((tf, D), lambda g_, i, f: (f, 0))],
        out_specs=pl.BlockSpec((1, tm, D), lambda g_, i, f: (g_, i, 0)),
        out_shape=jax.ShapeDtypeStruct((G, L, D), F32),
        scratch_shapes=[pltpu.VMEM((tm, D), BF16), pltpu.VMEM((tm, D), F32)],
        compiler_params=_cparams("parallel", "parallel", "arbitrary"),
        name="ffn_dense",
    )(xs, shift, scale, g, gate2, w1, w3, w2)


def _moe_kernel(x_ref, sh_ref, sc_ref, g_ref, g2_ref, rt_ref, w1_ref, w3_ref, w2_ref, o_ref,
                h_ref, gates_ref, acc_ref):
    e = pl.program_id(2)
    f = pl.program_id(3)

    @pl.when(jnp.logical_and(e == 0, f == 0))
    def _():
        h = _norm_mod(x_ref[0], g_ref[...], sh_ref[0], sc_ref[0])
        h_ref[...] = h.astype(BF16)
        acc_ref[...] = jnp.zeros_like(acc_ref)
        rt = rt_ref[...]
        hh = h.astype(BF16)
        hl = (h - hh.astype(F32)).astype(BF16)
        rh = rt.astype(BF16)
        rl = (rt - rh.astype(F32)).astype(BF16)
        logits = (jnp.dot(hh, rh, preferred_element_type=F32) + jnp.dot(hl, rh, preferred_element_type=F32)
                  + jnp.dot(hh, rl, preferred_element_type=F32))
        lane = lax.broadcasted_iota(jnp.int32, logits.shape, 1)
        logits = jnp.where(lane < N_EXPERTS, logits, NEG_BIG)
        m1 = jnp.max(logits, axis=-1, keepdims=True)
        i1 = jnp.min(jnp.where(logits == m1, lane, LANES), axis=-1, keepdims=True)
        rest = jnp.where(lane == i1, NEG_BIG, logits)
        m2 = jnp.max(rest, axis=-1, keepdims=True)
        i2 = jnp.min(jnp.where(rest == m2, lane, LANES), axis=-1, keepdims=True)
        e2 = jnp.exp(m2 - m1)
        gates_ref[...] = (jnp.where(lane == i1, 1.0 / (1.0 + e2), 0.0)
                          + jnp.where(lane == i2, e2 / (1.0 + e2), 0.0))

    h = h_ref[...]
    a = jnp.dot(h, w1_ref[0], preferred_element_type=F32)
    b = jnp.dot(h, w3_ref[0], preferred_element_type=F32)
    gates = gates_ref[...]
    lane = lax.broadcasted_iota(jnp.int32, gates.shape, 1)
    ge = jnp.sum(jnp.where(lane == e, gates, 0.0), axis=-1, keepdims=True)
    acc_ref[...] += _dot(_silu(a) * b * ge, w2_ref[0])

    @pl.when(jnp.logical_and(e == pl.num_programs(2) - 1, f == pl.num_programs(3) - 1))
    def _():
        o_ref[0] = x_ref[0] + g2_ref[0] * acc_ref[...]


def moe_ffn(xs, shift, scale, g, gate2, router, w1, w3, w2, *, g_off, tm=512, tf):
    G, L, D = xs.shape
    E, _, FF = w1.shape
    mod = pl.BlockSpec((1, 1, D), lambda g_, i, e, f: (g_ + g_off, 0, 0))
    return pl.pallas_call(
        _moe_kernel,
        grid=(G - g_off, L // tm, E, FF // tf),
        in_specs=[pl.BlockSpec((1, tm, D), lambda g_, i, e, f: (g_ + g_off, i, 0)), mod, mod,
                  pl.BlockSpec((1, D), lambda g_, i, e, f: (0, 0)), mod,
                  pl.BlockSpec((D, LANES), lambda g_, i, e, f: (0, 0)),
                  pl.BlockSpec((1, D, tf), lambda g_, i, e, f: (e, 0, f)),
                  pl.BlockSpec((1, D, tf), lambda g_, i, e, f: (e, 0, f)),
                  pl.BlockSpec((1, tf, D), lambda g_, i, e, f: (e, f, 0))],
        out_specs=pl.BlockSpec((1, tm, D), lambda g_, i, e, f: (g_, i, 0)),
        out_shape=jax.ShapeDtypeStruct((G - g_off, L, D), F32),
        scratch_shapes=[pltpu.VMEM((tm, D), BF16), pltpu.VMEM((tm, LANES), F32), pltpu.VMEM((tm, D), F32)],
        compiler_params=_cparams("parallel", "parallel", "arbitrary", "arbitrary"),
        name="moe_ffn",
    )(xs, shift, scale, g, gate2, router, w1, w3, w2)


def _rope_tables(n):
    half = HEAD_DIM // 2
    nf = half // 2
    t = np.arange(n)
    inv = ROPE_BASE ** (-np.arange(nf, dtype=np.float64) / nf)
    d = np.arange(HEAD_DIM)
    pos = np.where(d[None, :] < half, (t // GRID_W)[:, None], (t % GRID_W)[:, None]).astype(np.float64)
    ang = pos * inv[d % nf][None, :]
    sign = np.where(d % half < nf, -1.0, 1.0)
    cos = np.tile(np.cos(ang), (1, HEADS))
    sin = np.tile(np.sin(ang) * sign[None, :], (1, HEADS))
    return jnp.asarray(cos, dtype=F32), jnp.asarray(sin, dtype=F32)


def kernel(x, c, ctx, c_ctx, ada_w, ada_b, norm1_g, norm2_g, w_in, rw_shift, rw_w0, rw_w2, rw_a0, rw_a2, rw_g2, rw_kk, rw_ka, rw_rk, rw_gn_g, rw_gn_b, hy_conv_w, hy_conv_b, hy_f_w1, hy_f_b1, hy_f_w2, hy_f_b2, hy_f_w3, hy_f_freq, hy_skip, na_q_gain, na_k_gain, na_rpb, w_br_rw, w_br_hy, w_br_na, w_out, ff_w1, ff_w3, ff_w2, moe_router, moe_w1, moe_w3, moe_w2):
    B, L, D = x.shape
    lc = ctx.shape[1]
    depth = ada_w.shape[0]
    assert B * lc == L and lc == SEQ_TILE and L % GRID_W == 0
    G = B + 1
    rows = L // GRID_W
    wr = min(NA_WIN_ROWS, rows)
    rw_in = 3 * WIDTH + 4 * RW_LORA + RW_GATE_LORA
    hy_in = (HY_ORDER + 1) * HY_WIDTH
    na_in = 3 * WIDTH
    splits = (rw_in, rw_in + hy_in, rw_in + hy_in + na_in)

    xs = jnp.concatenate([ctx.reshape(1, L, D), x], axis=0)
    act = jnp.concatenate([c_ctx[None, :], c], axis=0)
    act = act * _sigmoid(act)
    act = jnp.pad(act, ((0, (-G) % 16), (0, 0)))
    ones = _head_ones()
    cos, sin = _rope_tables(L)
    fwd_lat, inv_lat, _ = _dft_tables(L)
    fwd_ctx, inv_ctx, _ = _dft_tables(lc)
    fwd_lat, inv_lat, fwd_ctx, inv_ctx = (jnp.asarray(a, dtype=F32).astype(BF16)
                                          for a in (fwd_lat, inv_lat, fwd_ctx, inv_ctx))
    row2 = lambda a: a.reshape(1, -1)

    for li in range(depth):
        last = li == depth - 1
        mods = matmul(act, ada_w[li], tn=1024, split=True, name="adaln")[:G] + ada_b[li]
        sh1, sc1, g1, sh2, sc2, g2 = (m.reshape(G, 1, D) for m in jnp.split(mods, 6, axis=-1))

        w_in_b = w_in[li].astype(BF16)
        n1 = row2(norm1_g[li])
        proj = lambda lo, hi, nm: norm_mod_matmul(xs, sh1, sc1, n1, w_in_b[:, lo:hi], name=nm)
        u_rw = proj(0, splits[0], "proj_rw")
        u_hy = proj(splits[0], splits[1], "proj_hy")
        u_na = proj(splits[1], splits[2], "proj_na")
        u_gate = proj(splits[2], w_in_b.shape[1], "proj_gate")

        r, v, kk, g, ld0, ld1, k0, k1, b0, b1 = rwkv_features(
            u_rw, cos, sin, rw_shift[li], rw_w0[li], rw_w2[li].astype(BF16), rw_a0[li], rw_a2[li].astype(BF16),
            rw_g2[li].astype(BF16), row2(rw_kk[li]), row2(rw_ka[li]), ones)
        y0, y1 = rwkv_scan(r, v, kk, ld0, k0, b0, ld1, k1, b1, batch=B)
        y_rw = rwkv_readout(y0, y1, r, k0, k1, v, g, row2(rw_rk[li]), row2(rw_gn_g[li]), row2(rw_gn_b[li]), ones)

        hy = hyena_short_conv(u_hy, hy_conv_w[li], row2(hy_conv_b[li]))
        f_args = (hy_f_w1[li], hy_f_b1[li], hy_f_w2[li], hy_f_b2[li], hy_f_w3[li], hy_f_freq[li])
        skip = hy_skip[li].reshape(HY_ORDER, 1, HY_WIDTH)
        hre, him = hyena_filter_spectrum(L, *f_args, fwd_lat)
        conv = functools.partial(hyena_long_conv, skip=skip, fwd=fwd_lat, inv=inv_lat, n_seq=B, n=L, tf=384)
        z1 = conv(hy, (1, 0), hy, (1, 1), hre, him, order=0)
        y_hy = conv(z1, (0, 0), hy, (1, 2), hre, him, order=1)
        if last:
            y_hy = jnp.concatenate([jnp.zeros((1, L, HY_WIDTH), F32), y_hy], axis=0)
        else:
            hy_c = hy[0].reshape(B, lc, hy_in)
            hre_c, him_c = hyena_filter_spectrum(lc, *f_args, fwd_ctx)
            conv_c = functools.partial(hyena_long_conv, skip=skip, fwd=fwd_ctx, inv=inv_ctx, n_seq=B, n=lc,
                                       tf=fwd_ctx.shape[1])
            z1c = conv_c(hy_c, (0, 0), hy_c, (0, 1), hre_c, him_c, order=0)
            y_hy_c = conv_c(z1c, (0, 0), hy_c, (0, 2), hre_c, him_c, order=1)
            y_hy = jnp.concatenate([y_hy_c.reshape(1, L, HY_WIDTH), y_hy], axis=0)

        qn, kn, vb = na_prepare(u_na, row2(jnp.tile(na_q_gain[li], HEADS)), row2(jnp.tile(na_k_gain[li], HEADS)), ones)
        bias = _na_bias_table(na_rpb[li], rows, wr)
        y_na = na_latent(qn, kn, vb, bias, batch=B)
        y_na_c = jnp.zeros((1, L, WIDTH), F32) if last else na_context(qn, kn, vb, batch=B)
        y_na = jnp.concatenate([y_na_c, y_na], axis=0)

        g_off = 1 if last else 0
        xs = merge_branches(xs, u_gate, y_rw, y_hy, y_na, g1, w_br_rw[li].astype(BF16), w_br_hy[li].astype(BF16),
                            w_br_na[li].astype(BF16), w_out[li].astype(BF16), g_off=g_off)
        if last:
            sh2, sc2, g2 = sh2[1:], sc2[1:], g2[1:]

        n2 = row2(norm2_g[li])
        if li % 2 == 0:
            j = li // 2
            xs = ffn_dense(xs, sh2, sc2, n2, g2, ff_w1[j].astype(BF16), ff_w3[j].astype(BF16), ff_w2[j].astype(BF16),
                           tf=ff_w1.shape[2] // 2)
        else:
            j = li // 2
            router = jnp.pad(moe_router[j], ((0, 0), (0, LANES - N_EXPERTS)))
            xs = moe_ffn(xs, sh2, sc2, n2, g2, router, moe_w1[j].astype(BF16), moe_w3[j].astype(BF16),
                         moe_w2[j].astype(BF16), g_off=0, tf=moe_w1.shape[3] // 2)
    return xs if depth == 0 else xs[-B:]
```

```python
import functools
import math

import numpy as np
import jax
import jax.numpy as jnp
from jax import lax
from jax.experimental import pallas as pl
from jax.experimental.pallas import tpu as pltpu

F32 = jnp.float32
BF16 = jnp.bfloat16

NORM_EPS = 1e-6
ROPE_BASE = 10000.0
GRID_W = 64
HEADS = 6
HEAD_DIM = 64
WIDTH = HEADS * HEAD_DIM
RW_LORA = 64
RW_GATE_LORA = 128
RW_GN_EPS = 64e-5
HY_WIDTH = 256
HY_ORDER = 2
HY_EMB_DIM = 33
HY_DECAY_TARGET = 1e-2
HY_FAST_DECAY = 0.3
HY_SLOW_DECAY = 1.5
NA_WIN_ROWS = 8
NA_WIN_COLS = 16
N_EXPERTS = 8
LANES = 128
SEQ_TILE = 256
HALO = 8
CHUNK = 64
VMEM_LIMIT = 56 * 1024 * 1024
NEG_BIG = -1e30


def _cparams(*sem):
    return pltpu.CompilerParams(dimension_semantics=sem, vmem_limit_bytes=VMEM_LIMIT)


def _dot(a, b):
    return jnp.dot(a.astype(BF16), b.astype(BF16), preferred_element_type=F32)


def _dot_nt(a, b):
    return lax.dot_general(a.astype(BF16), b.astype(BF16), (((1,), (1,)), ((), ())),
                           preferred_element_type=F32)


def _dot_tn(a, b):
    return lax.dot_general(a.astype(BF16), b.astype(BF16), (((0,), (0,)), ((), ())),
                           preferred_element_type=F32)


def _dot_split(a, b_exact):
    hi = a.astype(BF16)
    lo = (a - hi.astype(F32)).astype(BF16)
    return (jnp.dot(hi, b_exact, preferred_element_type=F32)
            + jnp.dot(lo, b_exact, preferred_element_type=F32))


def _sigmoid(x):
    return 1.0 / (1.0 + jnp.exp(-x))


def _silu(x):
    return x * _sigmoid(x)


def _head_ones():
    h = np.arange(WIDTH) // HEAD_DIM
    return jnp.asarray((h[:, None] == h[None, :]).astype(np.float32), dtype=BF16)


def _mm_kernel(x_ref, w_ref, o_ref, *, split):
    x = x_ref[...]
    w = w_ref[...]
    if split:
        xh = x.astype(BF16)
        xl = (x - xh.astype(F32)).astype(BF16)
        wh = w.astype(BF16)
        wl = (w - wh.astype(F32)).astype(BF16)
        o_ref[...] = (jnp.dot(xh, wh, preferred_element_type=F32)
                      + jnp.dot(xl, wh, preferred_element_type=F32)
                      + jnp.dot(xh, wl, preferred_element_type=F32))
    else:
        o_ref[...] = jnp.dot(x.astype(BF16), w.astype(BF16), preferred_element_type=F32)


def matmul(x, w, *, tn, split=False, name="matmul"):
    m, k = x.shape
    n = w.shape[1]
    assert n % tn == 0
    return pl.pallas_call(
        functools.partial(_mm_kernel, split=split),
        grid=(n // tn,),
        in_specs=[pl.BlockSpec((m, k), lambda j: (0, 0)),
                  pl.BlockSpec((k, tn), lambda j: (0, j))],
        out_specs=pl.BlockSpec((m, tn), lambda j: (0, j)),
        out_shape=jax.ShapeDtypeStruct((m, n), F32),
        compiler_params=_cparams("parallel"),
        name=name,
    )(x, w)


def _norm_mod(x, g, shift, scale):
    ms = jnp.mean(x * x, axis=-1, keepdims=True)
    h = x * lax.rsqrt(ms + NORM_EPS) * g
    return h * (1.0 + scale) + shift


def _nmm_kernel(x_ref, sh_ref, sc_ref, g_ref, w_ref, o_ref):
    h = _norm_mod(x_ref[0], g_ref[...], sh_ref[0], sc_ref[0])
    o_ref[0] = jnp.dot(h.astype(BF16), w_ref[...], preferred_element_type=F32)


def norm_mod_matmul(xs, shift, scale, g, w, *, tm=512, name="norm_mod_matmul"):
    G, L, D = xs.shape
    N = w.shape[1]
    return pl.pallas_call(
        _nmm_kernel,
        grid=(G, L // tm),
        in_specs=[pl.BlockSpec((1, tm, D), lambda g_, i: (g_, i, 0)),
                  pl.BlockSpec((1, 1, D), lambda g_, i: (g_, 0, 0)),
                  pl.BlockSpec((1, 1, D), lambda g_, i: (g_, 0, 0)),
                  pl.BlockSpec((1, D), lambda g_, i: (0, 0)),
                  pl.BlockSpec((D, N), lambda g_, i: (0, 0))],
        out_specs=pl.BlockSpec((1, tm, N), lambda g_, i: (g_, i, 0)),
        out_shape=jax.ShapeDtypeStruct((G, L, N), F32),
        compiler_params=_cparams("parallel", "parallel"),
        name=name,
    )(xs, shift, scale, g, w)


def _halo_specs(width, n_tiles):
    per = SEQ_TILE // HALO
    last = n_tiles * per - 1
    return [pl.BlockSpec((1, SEQ_TILE, width), lambda g_, i: (g_, i, 0)),
            pl.BlockSpec((1, HALO, width), lambda g_, i: (g_, jnp.maximum(i * per - 1, 0), 0)),
            pl.BlockSpec((1, HALO, width), lambda g_, i: (g_, jnp.minimum((i + 1) * per, last), 0))]


def _neighbours(u, prev_ref, next_ref):
    g_ = pl.program_id(0)
    i = pl.program_id(1)
    n_tiles = pl.num_programs(1)
    has_prev = jnp.logical_and(g_ > 0, i > 0)
    has_next = jnp.logical_and(g_ > 0, i < n_tiles - 1)
    row = lax.broadcasted_iota(jnp.int32, u.shape, 0)
    halo_p = jnp.where(has_prev, prev_ref[0, HALO - 1:HALO, :], 0.0)
    halo_n = jnp.where(has_next, next_ref[0, 0:1, :], 0.0)
    prev = jnp.where(row == 0, halo_p, pltpu.roll(u, 1, 0))
    nxt = jnp.where(row == SEQ_TILE - 1, halo_n, pltpu.roll(u, SEQ_TILE - 1, 0))
    return prev, nxt


def _rope(z, cos, sin_signed):
    lane = lax.broadcasted_iota(jnp.int32, z.shape, 1)
    partner = jnp.where(lane % 32 < 16, pltpu.roll(z, WIDTH - 16, 1), pltpu.roll(z, 16, 1))
    return z * cos + partner * sin_signed


def _rw_feat_kernel(u_ref, up_ref, un_ref, cos_ref, sin_ref, mu_ref, w0_ref, w2_ref, a0_ref, a2_ref,
                    g2_ref, kk_ref, ka_ref, ones_ref,
                    r_o, v_o, kkn_o, g_o, ld0_o, ld1_o, k0_o, k1_o, b0_o, b1_o):
    u = u_ref[0]
    prev, nxt = _neighbours(u, up_ref, un_ref)
    u = u + mu_ref[0:1, :] * (prev - u) + mu_ref[1:2, :] * (nxt - u)
    r = u[:, 0:WIDTH]
    k = u[:, WIDTH:2 * WIDTH]
    v = u[:, 2 * WIDTH:3 * WIDTH]
    o = 3 * WIDTH
    lw = u[:, o:o + 2 * RW_LORA]
    la = u[:, o + 2 * RW_LORA:o + 4 * RW_LORA]
    lg = u[:, o + 4 * RW_LORA:o + 4 * RW_LORA + RW_GATE_LORA]
    is_lat = pl.program_id(0) > 0
    cos = jnp.where(is_lat, cos_ref[...], 1.0)
    sin = jnp.where(is_lat, sin_ref[...], 0.0)
    r = _rope(r, cos, sin)
    k = _rope(k, cos, sin)
    g = _dot(_sigmoid(lg), g2_ref[...])
    kk = k * kk_ref[...]
    nrm = jnp.sqrt(_dot_split(kk * kk, ones_ref[...]))
    kk = kk / jnp.maximum(nrm, 1e-12)
    r_o[0] = r
    v_o[0] = v
    kkn_o[0] = kk
    g_o[0] = g
    tanh_lw = jnp.tanh(lw)
    for d, (ld_o, k_o, b_o) in enumerate(((ld0_o, k0_o, b0_o), (ld1_o, k1_o, b1_o))):
        zw = w0_ref[d:d + 1, :] + _dot(tanh_lw[:, d * RW_LORA:(d + 1) * RW_LORA], w2_ref[d])
        softplus = jnp.maximum(-zw, 0.0) + jnp.log(1.0 + jnp.exp(-jnp.abs(zw)))
        ld_o[0] = -jnp.exp(-softplus - 0.5)
        iclr = _sigmoid(a0_ref[d:d + 1, :] + _dot(la[:, d * RW_LORA:(d + 1) * RW_LORA], a2_ref[d]))
        k_o[0] = k * (1.0 + (iclr - 1.0) * ka_ref[...])
        b_o[0] = kk * iclr


def rwkv_features(u_rw, cos, sin, mu, w0, w2, a0, a2, g2, k_k, k_a, ones):
    G, L, W_IN = u_rw.shape
    n_tiles = L // SEQ_TILE
    full = lambda a: pl.BlockSpec(a.shape, lambda g_, i: (0,) * a.ndim)
    tab = pl.BlockSpec((SEQ_TILE, WIDTH), lambda g_, i: (i, 0))
    params = (mu, w0, w2, a0, a2, g2, k_k, k_a, ones)
    out = jax.ShapeDtypeStruct((G, L, WIDTH), F32)
    return pl.pallas_call(
        _rw_feat_kernel,
        grid=(G, n_tiles),
        in_specs=_halo_specs(W_IN, n_tiles) + [tab, tab] + [full(p) for p in params],
        out_specs=[pl.BlockSpec((1, SEQ_TILE, WIDTH), lambda g_, i: (g_, i, 0))] * 10,
        out_shape=[out] * 10,
        compiler_params=_cparams("parallel", "parallel"),
        name="rwkv_features",
    )(u_rw, u_rw, u_rw, cos, sin, *params)


def _rw_chunk_operands(r_ref, ld_ref, k_ref, v_ref, kk_ref, b_ref, incl, reverse):
    C = CHUNK
    ld = ld_ref[0]
    ld_hi = ld.astype(BF16)
    ld_lo = (ld - ld_hi.astype(F32)).astype(BF16)
    tri = jnp.where(incl, 1.0, 0.0).astype(BF16)
    cum = (jnp.dot(tri, ld_hi, preferred_element_type=F32)
           + jnp.dot(tri, ld_lo, preferred_element_type=F32))
    tot = cum[0:1, :] if reverse else cum[C - 1:C, :]
    e_neg = jnp.exp(-cum)
    e_rem = jnp.exp(tot - cum)
    b = b_ref[0]
    k = k_ref[0]
    return dict(
        a_t=(-kk_ref[0] * jnp.exp(cum - ld)).astype(BF16),
        r_t=(r_ref[0] * jnp.exp(cum)).astype(BF16),
        b_t=(b * e_neg).astype(BF16),
        k_t=(k * e_neg).astype(BF16),
        b_p=(b * e_rem).astype(BF16),
        k_p=(k * e_rem).astype(BF16),
        p_c=jnp.exp(tot),
        v=v_ref[0].astype(BF16))


def _rw_scan_kernel(*refs):
    in_refs, y_refs, s_ref = refs[:12], refs[12:14], refs[14]
    C = CHUNK

    @pl.when(pl.program_id(1) == 0)
    def _():
        s_ref[...] = jnp.zeros_like(s_ref)

    row = lax.broadcasted_iota(jnp.int32, (C, C), 0)
    col = lax.broadcasted_iota(jnp.int32, (C, C), 1)
    eye = row == col
    chains = []
    for d in range(2):
        incl, strict = (row <= col, row < col) if d == 1 else (row >= col, row > col)
        ops = _rw_chunk_operands(*in_refs[6 * d:6 * d + 6], incl, d == 1)
        for h in range(HEADS):
            sl = slice(h * HEAD_DIM, (h + 1) * HEAD_DIM)
            ch = {name: val[:, sl] for name, val in ops.items()}
            ch.update(d=d, h=h, sl=sl, incl=incl, strict=strict)
            chains.append(ch)

    each = lambda fn: [fn(ch) for ch in chains]
    both = lambda fn, xs: [fn(ch, x) for ch, x in zip(chains, xs)]
    ar = each(lambda ch: jnp.concatenate([ch["a_t"], ch["r_t"]], axis=0))
    g_b = both(lambda ch, x: _dot_nt(x, ch["b_t"]), ar)
    g_k = both(lambda ch, x: _dot_nt(x, ch["k_t"]), ar)
    a_ab = both(lambda ch, g: jnp.where(ch["strict"], g[:C], 0.0), g_b)
    a_rb = both(lambda ch, g: jnp.where(ch["incl"], g[C:], 0.0), g_b)
    a_k = both(lambda ch, g: jnp.concatenate([jnp.where(ch["strict"], g[:C], 0.0),
                                              jnp.where(ch["incl"], g[C:], 0.0)], axis=0), g_k)
    av = both(lambda ch, a: _dot(a, ch["v"]), a_k)
    t_inv = [jnp.where(eye, 1.0, a) for a in a_ab]
    pw = a_ab
    for _ in range(int(math.log2(C)) - 1):
        pw = [_dot(p, p) for p in pw]
        t_inv = [t + _dot(t, p) for t, p in zip(t_inv, pw)]
    w1 = both(lambda ch, t: _dot(t, ch["a_t"]), t_inv)
    w2 = [_dot(t, x[:C]) for t, x in zip(t_inv, av)]
    q_p = [ch["r_t"].astype(F32) + _dot(a, w) for ch, a, w in zip(chains, a_rb, w1)]
    y_p = [x[C:] + _dot(a, w) for x, a, w in zip(av, a_rb, w2)]
    m_mat = both(lambda ch, w: jnp.where(eye, ch["p_c"], 0.0) + _dot_tn(w, ch["b_p"]), w1)
    n_mat = both(lambda ch, w: _dot_tn(w, ch["b_p"]) + _dot_tn(ch["v"], ch["k_p"]), w2)
    s0 = each(lambda ch: s_ref[ch["d"], ch["h"]])
    y = [_dot_nt(q, s) + yp for q, s, yp in zip(q_p, s0, y_p)]
    s1 = [_dot(s, m) + n for s, m, n in zip(s0, m_mat, n_mat)]
    for ch, y_c, s_c in zip(chains, y, s1):
        y_refs[ch["d"]][0, :, ch["sl"]] = y_c
        s_ref[ch["d"], ch["h"]] = s_c


def rwkv_scan(r, v, kk, ld0, k0, b0, ld1, k1, b1, *, batch):
    G, L, _ = r.shape
    lc = L // batch
    nc_ctx = lc // CHUNK
    nc_lat = L // CHUNK
    n_steps = nc_ctx + nc_lat

    def idx(reverse, bi, s):
        in_ctx = s < nc_ctx
        c_ctx = nc_ctx - 1 - s if reverse else s
        c_lat = n_steps - 1 - s if reverse else s - nc_ctx
        return (jnp.where(in_ctx, 0, bi + 1), jnp.where(in_ctx, bi * nc_ctx + c_ctx, c_lat), 0)

    fwd = pl.BlockSpec((1, CHUNK, WIDTH), functools.partial(idx, False))
    rev = pl.BlockSpec((1, CHUNK, WIDTH), functools.partial(idx, True))
    out = jax.ShapeDtypeStruct((G, L, WIDTH), F32)
    return pl.pallas_call(
        _rw_scan_kernel,
        grid=(batch, n_steps),
        in_specs=[fwd] * 6 + [rev] * 6,
        out_specs=[fwd, rev],
        out_shape=[out, out],
        scratch_shapes=[pltpu.VMEM((2, HEADS, HEAD_DIM, HEAD_DIM), F32)],
        compiler_params=_cparams("parallel", "arbitrary"),
        name="rwkv_scan",
    )(r, ld0, k0, v, kk, b0, r, ld1, k1, v, kk, b1)


def _rw_readout_kernel(y0_ref, y1_ref, r_ref, k0_ref, k1_ref, v_ref, g_ref, rk_ref, gg_ref, gb_ref,
                       ones_ref, o_ref):
    y = y0_ref[0] + y1_ref[0]
    ones = ones_ref[...]
    inv_n = 1.0 / HEAD_DIM
    mu = _dot_split(y, ones) * inv_n
    yc = y - mu
    var = _dot_split(yc * yc, ones) * inv_n
    yn = yc * lax.rsqrt(var + RW_GN_EPS) * gg_ref[...] + gb_ref[...]
    bonus = _dot_split(r_ref[0] * (k0_ref[0] + k1_ref[0]) * rk_ref[...], ones) * v_ref[0]
    o_ref[0] = (yn + bonus) * g_ref[0]


def rwkv_readout(y0, y1, r, k0, k1, v, g, r_k, gn_g, gn_b, ones, *, tm=512):
    G, L, _ = r.shape
    spec = pl.BlockSpec((1, tm, WIDTH), lambda g_, i: (g_, i, 0))
    full = lambda a: pl.BlockSpec(a.shape, lambda g_, i: (0,) * a.ndim)
    params = (r_k, gn_g, gn_b, ones)
    return pl.pallas_call(
        _rw_readout_kernel,
        grid=(G, L // tm),
        in_specs=[spec] * 7 + [full(p) for p in params],
        out_specs=spec,
        out_shape=jax.ShapeDtypeStruct((G, L, WIDTH), F32),
        compiler_params=_cparams("parallel", "parallel"),
        name="rwkv_readout",
    )(y0, y1, r, k0, k1, v, g, *params)


def _hy_short_kernel(u_ref, up_ref, un_ref, w_ref, b_ref, o_ref):
    u = u_ref[0]
    prev, nxt = _neighbours(u, up_ref, un_ref)
    o_ref[0] = b_ref[...] + w_ref[0:1, :] * prev + w_ref[1:2, :] * u + w_ref[2:3, :] * nxt


def hyena_short_conv(u_hy, w, b):
    G, L, Wd = u_hy.shape
    n_tiles = L // SEQ_TILE
    full = lambda a: pl.BlockSpec(a.shape, lambda g_, i: (0,) * a.ndim)
    return pl.pallas_call(
        _hy_short_kernel,
        grid=(G, n_tiles),
        in_specs=_halo_specs(Wd, n_tiles) + [full(w), full(b)],
        out_specs=pl.BlockSpec((1, SEQ_TILE, Wd), lambda g_, i: (g_, i, 0)),
        out_shape=jax.ShapeDtypeStruct((G, L, Wd), F32),
        compiler_params=_cparams("parallel", "parallel"),
        name="hyena_short_conv",
    )(u_hy, u_hy, u_hy, w, b)


@functools.lru_cache(maxsize=None)
def _dft_tables(n):
    nn = 2 * n
    nf = n + 256
    kf = np.arange(nf, dtype=np.int64)
    t = np.arange(n, dtype=np.int64)
    ang = 2.0 * np.pi * ((kf[:, None] * t[None, :]) % nn).astype(np.float64) / nn
    valid = (kf <= n)[:, None]
    cosm = np.where(valid, np.cos(ang), 0.0)
    sinm = np.where(valid, np.sin(ang), 0.0)
    ck = np.where((kf == 0) | (kf == n), 1.0, 2.0)[:, None] / nn
    fwd = np.stack([cosm, -sinm])
    inv = np.stack([(ck * cosm).T, (-ck * sinm).T])
    return fwd, inv, nf


def _hy_conv_kernel(z_ref, x_ref, fwd_ref, inv_ref, hre_ref, him_ref, skip_ref, o_ref, acc_ref):
    f = pl.program_id(1)

    @pl.when(f == 0)
    def _():
        acc_ref[...] = jnp.zeros_like(acc_ref)

    z = z_ref[0]
    zb = z.astype(BF16)
    zre = jnp.dot(fwd_ref[0], zb, preferred_element_type=F32)
    zim = jnp.dot(fwd_ref[1], zb, preferred_element_type=F32)
    hre = hre_ref[0]
    him = him_ref[0]
    yre = zre * hre - zim * him
    yim = zre * him + zim * hre
    acc_ref[...] += (jnp.dot(inv_ref[0], yre.astype(BF16), preferred_element_type=F32)
                     + jnp.dot(inv_ref[1], yim.astype(BF16), preferred_element_type=F32))

    @pl.when(f == pl.num_programs(1) - 1)
    def _():
        o_ref[0] = x_ref[0] * (acc_ref[...] + skip_ref[0] * z)


def hyena_long_conv(z_arr, z_blk, x_arr, x_blk, hre, him, skip, order, fwd, inv, *, n_seq, n, tf):
    nf = fwd.shape[1]
    zo, zc = z_blk
    xo, xc = x_blk
    return pl.pallas_call(
        _hy_conv_kernel,
        grid=(n_seq, nf // tf),
        in_specs=[pl.BlockSpec((1, n, HY_WIDTH), lambda s, f: (s + zo, 0, zc)),
                  pl.BlockSpec((1, n, HY_WIDTH), lambda s, f: (s + xo, 0, xc)),
                  pl.BlockSpec((2, tf, n), lambda s, f: (0, f, 0)),
                  pl.BlockSpec((2, n, tf), lambda s, f: (0, 0, f)),
                  pl.BlockSpec((1, tf, HY_WIDTH), lambda s, f: (order, f, 0)),
                  pl.BlockSpec((1, tf, HY_WIDTH), lambda s, f: (order, f, 0)),
                  pl.BlockSpec((1, 1, HY_WIDTH), lambda s, f: (order, 0, 0))],
        out_specs=pl.BlockSpec((1, n, HY_WIDTH), lambda s, f: (s, 0, 0)),
        out_shape=jax.ShapeDtypeStruct((n_seq, n, HY_WIDTH), F32),
        scratch_shapes=[pltpu.VMEM((n, HY_WIDTH), F32)],
        compiler_params=_cparams("parallel", "arbitrary"),
        name=f"hyena_long_conv_n{n}_o{order}",
    )(z_arr, x_arr, fwd, inv, hre, him, skip)


def hyena_filter_spectrum(n, w1, b1, w2, b2, w3, freq, fwd):
    hp = lax.Precision.HIGHEST
    pos = jnp.arange(n, dtype=F32)
    t = pos / max(n - 1, 1)
    bands = (HY_EMB_DIM - 1) // 2
    fr = jnp.linspace(1e-4, bands - 1, bands, dtype=F32)
    ang = (2 * math.pi / n) * pos[:, None] * fr[None, :]
    z = jnp.concatenate([t[:, None], jnp.cos(ang), -jnp.sin(ang)], axis=-1)
    h = jnp.sin(freq * (jnp.dot(z, w1, precision=hp) + b1))
    h = jnp.sin(freq * (jnp.dot(h, w2, precision=hp) + b2))
    h = matmul(h, w3, tn=w3.shape[1], split=True, name="hyena_filter_out")
    h = h.reshape(n, HY_ORDER, 2, HY_WIDTH)
    max_decay = math.log(HY_DECAY_TARGET) / HY_FAST_DECAY
    min_decay = math.log(HY_DECAY_TARGET) / HY_SLOW_DECAY
    deltas = jnp.abs(jnp.linspace(min_decay, max_decay, HY_WIDTH, dtype=F32))
    h = h * jnp.exp(-t[:, None] * deltas)[:, None, None, :]
    h_fwd = h[:, :, 0]
    h_bwd = h[:, :, 1] * (pos > 0).astype(F32)[:, None, None]
    l1 = jnp.sum(jnp.abs(h_fwd), axis=0) + jnp.sum(jnp.abs(h_bwd), axis=0)
    even = ((h_fwd + h_bwd) / l1).reshape(n, HY_ORDER * HY_WIDTH)
    odd = ((h_fwd - h_bwd) / l1).reshape(n, HY_ORDER * HY_WIDTH)
    nf = fwd.shape[1]
    hre = matmul(fwd[0], even, tn=HY_ORDER * HY_WIDTH, name="hyena_filter_dft_re")
    him = matmul(fwd[1], odd, tn=HY_ORDER * HY_WIDTH, name="hyena_filter_dft_im")
    to_ofc = lambda a: jnp.moveaxis(a.reshape(nf, HY_ORDER, HY_WIDTH), 1, 0)
    return to_ofc(hre), to_ofc(him)


def _na_prep_kernel(u_ref, qg_ref, kg_ref, ones_ref, q_o, k_o, v_o):
    u = u_ref[0]
    ones = ones_ref[...]
    q = u[:, 0:WIDTH]
    k = u[:, WIDTH:2 * WIDTH]
    inv_n = 1.0 / HEAD_DIM
    qn = q * lax.rsqrt(_dot_split(q * q, ones) * inv_n + NORM_EPS) * qg_ref[...]
    kn = k * lax.rsqrt(_dot_split(k * k, ones) * inv_n + NORM_EPS) * kg_ref[...]
    q_o[0] = (qn * (HEAD_DIM ** -0.5)).astype(BF16)
    k_o[0] = kn.astype(BF16)
    v_o[0] = u[:, 2 * WIDTH:3 * WIDTH].astype(BF16)


def na_prepare(u_na, q_gain, k_gain, ones, *, tm=512):
    G, L, Wd = u_na.shape
    full = lambda a: pl.BlockSpec(a.shape, lambda g_, i: (0,) * a.ndim)
    spec = pl.BlockSpec((1, tm, WIDTH), lambda g_, i: (g_, i, 0))
    out = jax.ShapeDtypeStruct((G, L, WIDTH), BF16)
    return pl.pallas_call(
        _na_prep_kernel,
        grid=(G, L // tm),
        in_specs=[pl.BlockSpec((1, tm, Wd), lambda g_, i: (g_, i, 0)), full(q_gain), full(k_gain), full(ones)],
        out_specs=[spec] * 3,
        out_shape=[out] * 3,
        compiler_params=_cparams("parallel", "parallel"),
        name="na_prepare",
    )(u_na, q_gain, k_gain, ones)


def _na_lat_kernel(q_ref, k_ref, v_ref, kc_ref, vc_ref, bias_ref, o_ref, *, rows, wr):
    i = pl.program_id(1)
    r0 = jnp.clip(i - wr // 2, 0, rows - wr)
    off = r0 - i + (NA_WIN_ROWS - 1)
    start = pl.multiple_of(r0 * GRID_W, GRID_W)
    q = q_ref[0]
    kw = k_ref[0, pl.ds(start, wr * GRID_W), :]
    vw = v_ref[0, pl.ds(start, wr * GRID_W), :]
    kc = kc_ref[0]
    vc = vc_ref[0]
    sls = [slice(h * HEAD_DIM, (h + 1) * HEAD_DIM) for h in range(HEADS)]
    s_loc = [_dot_nt(q[:, sl], kw[:, sl]) + bias_ref[h, off] for h, sl in enumerate(sls)]
    s_ctx = [_dot_nt(q[:, sl], kc[:, sl]) for sl in sls]
    m = [jnp.maximum(jnp.max(a, axis=-1, keepdims=True), jnp.max(b, axis=-1, keepdims=True))
         for a, b in zip(s_loc, s_ctx)]
    p_loc = [jnp.exp(a - mm) for a, mm in zip(s_loc, m)]
    p_ctx = [jnp.exp(b - mm) for b, mm in zip(s_ctx, m)]
    den = [jnp.sum(a, axis=-1, keepdims=True) + jnp.sum(b, axis=-1, keepdims=True) for a, b in zip(p_loc, p_ctx)]
    o = [_dot(a, vw[:, sl]) + _dot(b, vc[:, sl]) for a, b, sl in zip(p_loc, p_ctx, sls)]
    for sl, oo, dd in zip(sls, o, den):
        o_ref[0, :, sl] = oo / dd


def _na_bias_table(rpb, rows, wr):
    hp = lax.Precision.HIGHEST
    offs = np.arange(NA_WIN_ROWS) - (NA_WIN_ROWS - 1)
    dr = offs[:, None] + np.arange(wr)[None, :] + NA_WIN_ROWS - 1
    c_ar = np.arange(GRID_W)
    dc = np.clip(c_ar[None, :] - c_ar[:, None] + NA_WIN_COLS - 1, 0, 2 * NA_WIN_COLS - 2)
    c0 = np.clip(c_ar - NA_WIN_COLS // 2, 0, GRID_W - NA_WIN_COLS)
    in_win = (c_ar[None, :] >= c0[:, None]) & (c_ar[None, :] < c0[:, None] + NA_WIN_COLS)
    oh_r = jnp.asarray(dr[..., None] == np.arange(2 * NA_WIN_ROWS - 1), dtype=F32)
    oh_c = jnp.asarray(dc[..., None] == np.arange(2 * NA_WIN_COLS - 1), dtype=F32)
    by_row = jnp.einsum("hab,ora->horb", rpb.astype(F32), oh_r, precision=hp)
    bias = jnp.einsum("horb,qkb->hoqrk", by_row, oh_c, precision=hp)
    bias = jnp.where(in_win[None, None, :, None, :], bias, NEG_BIG)
    return bias.reshape(HEADS, NA_WIN_ROWS, GRID_W, wr * GRID_W)


def na_latent(qn, kn, vb, bias, *, batch):
    G, L, _ = qn.shape
    lc = L // batch
    rows = L // GRID_W
    wr = min(NA_WIN_ROWS, rows)
    qspec = pl.BlockSpec((1, GRID_W, WIDTH), lambda b_, i: (b_ + 1, i, 0))
    seq = pl.BlockSpec((1, L, WIDTH), lambda b_, i: (b_ + 1, 0, 0))
    ctx = pl.BlockSpec((1, lc, WIDTH), lambda b_, i: (0, b_, 0))
    return pl.pallas_call(
        functools.partial(_na_lat_kernel, rows=rows, wr=wr),
        grid=(batch, rows),
        in_specs=[qspec, seq, seq, ctx, ctx, pl.BlockSpec(bias.shape, lambda b_, i: (0, 0, 0, 0))],
        out_specs=pl.BlockSpec((1, GRID_W, WIDTH), lambda b_, i: (b_, i, 0)),
        out_shape=jax.ShapeDtypeStruct((batch, L, WIDTH), F32),
        compiler_params=_cparams("parallel", "arbitrary"),
        name="na_latent",
    )(qn, kn, vb, kn, vb, bias)


def _na_ctx_kernel(q_ref, k_ref, v_ref, o_ref):
    q = q_ref[0]
    k = k_ref[0]
    v = v_ref[0]
    for h in range(HEADS):
        sl = slice(h * HEAD_DIM, (h + 1) * HEAD_DIM)
        s = _dot_nt(q[:, sl], k[:, sl])
        p = jnp.exp(s - jnp.max(s, axis=-1, keepdims=True))
        o_ref[0, :, sl] = _dot(p, v[:, sl]) / jnp.sum(p, axis=-1, keepdims=True)


def na_context(qn, kn, vb, *, batch):
    G, L, _ = qn.shape
    lc = L // batch
    spec = pl.BlockSpec((1, lc, WIDTH), lambda b_: (0, b_, 0))
    return pl.pallas_call(
        _na_ctx_kernel,
        grid=(batch,),
        in_specs=[spec] * 3,
        out_specs=spec,
        out_shape=jax.ShapeDtypeStruct((1, L, WIDTH), F32),
        compiler_params=_cparams("parallel"),
        name="na_context",
    )(qn, kn, vb)


def _merge_kernel(x_ref, gl_ref, rw_ref, hy_ref, na_ref, g1_ref, wr_ref, wh_ref, wn_ref, wo_ref, o_ref):
    D = x_ref.shape[-1]
    gl = gl_ref[0]
    m = (_sigmoid(gl[:, 0:D]) * _dot(rw_ref[0], wr_ref[...])
         + _sigmoid(gl[:, D:2 * D]) * _dot(hy_ref[0], wh_ref[...])
         + _sigmoid(gl[:, 2 * D:3 * D]) * _dot(na_ref[0], wn_ref[...]))
    o_ref[0] = x_ref[0] + g1_ref[0] * _dot(m, wo_ref[...])


def merge_branches(xs, u_gate, y_rw, y_hy, y_na, gate1, w_rw, w_hy, w_na, w_o, *, g_off, tm=512):
    G, L, D = xs.shape
    tok = lambda width: pl.BlockSpec((1, tm, width), lambda g_, i: (g_ + g_off, i, 0))
    full = lambda a: pl.BlockSpec(a.shape, lambda g_, i: (0,) * a.ndim)
    weights = (w_rw, w_hy, w_na, w_o)
    return pl.pallas_call(
        _merge_kernel,
        grid=(G - g_off, L // tm),
        in_specs=[tok(D), tok(3 * D), tok(WIDTH), tok(HY_WIDTH), tok(WIDTH),
                  pl.BlockSpec((1, 1, D), lambda g_, i: (g_ + g_off, 0, 0))] + [full(w) for w in weights],
        out_specs=pl.BlockSpec((1, tm, D), lambda g_, i: (g_, i, 0)),
        out_shape=jax.ShapeDtypeStruct((G - g_off, L, D), F32),
        compiler_params=_cparams("parallel", "parallel"),
        name="merge_branches",
    )(xs, u_gate, y_rw, y_hy, y_na, gate1, *weights)


def _ffn_kernel(x_ref, sh_ref, sc_ref, g_ref, g2_ref, w1_ref, w3_ref, w2_ref, o_ref, h_ref, acc_ref):
    f = pl.program_id(2)

    @pl.when(f == 0)
    def _():
        h_ref[...] = _norm_mod(x_ref[0], g_ref[...], sh_ref[0], sc_ref[0]).astype(BF16)
        acc_ref[...] = jnp.zeros_like(acc_ref)

    h = h_ref[...]
    a = jnp.dot(h, w1_ref[...], preferred_element_type=F32)
    b = jnp.dot(h, w3_ref[...], preferred_element_type=F32)
    acc_ref[...] += _dot(_silu(a) * b, w2_ref[...])

    @pl.when(f == pl.num_programs(2) - 1)
    def _():
        o_ref[0] = x_ref[0] + g2_ref[0] * acc_ref[...]


def ffn_dense(xs, shift, scale, g, gate2, w1, w3, w2, *, tm=512, tf):
    G, L, D = xs.shape
    FF = w1.shape[1]
    mod = pl.BlockSpec((1, 1, D), lambda g_, i, f: (g_, 0, 0))
    return pl.pallas_call(
        _ffn_kernel,
        grid=(G, L // tm, FF // tf),
        in_specs=[pl.BlockSpec((1, tm, D), lambda g_, i, f: (g_, i, 0)), mod, mod,
                  pl.BlockSpec((1, D), lambda g_, i, f: (0, 0)), mod,
                  pl.BlockSpec((D, tf), lambda g_, i, f: (0, f)),
                  pl.BlockSpec((D, tf), lambda g_, i, f: (0, f)),
                  pl.BlockSpec((tf, D), lambda g_, i, f: (f, 0))],
        out_specs=pl.BlockSpec((1, tm, D), lambda g_, i, f: (g_, i, 0)),
        out_shape=jax.ShapeDtypeStruct((G, L, D), F32),
        scratch_shapes=[pltpu.VMEM((tm, D), BF16), pltpu.VMEM((tm, D), F32)],
        compiler_params=_cparams("parallel", "parallel", "arbitrary"),
        name="ffn_dense",
    )(xs, shift, scale, g, gate2, w1, w3, w2)


def _moe_route_kernel(x_ref, sh_ref, sc_ref, g_ref, rt_ref, h_o, gates_o, rank_o, rank_t_o, cnt_o,
                      cnt_row, cnt_col):
    i = pl.program_id(1)

    @pl.when(i == 0)
    def _():
        cnt_row[...] = jnp.zeros_like(cnt_row)
        cnt_col[...] = jnp.zeros_like(cnt_col)

    h = _norm_mod(x_ref[0], g_ref[...], sh_ref[0], sc_ref[0])
    h_o[0] = h.astype(BF16)
    rt = rt_ref[...]
    hh = h.astype(BF16)
    hl = (h - hh.astype(F32)).astype(BF16)
    rh = rt.astype(BF16)
    rl = (rt - rh.astype(F32)).astype(BF16)
    logits = (jnp.dot(hh, rh, preferred_element_type=F32) + jnp.dot(hl, rh, preferred_element_type=F32)
              + jnp.dot(hh, rl, preferred_element_type=F32))
    lane = lax.broadcasted_iota(jnp.int32, logits.shape, 1)
    logits = jnp.where(lane < N_EXPERTS, logits, NEG_BIG)
    m1 = jnp.max(logits, axis=-1, keepdims=True)
    i1 = jnp.min(jnp.where(logits == m1, lane, LANES), axis=-1, keepdims=True)
    rest = jnp.where(lane == i1, NEG_BIG, logits)
    m2 = jnp.max(rest, axis=-1, keepdims=True)
    i2 = jnp.min(jnp.where(rest == m2, lane, LANES), axis=-1, keepdims=True)
    e2 = jnp.exp(m2 - m1)
    gates = jnp.where(lane == i1, 1.0 / (1.0 + e2), 0.0) + jnp.where(lane == i2, e2 / (1.0 + e2), 0.0)
    gates_o[0] = gates

    tm = gates.shape[0]
    sel = gates > 0.0
    m_tok = jnp.where(sel, 1.0, 0.0)
    m_exp = m_tok.T[:2 * HALO]
    row = lax.broadcasted_iota(jnp.int32, (tm, tm), 0)
    col = lax.broadcasted_iota(jnp.int32, (tm, tm), 1)
    before = jnp.where(col < row, 1.0, 0.0).astype(BF16)
    after = jnp.where(row < col, 1.0, 0.0).astype(BF16)
    rank = jnp.dot(before, m_tok.astype(BF16), preferred_element_type=F32) + cnt_row[...]
    rank_t = jnp.dot(m_exp.astype(BF16), after, preferred_element_type=F32)[:HALO]
    rank_t = rank_t + jnp.tile(cnt_col[...], (1, tm // LANES))
    rank_o[0] = jnp.where(sel, rank, NEG_BIG)
    rank_t_o[0] = jnp.where(m_exp[:HALO] > 0.0, rank_t, NEG_BIG)
    cnt_row[...] += jnp.sum(m_tok, axis=0, keepdims=True)
    cnt_col[...] += jnp.sum(m_exp[:HALO], axis=1, keepdims=True)
    cnt_o[0] = cnt_row[...]


def _moe_expert_kernel(cnt_ref, h_ref, gates_ref, rank_ref, rank_t_ref, x_ref, g2_ref, w1_ref, w3_ref, w2_ref,
                       o_ref, hs_ref, ys_ref, *, rb):
    gi = pl.program_id(0)
    e = pl.program_id(1)
    f = pl.program_id(2)
    n_e = cnt_ref[gi * pl.num_programs(1) + e]
    nblk = (n_e + rb - 1) // rb
    tm = h_ref.shape[1]
    rows = lambda j: pl.ds(pl.multiple_of(j * rb, rb), rb)

    @pl.when(jnp.logical_and(e == 0, f == 0))
    def _():
        o_ref[...] = jnp.zeros_like(o_ref)

    @pl.when(f == 0)
    def _():
        rank_t = rank_t_ref[0, pl.ds(e, 1), :]
        sub = lax.broadcasted_iota(jnp.int32, (rb, tm), 0).astype(F32)

        def gather(j, carry):
            pick = jnp.where(rank_t == sub + (j * rb).astype(F32), 1.0, 0.0).astype(BF16)
            hs_ref[rows(j), :] = jnp.dot(pick, h_ref[0], preferred_element_type=F32).astype(BF16)
            return carry

        lax.fori_loop(0, nblk, gather, 0)

    def expert_block(j):
        hs = hs_ref[rows(j), :]
        a = jnp.dot(hs, w1_ref[0], preferred_element_type=F32)
        b = jnp.dot(hs, w3_ref[0], preferred_element_type=F32)
        return _dot(_silu(a) * b, w2_ref[0])

    @pl.when(f == 0)
    def _():
        def first(j, carry):
            ys_ref[rows(j), :] = expert_block(j)
            return carry

        lax.fori_loop(0, nblk, first, 0)

    @pl.when(f > 0)
    def _():
        def more(j, carry):
            ys_ref[rows(j), :] += expert_block(j)
            return carry

        lax.fori_loop(0, nblk, more, 0)

    last_f = f == pl.num_programs(2) - 1

    @pl.when(last_f)
    def _():
        lane = lax.broadcasted_iota(jnp.int32, (tm, LANES), 1)
        gate = jnp.sum(jnp.where(lane == e, gates_ref[0], 0.0), axis=-1, keepdims=True)
        rank = jnp.sum(jnp.where(lane == e, rank_ref[0], 0.0), axis=-1, keepdims=True)
        slot = lax.broadcasted_iota(jnp.int32, (tm, rb), 1).astype(F32)

        def scatter(j, carry):
            put = jnp.where(rank == slot + (j * rb).astype(F32), gate, 0.0)
            o_ref[0] += _dot(put, ys_ref[rows(j), :])
            return carry

        lax.fori_loop(0, nblk, scatter, 0)

    @pl.when(jnp.logical_and(e == pl.num_programs(1) - 1, last_f))
    def _():
        o_ref[0] = x_ref[0] + g2_ref[0] * o_ref[0]


def moe_ffn(xs, shift, scale, g, gate2, router, w1, w3, w2, *, g_off, tm_route=512, rb=256, tf):
    G, L, D = xs.shape
    E, _, FF = w1.shape
    n_g = G - g_off
    mod = pl.BlockSpec((1, 1, D), lambda g_, i: (g_ + g_off, 0, 0))
    tok = lambda width: pl.BlockSpec((1, tm_route, width), lambda g_, i: (g_, i, 0))
    h, gates, rank, rank_t, cnt = pl.pallas_call(
        _moe_route_kernel,
        grid=(n_g, L // tm_route),
        in_specs=[pl.BlockSpec((1, tm_route, D), lambda g_, i: (g_ + g_off, i, 0)), mod, mod,
                  pl.BlockSpec((1, D), lambda g_, i: (0, 0)),
                  pl.BlockSpec((D, LANES), lambda g_, i: (0, 0))],
        out_specs=[tok(D), tok(LANES), tok(LANES),
                   pl.BlockSpec((1, HALO, tm_route), lambda g_, i: (g_, 0, i)),
                   pl.BlockSpec((1, 1, LANES), lambda g_, i: (g_, 0, 0))],
        out_shape=[jax.ShapeDtypeStruct((n_g, L, D), BF16),
                   jax.ShapeDtypeStruct((n_g, L, LANES), F32),
                   jax.ShapeDtypeStruct((n_g, L, LANES), F32),
                   jax.ShapeDtypeStruct((n_g, HALO, L), F32),
                   jax.ShapeDtypeStruct((n_g, 1, LANES), F32)],
        scratch_shapes=[pltpu.VMEM((1, LANES), F32), pltpu.VMEM((HALO, LANES), F32)],
        compiler_params=_cparams("parallel", "arbitrary"),
        name="moe_route",
    )(xs, shift, scale, g, router)
    counts = cnt[:, 0, :E].astype(jnp.int32).reshape(-1)

    once = pl.Buffered(1)
    grp = lambda width: pl.BlockSpec((1, L, width), lambda g_, e, f, c: (g_, 0, 0), pipeline_mode=once)
    return pl.pallas_call(
        functools.partial(_moe_expert_kernel, rb=rb),
        grid_spec=pltpu.PrefetchScalarGridSpec(
            num_scalar_prefetch=1,
            grid=(n_g, E, FF // tf),
            in_specs=[grp(D), grp(LANES), grp(LANES),
                      pl.BlockSpec((1, HALO, L), lambda g_, e, f, c: (g_, 0, 0), pipeline_mode=once),
                      pl.BlockSpec((1, L, D), lambda g_, e, f, c: (g_ + g_off, 0, 0), pipeline_mode=once),
                      pl.BlockSpec((1, 1, D), lambda g_, e, f, c: (g_ + g_off, 0, 0)),
                      pl.BlockSpec((1, D, tf), lambda g_, e, f, c: (e, 0, f)),
                      pl.BlockSpec((1, D, tf), lambda g_, e, f, c: (e, 0, f)),
                      pl.BlockSpec((1, tf, D), lambda g_, e, f, c: (e, f, 0))],
            out_specs=pl.BlockSpec((1, L, D), lambda g_, e, f, c: (g_, 0, 0), pipeline_mode=once),
            scratch_shapes=[pltpu.VMEM((L + rb, D), BF16), pltpu.VMEM((L + rb, D), F32)]),
        out_shape=jax.ShapeDtypeStruct((n_g, L, D), F32),
        compiler_params=pltpu.CompilerParams(dimension_semantics=("parallel", "arbitrary", "arbitrary"),
                                             vmem_limit_bytes=60 * 1024 * 1024),
        name="moe_experts",
    )(counts, h, gates, rank, rank_t, xs, gate2, w1, w3, w2)


def _rope_tables(n):
    half = HEAD_DIM // 2
    nf = half // 2
    t = np.arange(n)
    inv = ROPE_BASE ** (-np.arange(nf, dtype=np.float64) / nf)
    d = np.arange(HEAD_DIM)
    pos = np.where(d[None, :] < half, (t // GRID_W)[:, None], (t % GRID_W)[:, None]).astype(np.float64)
    ang = pos * inv[d % nf][None, :]
    sign = np.where(d % half < nf, -1.0, 1.0)
    cos = np.tile(np.cos(ang), (1, HEADS))
    sin = np.tile(np.sin(ang) * sign[None, :], (1, HEADS))
    return jnp.asarray(cos, dtype=F32), jnp.asarray(sin, dtype=F32)


def kernel(x, c, ctx, c_ctx, ada_w, ada_b, norm1_g, norm2_g, w_in, rw_shift, rw_w0, rw_w2, rw_a0, rw_a2, rw_g2, rw_kk, rw_ka, rw_rk, rw_gn_g, rw_gn_b, hy_conv_w, hy_conv_b, hy_f_w1, hy_f_b1, hy_f_w2, hy_f_b2, hy_f_w3, hy_f_freq, hy_skip, na_q_gain, na_k_gain, na_rpb, w_br_rw, w_br_hy, w_br_na, w_out, ff_w1, ff_w3, ff_w2, moe_router, moe_w1, moe_w3, moe_w2):
    B, L, D = x.shape
    lc = ctx.shape[1]
    depth = ada_w.shape[0]
    assert B * lc == L and lc == SEQ_TILE and L % GRID_W == 0
    G = B + 1
    rows = L // GRID_W
    wr = min(NA_WIN_ROWS, rows)
    rw_in = 3 * WIDTH + 4 * RW_LORA + RW_GATE_LORA
    hy_in = (HY_ORDER + 1) * HY_WIDTH
    na_in = 3 * WIDTH
    splits = (rw_in, rw_in + hy_in, rw_in + hy_in + na_in)

    xs = jnp.concatenate([ctx.reshape(1, L, D), x], axis=0)
    act = jnp.concatenate([c_ctx[None, :], c], axis=0)
    act = act * _sigmoid(act)
    act = jnp.pad(act, ((0, (-G) % 16), (0, 0)))
    ones = _head_ones()
    cos, sin = _rope_tables(L)
    fwd_lat, inv_lat, _ = _dft_tables(L)
    fwd_ctx, inv_ctx, _ = _dft_tables(lc)
    fwd_lat, inv_lat, fwd_ctx, inv_ctx = (jnp.asarray(a, dtype=F32).astype(BF16)
                                          for a in (fwd_lat, inv_lat, fwd_ctx, inv_ctx))
    row2 = lambda a: a.reshape(1, -1)

    for li in range(depth):
        last = li == depth - 1
        mods = matmul(act, ada_w[li], tn=1024, split=True, name="adaln")[:G] + ada_b[li]
        sh1, sc1, g1, sh2, sc2, g2 = (m.reshape(G, 1, D) for m in jnp.split(mods, 6, axis=-1))

        w_in_b = w_in[li].astype(BF16)
        n1 = row2(norm1_g[li])
        proj = lambda lo, hi, nm: norm_mod_matmul(xs, sh1, sc1, n1, w_in_b[:, lo:hi], name=nm)
        u_rw = proj(0, splits[0], "proj_rw")
        u_hy = proj(splits[0], splits[1], "proj_hy")
        u_na = proj(splits[1], splits[2], "proj_na")
        u_gate = proj(splits[2], w_in_b.shape[1], "proj_gate")

        r, v, kk, g, ld0, ld1, k0, k1, b0, b1 = rwkv_features(
            u_rw, cos, sin, rw_shift[li], rw_w0[li], rw_w2[li].astype(BF16), rw_a0[li], rw_a2[li].astype(BF16),
            rw_g2[li].astype(BF16), row2(rw_kk[li]), row2(rw_ka[li]), ones)
        y0, y1 = rwkv_scan(r, v, kk, ld0, k0, b0, ld1, k1, b1, batch=B)
        y_rw = rwkv_readout(y0, y1, r, k0, k1, v, g, row2(rw_rk[li]), row2(rw_gn_g[li]), row2(rw_gn_b[li]), ones)

        hy = hyena_short_conv(u_hy, hy_conv_w[li], row2(hy_conv_b[li]))
        f_args = (hy_f_w1[li], hy_f_b1[li], hy_f_w2[li], hy_f_b2[li], hy_f_w3[li], hy_f_freq[li])
        skip = hy_skip[li].reshape(HY_ORDER, 1, HY_WIDTH)
        hre, him = hyena_filter_spectrum(L, *f_args, fwd_lat)
        conv = functools.partial(hyena_long_conv, skip=skip, fwd=fwd_lat, inv=inv_lat, n_seq=B, n=L, tf=384)
        z1 = conv(hy, (1, 0), hy, (1, 1), hre, him, order=0)
        y_hy = conv(z1, (0, 0), hy, (1, 2), hre, him, order=1)
        if last:
            y_hy = jnp.concatenate([jnp.zeros((1, L, HY_WIDTH), F32), y_hy], axis=0)
        else:
            hy_c = hy[0].reshape(B, lc, hy_in)
            hre_c, him_c = hyena_filter_spectrum(lc, *f_args, fwd_ctx)
            conv_c = functools.partial(hyena_long_conv, skip=skip, fwd=fwd_ctx, inv=inv_ctx, n_seq=B, n=lc,
                                       tf=fwd_ctx.shape[1])
            z1c = conv_c(hy_c, (0, 0), hy_c, (0, 1), hre_c, him_c, order=0)
            y_hy_c = conv_c(z1c, (0, 0), hy_c, (0, 2), hre_c, him_c, order=1)
            y_hy = jnp.concatenate([y_hy_c.reshape(1, L, HY_WIDTH), y_hy], axis=0)

        qn, kn, vb = na_prepare(u_na, row2(jnp.tile(na_q_gain[li], HEADS)), row2(jnp.tile(na_k_gain[li], HEADS)), ones)
        bias = _na_bias_table(na_rpb[li], rows, wr)
        y_na = na_latent(qn, kn, vb, bias, batch=B)
        y_na_c = jnp.zeros((1, L, WIDTH), F32) if last else na_context(qn, kn, vb, batch=B)
        y_na = jnp.concatenate([y_na_c, y_na], axis=0)

        g_off = 1 if last else 0
        xs = merge_branches(xs, u_gate, y_rw, y_hy, y_na, g1, w_br_rw[li].astype(BF16), w_br_hy[li].astype(BF16),
                            w_br_na[li].astype(BF16), w_out[li].astype(BF16), g_off=g_off)
        if last:
            sh2, sc2, g2 = sh2[1:], sc2[1:], g2[1:]

        n2 = row2(norm2_g[li])
        if li % 2 == 0:
            j = li // 2
            xs = ffn_dense(xs, sh2, sc2, n2, g2, ff_w1[j].astype(BF16), ff_w3[j].astype(BF16), ff_w2[j].astype(BF16),
                           tf=ff_w1.shape[2] // 2)
        else:
            j = li // 2
            router = jnp.pad(moe_router[j], ((0, 0), (0, LANES - N_EXPERTS)))
            xs = moe_ffn(xs, sh2, sc2, n2, g2, router, moe_w1[j].astype(BF16), moe_w3[j].astype(BF16),
                         moe_w2[j].astype(BF16), g_off=0, tf=moe_w1.shape[3] // 4)
    return xs if depth == 0 else xs[-B:]
```

```python
import functools
import math

import numpy as np
import jax
import jax.numpy as jnp
from jax import lax
from jax.experimental import pallas as pl
from jax.experimental.pallas import tpu as pltpu

F32 = jnp.float32
BF16 = jnp.bfloat16

NORM_EPS = 1e-6
ROPE_BASE = 10000.0
GRID_W = 64
HEADS = 6
HEAD_DIM = 64
WIDTH = HEADS * HEAD_DIM
RW_LORA = 64
RW_GATE_LORA = 128
RW_GN_EPS = 64e-5
HY_WIDTH = 256
HY_ORDER = 2
HY_EMB_DIM = 33
HY_DECAY_TARGET = 1e-2
HY_FAST_DECAY = 0.3
HY_SLOW_DECAY = 1.5
NA_WIN_ROWS = 8
NA_WIN_COLS = 16
N_EXPERTS = 8
LANES = 128
SEQ_TILE = 256
HALO = 8
CHUNK = 64
SCAN_CHUNKS = 2
HY_SEQS = 2
NA_ROWS = 2
VMEM_LIMIT = 56 * 1024 * 1024
NEG_BIG = -1e30


def _cparams(*sem):
    return pltpu.CompilerParams(dimension_semantics=sem, vmem_limit_bytes=VMEM_LIMIT)


def _dot(a, b):
    return jnp.dot(a.astype(BF16), b.astype(BF16), preferred_element_type=F32)


def _dot_nt(a, b):
    return lax.dot_general(a.astype(BF16), b.astype(BF16), (((1,), (1,)), ((), ())),
                           preferred_element_type=F32)


def _dot_tn(a, b):
    return lax.dot_general(a.astype(BF16), b.astype(BF16), (((0,), (0,)), ((), ())),
                           preferred_element_type=F32)


def _dot_split(a, b_exact):
    hi = a.astype(BF16)
    lo = (a - hi.astype(F32)).astype(BF16)
    return (jnp.dot(hi, b_exact, preferred_element_type=F32)
            + jnp.dot(lo, b_exact, preferred_element_type=F32))


def _sigmoid(x):
    return 1.0 / (1.0 + jnp.exp(-x))


def _silu(x):
    return x * _sigmoid(x)


def _head_ones():
    h = np.arange(WIDTH) // HEAD_DIM
    return jnp.asarray((h[:, None] == h[None, :]).astype(np.float32), dtype=BF16)


def _mm_kernel(x_ref, w_ref, o_ref, *, split):
    x = x_ref[...]
    w = w_ref[...]
    if split:
        xh = x.astype(BF16)
        xl = (x - xh.astype(F32)).astype(BF16)
        wh = w.astype(BF16)
        wl = (w - wh.astype(F32)).astype(BF16)
        o_ref[...] = (jnp.dot(xh, wh, preferred_element_type=F32)
                      + jnp.dot(xl, wh, preferred_element_type=F32)
                      + jnp.dot(xh, wl, preferred_element_type=F32))
    else:
        o_ref[...] = jnp.dot(x.astype(BF16), w.astype(BF16), preferred_element_type=F32)


def matmul(x, w, *, tn, split=False, name="matmul"):
    m, k = x.shape
    n = w.shape[1]
    assert n % tn == 0
    return pl.pallas_call(
        functools.partial(_mm_kernel, split=split),
        grid=(n // tn,),
        in_specs=[pl.BlockSpec((m, k), lambda j: (0, 0)),
                  pl.BlockSpec((k, tn), lambda j: (0, j))],
        out_specs=pl.BlockSpec((m, tn), lambda j: (0, j)),
        out_shape=jax.ShapeDtypeStruct((m, n), F32),
        compiler_params=_cparams("parallel"),
        name=name,
    )(x, w)


def _norm_mod(x, g, shift, scale):
    ms = jnp.mean(x * x, axis=-1, keepdims=True)
    h = x * lax.rsqrt(ms + NORM_EPS) * g
    return h * (1.0 + scale) + shift


def _proj_kernel(x_ref, sh_ref, sc_ref, g_ref, w_ref, qg_ref, kg_ref, ones_ref,
                 rw_o, hy_o, q_o, k_o, v_o, gate_o, *, splits):
    h = _norm_mod(x_ref[0], g_ref[...], sh_ref[0], sc_ref[0]).astype(BF16)
    s_rw, s_hy, s_na = splits
    proj = lambda lo, hi: jnp.dot(h, w_ref[:, lo:hi], preferred_element_type=F32)
    rw_o[0] = proj(0, s_rw)
    hy_o[0] = proj(s_rw, s_hy)
    gate_o[0] = proj(s_na, w_ref.shape[1]).astype(BF16)
    ones = ones_ref[...]
    inv_n = 1.0 / HEAD_DIM
    q = proj(s_hy, s_hy + WIDTH)
    k = proj(s_hy + WIDTH, s_hy + 2 * WIDTH)
    qn = q * lax.rsqrt(_dot_split(q * q, ones) * inv_n + NORM_EPS) * qg_ref[...]
    kn = k * lax.rsqrt(_dot_split(k * k, ones) * inv_n + NORM_EPS) * kg_ref[...]
    q_o[0] = (qn * (HEAD_DIM ** -0.5)).astype(BF16)
    k_o[0] = kn.astype(BF16)
    v_o[0] = proj(s_hy + 2 * WIDTH, s_na).astype(BF16)


def branch_projections(xs, shift, scale, g, w, q_gain, k_gain, ones, splits, *, tm=512):
    G, L, D = xs.shape
    s_rw, s_hy, s_na = splits
    widths = (s_rw, s_hy - s_rw, WIDTH, WIDTH, WIDTH, w.shape[1] - s_na)
    dtypes = (F32, F32, BF16, BF16, BF16, BF16)
    full = lambda a: pl.BlockSpec(a.shape, lambda g_, i: (0,) * a.ndim, pipeline_mode=pl.Buffered(1))
    return pl.pallas_call(
        functools.partial(_proj_kernel, splits=splits),
        grid=(G, L // tm),
        in_specs=[pl.BlockSpec((1, tm, D), lambda g_, i: (g_, i, 0)),
                  pl.BlockSpec((1, 1, D), lambda g_, i: (g_, 0, 0)),
                  pl.BlockSpec((1, 1, D), lambda g_, i: (g_, 0, 0)),
                  full(g), full(w), full(q_gain), full(k_gain), full(ones)],
        out_specs=[pl.BlockSpec((1, tm, n), lambda g_, i: (g_, i, 0)) for n in widths],
        out_shape=[jax.ShapeDtypeStruct((G, L, n), dt) for n, dt in zip(widths, dtypes)],
        compiler_params=_cparams("parallel", "parallel"),
        name="branch_projections",
    )(xs, shift, scale, g, w, q_gain, k_gain, ones)


def _halo_specs(width, n_tiles):
    per = SEQ_TILE // HALO
    last = n_tiles * per - 1
    return [pl.BlockSpec((1, SEQ_TILE, width), lambda g_, i: (g_, i, 0)),
            pl.BlockSpec((1, HALO, width), lambda g_, i: (g_, jnp.maximum(i * per - 1, 0), 0)),
            pl.BlockSpec((1, HALO, width), lambda g_, i: (g_, jnp.minimum((i + 1) * per, last), 0))]


def _neighbours(u, prev_ref, next_ref):
    g_ = pl.program_id(0)
    i = pl.program_id(1)
    n_tiles = pl.num_programs(1)
    has_prev = jnp.logical_and(g_ > 0, i > 0)
    has_next = jnp.logical_and(g_ > 0, i < n_tiles - 1)
    row = lax.broadcasted_iota(jnp.int32, u.shape, 0)
    halo_p = jnp.where(has_prev, prev_ref[0, HALO - 1:HALO, :], 0.0)
    halo_n = jnp.where(has_next, next_ref[0, 0:1, :], 0.0)
    prev = jnp.where(row == 0, halo_p, pltpu.roll(u, 1, 0))
    nxt = jnp.where(row == SEQ_TILE - 1, halo_n, pltpu.roll(u, SEQ_TILE - 1, 0))
    return prev, nxt


def _rope(z, cos, sin_signed):
    lane = lax.broadcasted_iota(jnp.int32, z.shape, 1)
    partner = jnp.where(lane % 32 < 16, pltpu.roll(z, WIDTH - 16, 1), pltpu.roll(z, 16, 1))
    return z * cos + partner * sin_signed


def _rw_feat_kernel(u_ref, up_ref, un_ref, cos_ref, sin_ref, mu_ref, w0_ref, w2_ref, a0_ref, a2_ref,
                    g2_ref, kk_ref, ka_ref, ones_ref,
                    r_o, v_o, kkn_o, g_o, ld0_o, ld1_o, k0_o, k1_o, b0_o, b1_o):
    u = u_ref[0]
    prev, nxt = _neighbours(u, up_ref, un_ref)
    u = u + mu_ref[0:1, :] * (prev - u) + mu_ref[1:2, :] * (nxt - u)
    r = u[:, 0:WIDTH]
    k = u[:, WIDTH:2 * WIDTH]
    v = u[:, 2 * WIDTH:3 * WIDTH]
    o = 3 * WIDTH
    lw = u[:, o:o + 2 * RW_LORA]
    la = u[:, o + 2 * RW_LORA:o + 4 * RW_LORA]
    lg = u[:, o + 4 * RW_LORA:o + 4 * RW_LORA + RW_GATE_LORA]
    is_lat = pl.program_id(0) > 0
    cos = jnp.where(is_lat, cos_ref[...], 1.0)
    sin = jnp.where(is_lat, sin_ref[...], 0.0)
    r = _rope(r, cos, sin)
    k = _rope(k, cos, sin)
    g = _dot(_sigmoid(lg), g2_ref[...])
    kk = k * kk_ref[...]
    nrm = jnp.sqrt(_dot_split(kk * kk, ones_ref[...]))
    kk = kk / jnp.maximum(nrm, 1e-12)
    r_o[0] = r
    v_o[0] = v
    kkn_o[0] = kk
    g_o[0] = g
    tanh_lw = jnp.tanh(lw)
    for d, (ld_o, k_o, b_o) in enumerate(((ld0_o, k0_o, b0_o), (ld1_o, k1_o, b1_o))):
        zw = w0_ref[d:d + 1, :] + _dot(tanh_lw[:, d * RW_LORA:(d + 1) * RW_LORA], w2_ref[d])
        softplus = jnp.maximum(-zw, 0.0) + jnp.log(1.0 + jnp.exp(-jnp.abs(zw)))
        ld_o[0] = -jnp.exp(-softplus - 0.5)
        iclr = _sigmoid(a0_ref[d:d + 1, :] + _dot(la[:, d * RW_LORA:(d + 1) * RW_LORA], a2_ref[d]))
        k_o[0] = k * (1.0 + (iclr - 1.0) * ka_ref[...])
        b_o[0] = kk * iclr


def rwkv_features(u_rw, cos, sin, mu, w0, w2, a0, a2, g2, k_k, k_a, ones):
    G, L, W_IN = u_rw.shape
    n_tiles = L // SEQ_TILE
    full = lambda a: pl.BlockSpec(a.shape, lambda g_, i: (0,) * a.ndim)
    tab = pl.BlockSpec((SEQ_TILE, WIDTH), lambda g_, i: (i, 0))
    params = (mu, w0, w2, a0, a2, g2, k_k, k_a, ones)
    out = jax.ShapeDtypeStruct((G, L, WIDTH), F32)
    return pl.pallas_call(
        _rw_feat_kernel,
        grid=(G, n_tiles),
        in_specs=_halo_specs(W_IN, n_tiles) + [tab, tab] + [full(p) for p in params],
        out_specs=[pl.BlockSpec((1, SEQ_TILE, WIDTH), lambda g_, i: (g_, i, 0))] * 10,
        out_shape=[out] * 10,
        compiler_params=_cparams("parallel", "parallel"),
        name="rwkv_features",
    )(u_rw, u_rw, u_rw, cos, sin, *params)


def _rw_chunk_operands(r, ld, k, v, kk, b, incl, reverse):
    C = CHUNK
    ld_hi = ld.astype(BF16)
    ld_lo = (ld - ld_hi.astype(F32)).astype(BF16)
    tri = jnp.where(incl, 1.0, 0.0).astype(BF16)
    cum = (jnp.dot(tri, ld_hi, preferred_element_type=F32)
           + jnp.dot(tri, ld_lo, preferred_element_type=F32))
    tot = cum[0:1, :] if reverse else cum[C - 1:C, :]
    e_neg = jnp.exp(-cum)
    e_rem = jnp.exp(tot - cum)
    return dict(
        a_t=(-kk * jnp.exp(cum - ld)).astype(BF16),
        r_t=(r * jnp.exp(cum)).astype(BF16),
        b_t=(b * e_neg).astype(BF16),
        k_t=(k * e_neg).astype(BF16),
        b_p=(b * e_rem).astype(BF16),
        k_p=(k * e_rem).astype(BF16),
        p_c=jnp.exp(tot),
        v=v.astype(BF16))


def _rw_scan_kernel(*refs):
    in_refs, y_refs, s_ref = refs[:12], refs[12:14], refs[14]
    C = CHUNK

    @pl.when(pl.program_id(1) == 0)
    def _():
        s_ref[...] = jnp.zeros_like(s_ref)

    row = lax.broadcasted_iota(jnp.int32, (C, C), 0)
    col = lax.broadcasted_iota(jnp.int32, (C, C), 1)
    eye = row == col
    chains = []
    for d in range(2):
        incl, strict = (row <= col, row < col) if d == 1 else (row >= col, row > col)
        for order in range(SCAN_CHUNKS):
            sub = SCAN_CHUNKS - 1 - order if d == 1 else order
            rs = slice(sub * C, (sub + 1) * C)
            ops = _rw_chunk_operands(*(ref[0, rs, :] for ref in in_refs[6 * d:6 * d + 6]), incl, d == 1)
            for h in range(HEADS):
                sl = slice(h * HEAD_DIM, (h + 1) * HEAD_DIM)
                ch = {name: val[:, sl] for name, val in ops.items()}
                ch.update(d=d, h=h, sl=sl, rs=rs, order=order, incl=incl, strict=strict)
                chains.append(ch)

    each = lambda fn: [fn(ch) for ch in chains]
    both = lambda fn, xs: [fn(ch, x) for ch, x in zip(chains, xs)]
    ar = each(lambda ch: jnp.concatenate([ch["a_t"], ch["r_t"]], axis=0))
    g_b = both(lambda ch, x: _dot_nt(x, ch["b_t"]), ar)
    g_k = both(lambda ch, x: _dot_nt(x, ch["k_t"]), ar)
    a_ab = both(lambda ch, g: jnp.where(ch["strict"], g[:C], 0.0), g_b)
    a_rb = both(lambda ch, g: jnp.where(ch["incl"], g[C:], 0.0), g_b)
    a_k = both(lambda ch, g: jnp.concatenate([jnp.where(ch["strict"], g[:C], 0.0),
                                              jnp.where(ch["incl"], g[C:], 0.0)], axis=0), g_k)
    av = both(lambda ch, a: _dot(a, ch["v"]), a_k)
    t_inv = [jnp.where(eye, 1.0, a) for a in a_ab]
    pw = a_ab
    for _ in range(int(math.log2(C)) - 1):
        pw = [_dot(p, p) for p in pw]
        t_inv = [t + _dot(t, p) for t, p in zip(t_inv, pw)]
    w1 = both(lambda ch, t: _dot(t, ch["a_t"]), t_inv)
    w2 = [_dot(t, x[:C]) for t, x in zip(t_inv, av)]
    q_p = [ch["r_t"].astype(F32) + _dot(a, w) for ch, a, w in zip(chains, a_rb, w1)]
    y_p = [x[C:] + _dot(a, w) for x, a, w in zip(av, a_rb, w2)]
    m_mat = both(lambda ch, w: jnp.where(eye, ch["p_c"], 0.0) + _dot_tn(w, ch["b_p"]), w1)
    n_mat = both(lambda ch, w: _dot_tn(w, ch["b_p"]) + _dot_tn(ch["v"], ch["k_p"]), w2)
    state = {(d, h): s_ref[d, h] for d in range(2) for h in range(HEADS)}
    for order in range(SCAN_CHUNKS):
        now = [i for i, ch in enumerate(chains) if ch["order"] == order]
        s0 = [state[chains[i]["d"], chains[i]["h"]] for i in now]
        y = [_dot_nt(q_p[i], s) + y_p[i] for i, s in zip(now, s0)]
        s1 = [_dot(s, m_mat[i]) + n_mat[i] for i, s in zip(now, s0)]
        for i, y_c, s_c in zip(now, y, s1):
            ch = chains[i]
            y_refs[ch["d"]][0, ch["rs"], ch["sl"]] = y_c
            state[ch["d"], ch["h"]] = s_c
    for (d, h), s_c in state.items():
        s_ref[d, h] = s_c


def rwkv_scan(r, v, kk, ld0, k0, b0, ld1, k1, b1, *, batch):
    G, L, _ = r.shape
    step = SCAN_CHUNKS * CHUNK
    lc = L // batch
    ns_ctx = lc // step
    ns_lat = L // step
    n_steps = ns_ctx + ns_lat

    def idx(reverse, bi, s):
        in_ctx = s < ns_ctx
        c_ctx = ns_ctx - 1 - s if reverse else s
        c_lat = n_steps - 1 - s if reverse else s - ns_ctx
        return (jnp.where(in_ctx, 0, bi + 1), jnp.where(in_ctx, bi * ns_ctx + c_ctx, c_lat), 0)

    fwd = pl.BlockSpec((1, step, WIDTH), functools.partial(idx, False))
    rev = pl.BlockSpec((1, step, WIDTH), functools.partial(idx, True))
    out = jax.ShapeDtypeStruct((G, L, WIDTH), F32)
    return pl.pallas_call(
        _rw_scan_kernel,
        grid=(batch, n_steps),
        in_specs=[fwd] * 6 + [rev] * 6,
        out_specs=[fwd, rev],
        out_shape=[out, out],
        scratch_shapes=[pltpu.VMEM((2, HEADS, HEAD_DIM, HEAD_DIM), F32)],
        compiler_params=_cparams("parallel", "arbitrary"),
        name="rwkv_scan",
    )(r, ld0, k0, v, kk, b0, r, ld1, k1, v, kk, b1)


def _rw_readout_kernel(y0_ref, y1_ref, r_ref, k0_ref, k1_ref, v_ref, g_ref, rk_ref, gg_ref, gb_ref,
                       ones_ref, o_ref):
    y = y0_ref[0] + y1_ref[0]
    ones = ones_ref[...]
    inv_n = 1.0 / HEAD_DIM
    mu = _dot_split(y, ones) * inv_n
    yc = y - mu
    var = _dot_split(yc * yc, ones) * inv_n
    yn = yc * lax.rsqrt(var + RW_GN_EPS) * gg_ref[...] + gb_ref[...]
    bonus = _dot_split(r_ref[0] * (k0_ref[0] + k1_ref[0]) * rk_ref[...], ones) * v_ref[0]
    o_ref[0] = (yn + bonus) * g_ref[0]


def rwkv_readout(y0, y1, r, k0, k1, v, g, r_k, gn_g, gn_b, ones, *, tm=512):
    G, L, _ = r.shape
    spec = pl.BlockSpec((1, tm, WIDTH), lambda g_, i: (g_, i, 0))
    full = lambda a: pl.BlockSpec(a.shape, lambda g_, i: (0,) * a.ndim)
    params = (r_k, gn_g, gn_b, ones)
    return pl.pallas_call(
        _rw_readout_kernel,
        grid=(G, L // tm),
        in_specs=[spec] * 7 + [full(p) for p in params],
        out_specs=spec,
        out_shape=jax.ShapeDtypeStruct((G, L, WIDTH), F32),
        compiler_params=_cparams("parallel", "parallel"),
        name="rwkv_readout",
    )(y0, y1, r, k0, k1, v, g, *params)


def _hy_short_kernel(u_ref, up_ref, un_ref, w_ref, b_ref, o_ref):
    u = u_ref[0]
    prev, nxt = _neighbours(u, up_ref, un_ref)
    o_ref[0] = b_ref[...] + w_ref[0:1, :] * prev + w_ref[1:2, :] * u + w_ref[2:3, :] * nxt


def hyena_short_conv(u_hy, w, b):
    G, L, Wd = u_hy.shape
    n_tiles = L // SEQ_TILE
    full = lambda a: pl.BlockSpec(a.shape, lambda g_, i: (0,) * a.ndim)
    return pl.pallas_call(
        _hy_short_kernel,
        grid=(G, n_tiles),
        in_specs=_halo_specs(Wd, n_tiles) + [full(w), full(b)],
        out_specs=pl.BlockSpec((1, SEQ_TILE, Wd), lambda g_, i: (g_, i, 0)),
        out_shape=jax.ShapeDtypeStruct((G, L, Wd), F32),
        compiler_params=_cparams("parallel", "parallel"),
        name="hyena_short_conv",
    )(u_hy, u_hy, u_hy, w, b)


@functools.lru_cache(maxsize=None)
def _dft_tables(n):
    nn = 2 * n
    nf = n + 256
    kf = np.arange(nf, dtype=np.int64)
    t = np.arange(n, dtype=np.int64)
    ang = 2.0 * np.pi * ((kf[:, None] * t[None, :]) % nn).astype(np.float64) / nn
    valid = (kf <= n)[:, None]
    cosm = np.where(valid, np.cos(ang), 0.0)
    sinm = np.where(valid, np.sin(ang), 0.0)
    ck = np.where((kf == 0) | (kf == n), 1.0, 2.0)[:, None] / nn
    fwd = np.stack([cosm, -sinm])
    inv = np.stack([(ck * cosm).T, (-ck * sinm).T])
    return fwd, inv, nf


def _hy_conv_kernel(*refs):
    nb = HY_SEQS
    z_refs, x_refs = refs[:nb], refs[nb:2 * nb]
    fwd_ref, inv_ref, hre_ref, him_ref, skip_ref, o_ref, acc_ref = refs[2 * nb:]
    f = pl.program_id(1)

    @pl.when(f == 0)
    def _():
        acc_ref[...] = jnp.zeros_like(acc_ref)

    zb = jnp.concatenate([z_ref[0].astype(BF16) for z_ref in z_refs], axis=1)
    zre = jnp.dot(fwd_ref[0], zb, preferred_element_type=F32)
    zim = jnp.dot(fwd_ref[1], zb, preferred_element_type=F32)
    hre = jnp.concatenate([hre_ref[0]] * nb, axis=1)
    him = jnp.concatenate([him_ref[0]] * nb, axis=1)
    yre = zre * hre - zim * him
    yim = zre * him + zim * hre
    acc_ref[...] += (jnp.dot(inv_ref[0], yre.astype(BF16), preferred_element_type=F32)
                     + jnp.dot(inv_ref[1], yim.astype(BF16), preferred_element_type=F32))

    @pl.when(f == pl.num_programs(1) - 1)
    def _():
        for i in range(nb):
            conv = acc_ref[:, i * HY_WIDTH:(i + 1) * HY_WIDTH]
            o_ref[i] = x_refs[i][0] * (conv + skip_ref[0] * z_refs[i][0])


def hyena_long_conv(z_arr, z_blk, x_arr, x_blk, hre, him, skip, order, fwd, inv, *, n_seq, n, tf):
    nf = fwd.shape[1]
    nb = HY_SEQS
    zo, zc = z_blk
    xo, xc = x_blk
    seq = lambda off, cb, i: pl.BlockSpec((1, n, HY_WIDTH), lambda s, f: (s * nb + i + off, 0, cb))
    return pl.pallas_call(
        _hy_conv_kernel,
        grid=(n_seq // nb, nf // tf),
        in_specs=[seq(zo, zc, i) for i in range(nb)] + [seq(xo, xc, i) for i in range(nb)]
                 + [pl.BlockSpec((2, tf, n), lambda s, f: (0, f, 0)),
                    pl.BlockSpec((2, n, tf), lambda s, f: (0, 0, f)),
                    pl.BlockSpec((1, tf, HY_WIDTH), lambda s, f: (order, f, 0)),
                    pl.BlockSpec((1, tf, HY_WIDTH), lambda s, f: (order, f, 0)),
                    pl.BlockSpec((1, 1, HY_WIDTH), lambda s, f: (order, 0, 0))],
        out_specs=pl.BlockSpec((nb, n, HY_WIDTH), lambda s, f: (s, 0, 0)),
        out_shape=jax.ShapeDtypeStruct((n_seq, n, HY_WIDTH), F32),
        scratch_shapes=[pltpu.VMEM((n, nb * HY_WIDTH), F32)],
        compiler_params=_cparams("parallel", "arbitrary"),
        name=f"hyena_long_conv_n{n}_o{order}",
    )(*([z_arr] * nb), *([x_arr] * nb), fwd, inv, hre, him, skip)


def hyena_filter_spectrum(n, w1, b1, w2, b2, w3, freq, fwd):
    hp = lax.Precision.HIGHEST
    pos = jnp.arange(n, dtype=F32)
    t = pos / max(n - 1, 1)
    bands = (HY_EMB_DIM - 1) // 2
    fr = jnp.linspace(1e-4, bands - 1, bands, dtype=F32)
    ang = (2 * math.pi / n) * pos[:, None] * fr[None, :]
    z = jnp.concatenate([t[:, None], jnp.cos(ang), -jnp.sin(ang)], axis=-1)
    h = jnp.sin(freq * (jnp.dot(z, w1, precision=hp) + b1))
    h = jnp.sin(freq * (jnp.dot(h, w2, precision=hp) + b2))
    h = matmul(h, w3, tn=w3.shape[1], split=True, name="hyena_filter_out")
    h = h.reshape(n, HY_ORDER, 2, HY_WIDTH)
    max_decay = math.log(HY_DECAY_TARGET) / HY_FAST_DECAY
    min_decay = math.log(HY_DECAY_TARGET) / HY_SLOW_DECAY
    deltas = jnp.abs(jnp.linspace(min_decay, max_decay, HY_WIDTH, dtype=F32))
    h = h * jnp.exp(-t[:, None] * deltas)[:, None, None, :]
    h_fwd = h[:, :, 0]
    h_bwd = h[:, :, 1] * (pos > 0).astype(F32)[:, None, None]
    l1 = jnp.sum(jnp.abs(h_fwd), axis=0) + jnp.sum(jnp.abs(h_bwd), axis=0)
    even = ((h_fwd + h_bwd) / l1).reshape(n, HY_ORDER * HY_WIDTH)
    odd = ((h_fwd - h_bwd) / l1).reshape(n, HY_ORDER * HY_WIDTH)
    nf = fwd.shape[1]
    hre = matmul(fwd[0], even, tn=HY_ORDER * HY_WIDTH, name="hyena_filter_dft_re")
    him = matmul(fwd[1], odd, tn=HY_ORDER * HY_WIDTH, name="hyena_filter_dft_im")
    to_ofc = lambda a: jnp.moveaxis(a.reshape(nf, HY_ORDER, HY_WIDTH), 1, 0)
    return to_ofc(hre), to_ofc(him)


def _na_lat_kernel(q_ref, k_ref, v_ref, kc_ref, vc_ref, bias_ref, o_ref, *, rows, wr):
    kc = kc_ref[0]
    vc = vc_ref[0]
    jobs = []
    for rr in range(NA_ROWS):
        i = pl.program_id(1) * NA_ROWS + rr
        r0 = jnp.clip(i - wr // 2, 0, rows - wr)
        off = r0 - i + (NA_WIN_ROWS - 1)
        start = pl.multiple_of(r0 * GRID_W, GRID_W)
        qs = slice(rr * GRID_W, (rr + 1) * GRID_W)
        q = q_ref[0, qs, :]
        kw = k_ref[0, pl.ds(start, wr * GRID_W), :]
        vw = v_ref[0, pl.ds(start, wr * GRID_W), :]
        for h in range(HEADS):
            sl = slice(h * HEAD_DIM, (h + 1) * HEAD_DIM)
            jobs.append(dict(q=q[:, sl], kw=kw[:, sl], vw=vw[:, sl], h=h, off=off, qs=qs, sl=sl))
    s_loc = [_dot_nt(j["q"], j["kw"]) + bias_ref[j["h"], j["off"]] for j in jobs]
    s_ctx = [_dot_nt(j["q"], kc[:, j["sl"]]) for j in jobs]
    m = [jnp.maximum(jnp.max(a, axis=-1, keepdims=True), jnp.max(b, axis=-1, keepdims=True))
         for a, b in zip(s_loc, s_ctx)]
    p_loc = [jnp.exp(a - mm) for a, mm in zip(s_loc, m)]
    p_ctx = [jnp.exp(b - mm) for b, mm in zip(s_ctx, m)]
    den = [jnp.sum(a, axis=-1, keepdims=True) + jnp.sum(b, axis=-1, keepdims=True) for a, b in zip(p_loc, p_ctx)]
    o = [_dot(a, j["vw"]) + _dot(b, vc[:, j["sl"]]) for a, b, j in zip(p_loc, p_ctx, jobs)]
    for j, oo, dd in zip(jobs, o, den):
        o_ref[0, j["qs"], j["sl"]] = oo / dd


def _na_bias_table(rpb, rows, wr):
    hp = lax.Precision.HIGHEST
    offs = np.arange(NA_WIN_ROWS) - (NA_WIN_ROWS - 1)
    dr = offs[:, None] + np.arange(wr)[None, :] + NA_WIN_ROWS - 1
    c_ar = np.arange(GRID_W)
    dc = np.clip(c_ar[None, :] - c_ar[:, None] + NA_WIN_COLS - 1, 0, 2 * NA_WIN_COLS - 2)
    c0 = np.clip(c_ar - NA_WIN_COLS // 2, 0, GRID_W - NA_WIN_COLS)
    in_win = (c_ar[None, :] >= c0[:, None]) & (c_ar[None, :] < c0[:, None] + NA_WIN_COLS)
    oh_r = jnp.asarray(dr[..., None] == np.arange(2 * NA_WIN_ROWS - 1), dtype=F32)
    oh_c = jnp.asarray(dc[..., None] == np.arange(2 * NA_WIN_COLS - 1), dtype=F32)
    by_row = jnp.einsum("hab,ora->horb", rpb.astype(F32), oh_r, precision=hp)
    bias = jnp.einsum("horb,qkb->hoqrk", by_row, oh_c, precision=hp)
    bias = jnp.where(in_win[None, None, :, None, :], bias, NEG_BIG)
    return bias.reshape(HEADS, NA_WIN_ROWS, GRID_W, wr * GRID_W)


def na_latent(qn, kn, vb, bias, *, batch):
    G, L, _ = qn.shape
    lc = L // batch
    rows = L // GRID_W
    wr = min(NA_WIN_ROWS, rows)
    assert rows % NA_ROWS == 0
    qspec = pl.BlockSpec((1, NA_ROWS * GRID_W, WIDTH), lambda b_, i: (b_ + 1, i, 0))
    seq = pl.BlockSpec((1, L, WIDTH), lambda b_, i: (b_ + 1, 0, 0))
    ctx = pl.BlockSpec((1, lc, WIDTH), lambda b_, i: (0, b_, 0))
    return pl.pallas_call(
        functools.partial(_na_lat_kernel, rows=rows, wr=wr),
        grid=(batch, rows // NA_ROWS),
        in_specs=[qspec, seq, seq, ctx, ctx, pl.BlockSpec(bias.shape, lambda b_, i: (0, 0, 0, 0))],
        out_specs=pl.BlockSpec((1, NA_ROWS * GRID_W, WIDTH), lambda b_, i: (b_, i, 0)),
        out_shape=jax.ShapeDtypeStruct((batch, L, WIDTH), F32),
        compiler_params=_cparams("parallel", "arbitrary"),
        name="na_latent",
    )(qn, kn, vb, kn, vb, bias)


def _na_ctx_kernel(q_ref, k_ref, v_ref, o_ref):
    q = q_ref[0]
    k = k_ref[0]
    v = v_ref[0]
    for h in range(HEADS):
        sl = slice(h * HEAD_DIM, (h + 1) * HEAD_DIM)
        s = _dot_nt(q[:, sl], k[:, sl])
        p = jnp.exp(s - jnp.max(s, axis=-1, keepdims=True))
        o_ref[0, :, sl] = _dot(p, v[:, sl]) / jnp.sum(p, axis=-1, keepdims=True)


def na_context(qn, kn, vb, *, batch):
    G, L, _ = qn.shape
    lc = L // batch
    spec = pl.BlockSpec((1, lc, WIDTH), lambda b_: (0, b_, 0))
    return pl.pallas_call(
        _na_ctx_kernel,
        grid=(batch,),
        in_specs=[spec] * 3,
        out_specs=spec,
        out_shape=jax.ShapeDtypeStruct((1, L, WIDTH), F32),
        compiler_params=_cparams("parallel"),
        name="na_context",
    )(qn, kn, vb)


def _merge_kernel(x_ref, gl_ref, rw_ref, hy_ref, na_ref, g1_ref, wr_ref, wh_ref, wn_ref, wo_ref, o_ref):
    D = x_ref.shape[-1]
    gl = gl_ref[0].astype(F32)
    m = (_sigmoid(gl[:, 0:D]) * _dot(rw_ref[0], wr_ref[...])
         + _sigmoid(gl[:, D:2 * D]) * _dot(hy_ref[0], wh_ref[...])
         + _sigmoid(gl[:, 2 * D:3 * D]) * _dot(na_ref[0], wn_ref[...]))
    o_ref[0] = x_ref[0] + g1_ref[0] * _dot(m, wo_ref[...])


def merge_branches(xs, u_gate, y_rw, y_hy, y_na, gate1, w_rw, w_hy, w_na, w_o, *, g_off, tm=512):
    G, L, D = xs.shape
    tok = lambda width: pl.BlockSpec((1, tm, width), lambda g_, i: (g_ + g_off, i, 0))
    own = lambda width: pl.BlockSpec((1, tm, width), lambda g_, i: (g_, i, 0))
    full = lambda a: pl.BlockSpec(a.shape, lambda g_, i: (0,) * a.ndim, pipeline_mode=pl.Buffered(1))
    weights = (w_rw, w_hy, w_na, w_o)
    return pl.pallas_call(
        _merge_kernel,
        grid=(G - g_off, L // tm),
        in_specs=[tok(D), tok(3 * D), tok(WIDTH), own(HY_WIDTH), own(WIDTH),
                  pl.BlockSpec((1, 1, D), lambda g_, i: (g_ + g_off, 0, 0))] + [full(w) for w in weights],
        out_specs=pl.BlockSpec((1, tm, D), lambda g_, i: (g_, i, 0)),
        out_shape=jax.ShapeDtypeStruct((G - g_off, L, D), F32),
        compiler_params=_cparams("parallel", "parallel"),
        name="merge_branches",
    )(xs, u_gate, y_rw, y_hy, y_na, gate1, *weights)


def _ffn_kernel(x_ref, sh_ref, sc_ref, g_ref, g2_ref, w1_ref, w3_ref, w2_ref, o_ref, h_ref, acc_ref):
    f = pl.program_id(2)

    @pl.when(f == 0)
    def _():
        h_ref[...] = _norm_mod(x_ref[0], g_ref[...], sh_ref[0], sc_ref[0]).astype(BF16)
        acc_ref[...] = jnp.zeros_like(acc_ref)

    h = h_ref[...]
    a = jnp.dot(h, w1_ref[...], preferred_element_type=F32)
    b = jnp.dot(h, w3_ref[...], preferred_element_type=F32)
    acc_ref[...] += _dot(_silu(a) * b, w2_ref[...])

    @pl.when(f == pl.num_programs(2) - 1)
    def _():
        o_ref[0] = x_ref[0] + g2_ref[0] * acc_ref[...]


def ffn_dense(xs, shift, scale, g, gate2, w1, w3, w2, *, tm=512, tf):
    G, L, D = xs.shape
    FF = w1.shape[1]
    mod = pl.BlockSpec((1, 1, D), lambda g_, i, f: (g_, 0, 0))
    return pl.pallas_call(
        _ffn_kernel,
        grid=(G, L // tm, FF // tf),
        in_specs=[pl.BlockSpec((1, tm, D), lambda g_, i, f: (g_, i, 0)), mod, mod,
                  pl.BlockSpec((1, D), lambda g_, i, f: (0, 0)), mod,
                  pl.BlockSpec((D, tf), lambda g_, i, f: (0, f)),
                  pl.BlockSpec((D, tf), lambda g_, i, f: (0, f)),
                  pl.BlockSpec((tf, D), lambda g_, i, f: (f, 0))],
        out_specs=pl.BlockSpec((1, tm, D), lambda g_, i, f: (g_, i, 0)),
        out_shape=jax.ShapeDtypeStruct((G, L, D), F32),
        scratch_shapes=[pltpu.VMEM((tm, D), BF16), pltpu.VMEM((tm, D), F32)],
        compiler_params=_cparams("parallel", "parallel", "arbitrary"),
        name="ffn_dense",
    )(xs, shift, scale, g, gate2, w1, w3, w2)


def _moe_route_kernel(x_ref, sh_ref, sc_ref, g_ref, rt_ref, h_o, gates_o, rank_o, rank_t_o, cnt_o,
                      cnt_row, cnt_col):
    i = pl.program_id(1)

    @pl.when(i == 0)
    def _():
        cnt_row[...] = jnp.zeros_like(cnt_row)
        cnt_col[...] = jnp.zeros_like(cnt_col)

    h = _norm_mod(x_ref[0], g_ref[...], sh_ref[0], sc_ref[0])
    h_o[0] = h.astype(BF16)
    rt = rt_ref[...]
    hh = h.astype(BF16)
    hl = (h - hh.astype(F32)).astype(BF16)
    rh = rt.astype(BF16)
    rl = (rt - rh.astype(F32)).astype(BF16)
    logits = (jnp.dot(hh, rh, preferred_element_type=F32) + jnp.dot(hl, rh, preferred_element_type=F32)
              + jnp.dot(hh, rl, preferred_element_type=F32))
    lane = lax.broadcasted_iota(jnp.int32, logits.shape, 1)
    logits = jnp.where(lane < N_EXPERTS, logits, NEG_BIG)
    m1 = jnp.max(logits, axis=-1, keepdims=True)
    i1 = jnp.min(jnp.where(logits == m1, lane, LANES), axis=-1, keepdims=True)
    rest = jnp.where(lane == i1, NEG_BIG, logits)
    m2 = jnp.max(rest, axis=-1, keepdims=True)
    i2 = jnp.min(jnp.where(rest == m2, lane, LANES), axis=-1, keepdims=True)
    e2 = jnp.exp(m2 - m1)
    gates = jnp.where(lane == i1, 1.0 / (1.0 + e2), 0.0) + jnp.where(lane == i2, e2 / (1.0 + e2), 0.0)
    gates_o[0] = gates

    tm = gates.shape[0]
    sel = gates > 0.0
    m_tok = jnp.where(sel, 1.0, 0.0)
    m_exp = m_tok.T[:2 * HALO]
    row = lax.broadcasted_iota(jnp.int32, (tm, tm), 0)
    col = lax.broadcasted_iota(jnp.int32, (tm, tm), 1)
    before = jnp.where(col < row, 1.0, 0.0).astype(BF16)
    after = jnp.where(row < col, 1.0, 0.0).astype(BF16)
    rank = jnp.dot(before, m_tok.astype(BF16), preferred_element_type=F32) + cnt_row[...]
    rank_t = jnp.dot(m_exp.astype(BF16), after, preferred_element_type=F32)[:HALO]
    rank_t = rank_t + jnp.tile(cnt_col[...], (1, tm // LANES))
    rank_o[0] = jnp.where(sel, rank, NEG_BIG)
    rank_t_o[0] = jnp.where(m_exp[:HALO] > 0.0, rank_t, NEG_BIG)
    cnt_row[...] += jnp.sum(m_tok, axis=0, keepdims=True)
    cnt_col[...] += jnp.sum(m_exp[:HALO], axis=1, keepdims=True)
    cnt_o[0] = cnt_row[...]


def _moe_expert_kernel(cnt_ref, h_ref, gates_ref, rank_ref, rank_t_ref, x_ref, g2_ref, w1_ref, w3_ref, w2_ref,
                       o_ref, hs_ref, ys_ref, *, rb):
    gi = pl.program_id(0)
    e = pl.program_id(1)
    f = pl.program_id(2)
    n_e = cnt_ref[gi * pl.num_programs(1) + e]
    n_half = (n_e + rb // 2 - 1) // (rb // 2)
    n_full = n_half // 2
    tm = h_ref.shape[1]

    def for_blocks(fn):
        def body(j, carry):
            fn(pl.multiple_of(j * rb, rb), rb)
            return carry

        lax.fori_loop(0, n_full, body, 0)

        @pl.when(n_half % 2 == 1)
        def _():
            fn(pl.multiple_of(n_full * rb, rb), rb // 2)

    @pl.when(jnp.logical_and(e == 0, f == 0))
    def _():
        o_ref[...] = jnp.zeros_like(o_ref)

    @pl.when(f == 0)
    def _():
        rank_t = rank_t_ref[0, pl.ds(e, 1), :]

        def gather(start, size):
            sub = lax.broadcasted_iota(jnp.int32, (size, tm), 0).astype(F32) + start.astype(F32)
            pick = jnp.where(rank_t == sub, 1.0, 0.0).astype(BF16)
            hs_ref[pl.ds(start, size), :] = jnp.dot(pick, h_ref[0], preferred_element_type=F32).astype(BF16)

        for_blocks(gather)

    def expert_block(start, size):
        hs = hs_ref[pl.ds(start, size), :]
        a = jnp.dot(hs, w1_ref[0], preferred_element_type=F32)
        b = jnp.dot(hs, w3_ref[0], preferred_element_type=F32)
        return _dot(_silu(a) * b, w2_ref[0])

    @pl.when(f == 0)
    def _():
        def first(start, size):
            ys_ref[pl.ds(start, size), :] = expert_block(start, size)

        for_blocks(first)

    @pl.when(f > 0)
    def _():
        def more(start, size):
            ys_ref[pl.ds(start, size), :] += expert_block(start, size)

        for_blocks(more)

    last_f = f == pl.num_programs(2) - 1

    @pl.when(last_f)
    def _():
        lane = lax.broadcasted_iota(jnp.int32, (tm, LANES), 1)
        gate = jnp.sum(jnp.where(lane == e, gates_ref[0], 0.0), axis=-1, keepdims=True)
        rank = jnp.sum(jnp.where(lane == e, rank_ref[0], 0.0), axis=-1, keepdims=True)

        def scatter(start, size):
            slot = lax.broadcasted_iota(jnp.int32, (tm, size), 1).astype(F32) + start.astype(F32)
            put = jnp.where(rank == slot, gate, 0.0)
            o_ref[0] += _dot(put, ys_ref[pl.ds(start, size), :])

        for_blocks(scatter)

    @pl.when(jnp.logical_and(e == pl.num_programs(1) - 1, last_f))
    def _():
        o_ref[0] = x_ref[0] + g2_ref[0] * o_ref[0]


def moe_ffn(xs, shift, scale, g, gate2, router, w1, w3, w2, *, g_off, tm_route=512, rb=256, tf):
    G, L, D = xs.shape
    E, _, FF = w1.shape
    n_g = G - g_off
    mod = pl.BlockSpec((1, 1, D), lambda g_, i: (g_ + g_off, 0, 0))
    tok = lambda width: pl.BlockSpec((1, tm_route, width), lambda g_, i: (g_, i, 0))
    h, gates, rank, rank_t, cnt = pl.pallas_call(
        _moe_route_kernel,
        grid=(n_g, L // tm_route),
        in_specs=[pl.BlockSpec((1, tm_route, D), lambda g_, i: (g_ + g_off, i, 0)), mod, mod,
                  pl.BlockSpec((1, D), lambda g_, i: (0, 0)),
                  pl.BlockSpec((D, LANES), lambda g_, i: (0, 0))],
        out_specs=[tok(D), tok(LANES), tok(LANES),
                   pl.BlockSpec((1, HALO, tm_route), lambda g_, i: (g_, 0, i)),
                   pl.BlockSpec((1, 1, LANES), lambda g_, i: (g_, 0, 0))],
        out_shape=[jax.ShapeDtypeStruct((n_g, L, D), BF16),
                   jax.ShapeDtypeStruct((n_g, L, LANES), F32),
                   jax.ShapeDtypeStruct((n_g, L, LANES), F32),
                   jax.ShapeDtypeStruct((n_g, HALO, L), F32),
                   jax.ShapeDtypeStruct((n_g, 1, LANES), F32)],
        scratch_shapes=[pltpu.VMEM((1, LANES), F32), pltpu.VMEM((HALO, LANES), F32)],
        compiler_params=_cparams("parallel", "arbitrary"),
        name="moe_route",
    )(xs, shift, scale, g, router)
    counts = cnt[:, 0, :E].astype(jnp.int32).reshape(-1)

    once = pl.Buffered(1)
    grp = lambda width: pl.BlockSpec((1, L, width), lambda g_, e, f, c: (g_, 0, 0), pipeline_mode=once)
    return pl.pallas_call(
        functools.partial(_moe_expert_kernel, rb=rb),
        grid_spec=pltpu.PrefetchScalarGridSpec(
            num_scalar_prefetch=1,
            grid=(n_g, E, FF // tf),
            in_specs=[grp(D), grp(LANES), grp(LANES),
                      pl.BlockSpec((1, HALO, L), lambda g_, e, f, c: (g_, 0, 0), pipeline_mode=once),
                      pl.BlockSpec((1, L, D), lambda g_, e, f, c: (g_ + g_off, 0, 0), pipeline_mode=once),
                      pl.BlockSpec((1, 1, D), lambda g_, e, f, c: (g_ + g_off, 0, 0)),
                      pl.BlockSpec((1, D, tf), lambda g_, e, f, c: (e, 0, f)),
                      pl.BlockSpec((1, D, tf), lambda g_, e, f, c: (e, 0, f)),
                      pl.BlockSpec((1, tf, D), lambda g_, e, f, c: (e, f, 0))],
            out_specs=pl.BlockSpec((1, L, D), lambda g_, e, f, c: (g_, 0, 0), pipeline_mode=once),
            scratch_shapes=[pltpu.VMEM((L + rb, D), BF16), pltpu.VMEM((L + rb, D), F32)]),
        out_shape=jax.ShapeDtypeStruct((n_g, L, D), F32),
        compiler_params=pltpu.CompilerParams(dimension_semantics=("parallel", "arbitrary", "arbitrary"),
                                             vmem_limit_bytes=60 * 1024 * 1024),
        name="moe_experts",
    )(counts, h, gates, rank, rank_t, xs, gate2, w1, w3, w2)


def _rope_tables(n):
    half = HEAD_DIM // 2
    nf = half // 2
    t = np.arange(n)
    inv = ROPE_BASE ** (-np.arange(nf, dtype=np.float64) / nf)
    d = np.arange(HEAD_DIM)
    pos = np.where(d[None, :] < half, (t // GRID_W)[:, None], (t % GRID_W)[:, None]).astype(np.float64)
    ang = pos * inv[d % nf][None, :]
    sign = np.where(d % half < nf, -1.0, 1.0)
    cos = np.tile(np.cos(ang), (1, HEADS))
    sin = np.tile(np.sin(ang) * sign[None, :], (1, HEADS))
    return jnp.asarray(cos, dtype=F32), jnp.asarray(sin, dtype=F32)


def kernel(x, c, ctx, c_ctx, ada_w, ada_b, norm1_g, norm2_g, w_in, rw_shift, rw_w0, rw_w2, rw_a0, rw_a2, rw_g2, rw_kk, rw_ka, rw_rk, rw_gn_g, rw_gn_b, hy_conv_w, hy_conv_b, hy_f_w1, hy_f_b1, hy_f_w2, hy_f_b2, hy_f_w3, hy_f_freq, hy_skip, na_q_gain, na_k_gain, na_rpb, w_br_rw, w_br_hy, w_br_na, w_out, ff_w1, ff_w3, ff_w2, moe_router, moe_w1, moe_w3, moe_w2):
    B, L, D = x.shape
    lc = ctx.shape[1]
    depth = ada_w.shape[0]
    assert B * lc == L and lc == SEQ_TILE and L % GRID_W == 0
    G = B + 1
    rows = L // GRID_W
    wr = min(NA_WIN_ROWS, rows)
    rw_in = 3 * WIDTH + 4 * RW_LORA + RW_GATE_LORA
    hy_in = (HY_ORDER + 1) * HY_WIDTH
    na_in = 3 * WIDTH
    splits = (rw_in, rw_in + hy_in, rw_in + hy_in + na_in)

    xs = jnp.concatenate([ctx.reshape(1, L, D), x], axis=0)
    act = jnp.concatenate([c_ctx[None, :], c], axis=0)
    act = act * _sigmoid(act)
    act = jnp.pad(act, ((0, (-G) % 16), (0, 0)))
    ones = _head_ones()
    cos, sin = _rope_tables(L)
    fwd_lat, inv_lat, _ = _dft_tables(L)
    fwd_ctx, inv_ctx, _ = _dft_tables(lc)
    fwd_lat, inv_lat, fwd_ctx, inv_ctx = (jnp.asarray(a, dtype=F32).astype(BF16)
                                          for a in (fwd_lat, inv_lat, fwd_ctx, inv_ctx))
    row2 = lambda a: a.reshape(1, -1)

    for li in range(depth):
        last = li == depth - 1
        mods = matmul(act, ada_w[li], tn=1024, split=True, name="adaln")[:G] + ada_b[li]
        sh1, sc1, g1, sh2, sc2, g2 = (m.reshape(G, 1, D) for m in jnp.split(mods, 6, axis=-1))

        u_rw, u_hy, qn, kn, vb, u_gate = branch_projections(
            xs, sh1, sc1, row2(norm1_g[li]), w_in[li].astype(BF16), row2(jnp.tile(na_q_gain[li], HEADS)),
            row2(jnp.tile(na_k_gain[li], HEADS)), ones, splits)

        r, v, kk, g, ld0, ld1, k0, k1, b0, b1 = rwkv_features(
            u_rw, cos, sin, rw_shift[li], rw_w0[li], rw_w2[li].astype(BF16), rw_a0[li], rw_a2[li].astype(BF16),
            rw_g2[li].astype(BF16), row2(rw_kk[li]), row2(rw_ka[li]), ones)
        y0, y1 = rwkv_scan(r, v, kk, ld0, k0, b0, ld1, k1, b1, batch=B)
        y_rw = rwkv_readout(y0, y1, r, k0, k1, v, g, row2(rw_rk[li]), row2(rw_gn_g[li]), row2(rw_gn_b[li]), ones)

        hy = hyena_short_conv(u_hy, hy_conv_w[li], row2(hy_conv_b[li]))
        f_args = (hy_f_w1[li], hy_f_b1[li], hy_f_w2[li], hy_f_b2[li], hy_f_w3[li], hy_f_freq[li])
        skip = hy_skip[li].reshape(HY_ORDER, 1, HY_WIDTH)
        hre, him = hyena_filter_spectrum(L, *f_args, fwd_lat)
        conv = functools.partial(hyena_long_conv, skip=skip, fwd=fwd_lat, inv=inv_lat, n_seq=B, n=L, tf=384)
        z1 = conv(hy, (1, 0), hy, (1, 1), hre, him, order=0)
        y_hy = conv(z1, (0, 0), hy, (1, 2), hre, him, order=1)
        if not last:
            hy_c = hy[0].reshape(B, lc, hy_in)
            hre_c, him_c = hyena_filter_spectrum(lc, *f_args, fwd_ctx)
            conv_c = functools.partial(hyena_long_conv, skip=skip, fwd=fwd_ctx, inv=inv_ctx, n_seq=B, n=lc,
                                       tf=fwd_ctx.shape[1])
            z1c = conv_c(hy_c, (0, 0), hy_c, (0, 1), hre_c, him_c, order=0)
            y_hy_c = conv_c(z1c, (0, 0), hy_c, (0, 2), hre_c, him_c, order=1)
            y_hy = jnp.concatenate([y_hy_c.reshape(1, L, HY_WIDTH), y_hy], axis=0)

        bias = _na_bias_table(na_rpb[li], rows, wr)
        y_na = na_latent(qn, kn, vb, bias, batch=B)
        if not last:
            y_na = jnp.concatenate([na_context(qn, kn, vb, batch=B), y_na], axis=0)

        g_off = 1 if last else 0
        xs = merge_branches(xs, u_gate, y_rw, y_hy, y_na, g1, w_br_rw[li].astype(BF16), w_br_hy[li].astype(BF16),
                            w_br_na[li].astype(BF16), w_out[li].astype(BF16), g_off=g_off)
        if last:
            sh2, sc2, g2 = sh2[1:], sc2[1:], g2[1:]

        n2 = row2(norm2_g[li])
        if li % 2 == 0:
            j = li // 2
            xs = ffn_dense(xs, sh2, sc2, n2, g2, ff_w1[j].astype(BF16), ff_w3[j].astype(BF16), ff_w2[j].astype(BF16),
                           tf=ff_w1.shape[2] // 2)
        else:
            j = li // 2
            router = jnp.pad(moe_router[j], ((0, 0), (0, LANES - N_EXPERTS)))
            xs = moe_ffn(xs, sh2, sc2, n2, g2, router, moe_w1[j].astype(BF16), moe_w3[j].astype(BF16),
                         moe_w2[j].astype(BF16), g_off=0, tf=moe_w1.shape[3] // 4)
    return xs if depth == 0 else xs[-B:]
```

```python
import functools
import math

import numpy as np
import jax
import jax.numpy as jnp
from jax import lax
from jax.experimental import pallas as pl
from jax.experimental.pallas import tpu as pltpu

F32 = jnp.float32
BF16 = jnp.bfloat16

NORM_EPS = 1e-6
ROPE_BASE = 10000.0
GRID_W = 64
HEADS = 6
HEAD_DIM = 64
WIDTH = HEADS * HEAD_DIM
RW_LORA = 64
RW_GATE_LORA = 128
RW_GN_EPS = 64e-5
HY_WIDTH = 256
HY_ORDER = 2
HY_EMB_DIM = 33
HY_DECAY_TARGET = 1e-2
HY_FAST_DECAY = 0.3
HY_SLOW_DECAY = 1.5
NA_WIN_ROWS = 8
NA_WIN_COLS = 16
N_EXPERTS = 8
LANES = 128
SEQ_TILE = 256
HALO = 8
CHUNK = 64
SCAN_CHUNKS = 4
HY_SEQS = 2
NA_ROWS = 4
VMEM_LIMIT = 56 * 1024 * 1024
NEG_BIG = -1e30


def _cparams(*sem):
    return pltpu.CompilerParams(dimension_semantics=sem, vmem_limit_bytes=VMEM_LIMIT)


def _dot(a, b):
    return jnp.dot(a.astype(BF16), b.astype(BF16), preferred_element_type=F32)


def _dot_nt(a, b):
    return lax.dot_general(a.astype(BF16), b.astype(BF16), (((1,), (1,)), ((), ())),
                           preferred_element_type=F32)


def _dot_tn(a, b):
    return lax.dot_general(a.astype(BF16), b.astype(BF16), (((0,), (0,)), ((), ())),
                           preferred_element_type=F32)


def _dot_split(a, b_exact):
    hi = a.astype(BF16)
    lo = (a - hi.astype(F32)).astype(BF16)
    return (jnp.dot(hi, b_exact, preferred_element_type=F32)
            + jnp.dot(lo, b_exact, preferred_element_type=F32))


def _sigmoid(x):
    return 1.0 / (1.0 + jnp.exp(-x))


def _silu(x):
    return x * _sigmoid(x)


def _head_ones():
    h = np.arange(WIDTH) // HEAD_DIM
    return jnp.asarray((h[:, None] == h[None, :]).astype(np.float32), dtype=BF16)


def _mm_kernel(x_ref, w_ref, o_ref, *, split):
    x = x_ref[...]
    w = w_ref[...]
    if split:
        xh = x.astype(BF16)
        xl = (x - xh.astype(F32)).astype(BF16)
        wh = w.astype(BF16)
        wl = (w - wh.astype(F32)).astype(BF16)
        o_ref[...] = (jnp.dot(xh, wh, preferred_element_type=F32)
                      + jnp.dot(xl, wh, preferred_element_type=F32)
                      + jnp.dot(xh, wl, preferred_element_type=F32))
    else:
        o_ref[...] = jnp.dot(x.astype(BF16), w.astype(BF16), preferred_element_type=F32)


def matmul(x, w, *, tn, split=False, name="matmul"):
    m, k = x.shape
    n = w.shape[1]
    assert n % tn == 0
    return pl.pallas_call(
        functools.partial(_mm_kernel, split=split),
        grid=(n // tn,),
        in_specs=[pl.BlockSpec((m, k), lambda j: (0, 0)),
                  pl.BlockSpec((k, tn), lambda j: (0, j))],
        out_specs=pl.BlockSpec((m, tn), lambda j: (0, j)),
        out_shape=jax.ShapeDtypeStruct((m, n), F32),
        compiler_params=_cparams("parallel"),
        name=name,
    )(x, w)


def _norm_mod(x, g, shift, scale):
    ms = jnp.mean(x * x, axis=-1, keepdims=True)
    h = x * lax.rsqrt(ms + NORM_EPS) * g
    return h * (1.0 + scale) + shift


def _proj_kernel(x_ref, sh_ref, sc_ref, g_ref, w_ref, qg_ref, kg_ref, ones_ref,
                 rw_o, hy_o, q_o, k_o, v_o, gate_o, *, splits):
    h = _norm_mod(x_ref[0], g_ref[...], sh_ref[0], sc_ref[0]).astype(BF16)
    s_rw, s_hy, s_na = splits
    proj = lambda lo, hi: jnp.dot(h, w_ref[:, lo:hi], preferred_element_type=F32)
    rw_o[0] = proj(0, s_rw)
    hy_o[0] = proj(s_rw, s_hy)
    gate_o[0] = proj(s_na, w_ref.shape[1]).astype(BF16)
    ones = ones_ref[...]
    inv_n = 1.0 / HEAD_DIM
    q = proj(s_hy, s_hy + WIDTH)
    k = proj(s_hy + WIDTH, s_hy + 2 * WIDTH)
    qn = q * lax.rsqrt(_dot_split(q * q, ones) * inv_n + NORM_EPS) * qg_ref[...]
    kn = k * lax.rsqrt(_dot_split(k * k, ones) * inv_n + NORM_EPS) * kg_ref[...]
    q_o[0] = (qn * (HEAD_DIM ** -0.5)).astype(BF16)
    k_o[0] = kn.astype(BF16)
    v_o[0] = proj(s_hy + 2 * WIDTH, s_na).astype(BF16)


def branch_projections(xs, shift, scale, g, w, q_gain, k_gain, ones, splits, *, tm=512):
    G, L, D = xs.shape
    s_rw, s_hy, s_na = splits
    widths = (s_rw, s_hy - s_rw, WIDTH, WIDTH, WIDTH, w.shape[1] - s_na)
    dtypes = (F32, F32, BF16, BF16, BF16, BF16)
    full = lambda a: pl.BlockSpec(a.shape, lambda g_, i: (0,) * a.ndim, pipeline_mode=pl.Buffered(1))
    return pl.pallas_call(
        functools.partial(_proj_kernel, splits=splits),
        grid=(G, L // tm),
        in_specs=[pl.BlockSpec((1, tm, D), lambda g_, i: (g_, i, 0)),
                  pl.BlockSpec((1, 1, D), lambda g_, i: (g_, 0, 0)),
                  pl.BlockSpec((1, 1, D), lambda g_, i: (g_, 0, 0)),
                  full(g), full(w), full(q_gain), full(k_gain), full(ones)],
        out_specs=[pl.BlockSpec((1, tm, n), lambda g_, i: (g_, i, 0)) for n in widths],
        out_shape=[jax.ShapeDtypeStruct((G, L, n), dt) for n, dt in zip(widths, dtypes)],
        compiler_params=_cparams("parallel", "parallel"),
        name="branch_projections",
    )(xs, shift, scale, g, w, q_gain, k_gain, ones)


def _halo_specs(width, n_tiles):
    per = SEQ_TILE // HALO
    last = n_tiles * per - 1
    return [pl.BlockSpec((1, SEQ_TILE, width), lambda g_, i: (g_, i, 0)),
            pl.BlockSpec((1, HALO, width), lambda g_, i: (g_, jnp.maximum(i * per - 1, 0), 0)),
            pl.BlockSpec((1, HALO, width), lambda g_, i: (g_, jnp.minimum((i + 1) * per, last), 0))]


def _neighbours(u, prev_ref, next_ref):
    g_ = pl.program_id(0)
    i = pl.program_id(1)
    n_tiles = pl.num_programs(1)
    has_prev = jnp.logical_and(g_ > 0, i > 0)
    has_next = jnp.logical_and(g_ > 0, i < n_tiles - 1)
    row = lax.broadcasted_iota(jnp.int32, u.shape, 0)
    halo_p = jnp.where(has_prev, prev_ref[0, HALO - 1:HALO, :], 0.0)
    halo_n = jnp.where(has_next, next_ref[0, 0:1, :], 0.0)
    prev = jnp.where(row == 0, halo_p, pltpu.roll(u, 1, 0))
    nxt = jnp.where(row == SEQ_TILE - 1, halo_n, pltpu.roll(u, SEQ_TILE - 1, 0))
    return prev, nxt


def _rope(z, cos, sin_signed):
    lane = lax.broadcasted_iota(jnp.int32, z.shape, 1)
    partner = jnp.where(lane % 32 < 16, pltpu.roll(z, WIDTH - 16, 1), pltpu.roll(z, 16, 1))
    return z * cos + partner * sin_signed


def _rw_feat_kernel(u_ref, up_ref, un_ref, cos_ref, sin_ref, mu_ref, w0_ref, w2_ref, a0_ref, a2_ref,
                    g2_ref, kk_ref, ka_ref, ones_ref,
                    r_o, v_o, kkn_o, g_o, ld0_o, ld1_o, k0_o, k1_o, b0_o, b1_o):
    u = u_ref[0]
    prev, nxt = _neighbours(u, up_ref, un_ref)
    u = u + mu_ref[0:1, :] * (prev - u) + mu_ref[1:2, :] * (nxt - u)
    r = u[:, 0:WIDTH]
    k = u[:, WIDTH:2 * WIDTH]
    v = u[:, 2 * WIDTH:3 * WIDTH]
    o = 3 * WIDTH
    lw = u[:, o:o + 2 * RW_LORA]
    la = u[:, o + 2 * RW_LORA:o + 4 * RW_LORA]
    lg = u[:, o + 4 * RW_LORA:o + 4 * RW_LORA + RW_GATE_LORA]
    is_lat = pl.program_id(0) > 0
    cos = jnp.where(is_lat, cos_ref[...], 1.0)
    sin = jnp.where(is_lat, sin_ref[...], 0.0)
    r = _rope(r, cos, sin)
    k = _rope(k, cos, sin)
    g = _dot(_sigmoid(lg), g2_ref[...])
    kk = k * kk_ref[...]
    nrm = jnp.sqrt(_dot_split(kk * kk, ones_ref[...]))
    kk = kk / jnp.maximum(nrm, 1e-12)
    r_o[0] = r
    v_o[0] = v
    kkn_o[0] = kk
    g_o[0] = g
    tanh_lw = jnp.tanh(lw)
    for d, (ld_o, k_o, b_o) in enumerate(((ld0_o, k0_o, b0_o), (ld1_o, k1_o, b1_o))):
        zw = w0_ref[d:d + 1, :] + _dot(tanh_lw[:, d * RW_LORA:(d + 1) * RW_LORA], w2_ref[d])
        softplus = jnp.maximum(-zw, 0.0) + jnp.log(1.0 + jnp.exp(-jnp.abs(zw)))
        ld_o[0] = -jnp.exp(-softplus - 0.5)
        iclr = _sigmoid(a0_ref[d:d + 1, :] + _dot(la[:, d * RW_LORA:(d + 1) * RW_LORA], a2_ref[d]))
        k_o[0] = k * (1.0 + (iclr - 1.0) * ka_ref[...])
        b_o[0] = kk * iclr


def rwkv_features(u_rw, cos, sin, mu, w0, w2, a0, a2, g2, k_k, k_a, ones):
    G, L, W_IN = u_rw.shape
    n_tiles = L // SEQ_TILE
    full = lambda a: pl.BlockSpec(a.shape, lambda g_, i: (0,) * a.ndim)
    tab = pl.BlockSpec((SEQ_TILE, WIDTH), lambda g_, i: (i, 0))
    params = (mu, w0, w2, a0, a2, g2, k_k, k_a, ones)
    out = jax.ShapeDtypeStruct((G, L, WIDTH), F32)
    return pl.pallas_call(
        _rw_feat_kernel,
        grid=(G, n_tiles),
        in_specs=_halo_specs(W_IN, n_tiles) + [tab, tab] + [full(p) for p in params],
        out_specs=[pl.BlockSpec((1, SEQ_TILE, WIDTH), lambda g_, i: (g_, i, 0))] * 10,
        out_shape=[out] * 10,
        compiler_params=_cparams("parallel", "parallel"),
        name="rwkv_features",
    )(u_rw, u_rw, u_rw, cos, sin, *params)


def _rw_chunk_operands(r, ld, k, v, kk, b, incl, reverse):
    C = CHUNK
    ld_hi = ld.astype(BF16)
    ld_lo = (ld - ld_hi.astype(F32)).astype(BF16)
    tri = jnp.where(incl, 1.0, 0.0).astype(BF16)
    cum = (jnp.dot(tri, ld_hi, preferred_element_type=F32)
           + jnp.dot(tri, ld_lo, preferred_element_type=F32))
    tot = cum[0:1, :] if reverse else cum[C - 1:C, :]
    e_neg = jnp.exp(-cum)
    e_rem = jnp.exp(tot - cum)
    return dict(
        a_t=(-kk * jnp.exp(cum - ld)).astype(BF16),
        r_t=(r * jnp.exp(cum)).astype(BF16),
        b_t=(b * e_neg).astype(BF16),
        k_t=(k * e_neg).astype(BF16),
        b_p=(b * e_rem).astype(BF16),
        k_p=(k * e_rem).astype(BF16),
        p_c=jnp.exp(tot),
        v=v.astype(BF16))


def _rw_scan_kernel(*refs):
    in_refs, y_refs, s_ref = refs[:12], refs[12:14], refs[14]
    C = CHUNK

    @pl.when(pl.program_id(1) == 0)
    def _():
        s_ref[...] = jnp.zeros_like(s_ref)

    row = lax.broadcasted_iota(jnp.int32, (C, C), 0)
    col = lax.broadcasted_iota(jnp.int32, (C, C), 1)
    eye = row == col
    chains = []
    for d in range(2):
        incl, strict = (row <= col, row < col) if d == 1 else (row >= col, row > col)
        for order in range(SCAN_CHUNKS):
            sub = SCAN_CHUNKS - 1 - order if d == 1 else order
            rs = slice(sub * C, (sub + 1) * C)
            ops = _rw_chunk_operands(*(ref[0, rs, :] for ref in in_refs[6 * d:6 * d + 6]), incl, d == 1)
            for h in range(HEADS):
                sl = slice(h * HEAD_DIM, (h + 1) * HEAD_DIM)
                ch = {name: val[:, sl] for name, val in ops.items()}
                ch.update(d=d, h=h, sl=sl, rs=rs, order=order, incl=incl, strict=strict)
                chains.append(ch)

    each = lambda fn: [fn(ch) for ch in chains]
    both = lambda fn, xs: [fn(ch, x) for ch, x in zip(chains, xs)]
    ar = each(lambda ch: jnp.concatenate([ch["a_t"], ch["r_t"]], axis=0))
    bk = each(lambda ch: jnp.concatenate([ch["b_t"], ch["k_t"]], axis=0))
    g = [_dot_nt(x, y) for x, y in zip(ar, bk)]
    a_ab = both(lambda ch, x: jnp.where(ch["strict"], x[:C, :C], 0.0), g)
    a_rb = both(lambda ch, x: jnp.where(ch["incl"], x[C:, :C], 0.0), g)
    a_k = both(lambda ch, x: jnp.concatenate([jnp.where(ch["strict"], x[:C, C:], 0.0),
                                              jnp.where(ch["incl"], x[C:, C:], 0.0)], axis=0), g)
    av = both(lambda ch, a: _dot(a, ch["v"]), a_k)
    t_inv = [jnp.where(eye, 1.0, a) for a in a_ab]
    pw = [_dot(a, a) for a in a_ab]
    for _ in range(int(math.log2(C)) - 2):
        nxt = [_dot(jnp.concatenate([p, t], axis=0), p) for p, t in zip(pw, t_inv)]
        t_inv = [t + x[C:] for t, x in zip(t_inv, nxt)]
        pw = [x[:C] for x in nxt]
    t_inv = [t + _dot(t, p) for t, p in zip(t_inv, pw)]
    w = [_dot(t, jnp.concatenate([ch["a_t"], x[:C].astype(BF16)], axis=1)) for ch, t, x in zip(chains, t_inv, av)]
    qy = [_dot(a, x) for a, x in zip(a_rb, w)]
    q_p = [ch["r_t"].astype(F32) + x[:, :HEAD_DIM] for ch, x in zip(chains, qy)]
    y_p = [x[C:] + z[:, HEAD_DIM:] for x, z in zip(av, qy)]
    mn = both(lambda ch, x: _dot_tn(x, ch["b_p"]), w)
    m_mat = both(lambda ch, x: jnp.where(eye, ch["p_c"], 0.0) + x[:HEAD_DIM], mn)
    n_mat = both(lambda ch, x: x[HEAD_DIM:] + _dot_tn(ch["v"], ch["k_p"]), mn)
    state = {(d, h): s_ref[d, h] for d in range(2) for h in range(HEADS)}
    for order in range(SCAN_CHUNKS):
        now = [i for i, ch in enumerate(chains) if ch["order"] == order]
        s0 = [state[chains[i]["d"], chains[i]["h"]] for i in now]
        y = [_dot_nt(q_p[i], s) + y_p[i] for i, s in zip(now, s0)]
        s1 = [_dot(s, m_mat[i]) + n_mat[i] for i, s in zip(now, s0)]
        for i, y_c, s_c in zip(now, y, s1):
            ch = chains[i]
            y_refs[ch["d"]][0, ch["rs"], ch["sl"]] = y_c
            state[ch["d"], ch["h"]] = s_c
    for (d, h), s_c in state.items():
        s_ref[d, h] = s_c


def rwkv_scan(r, v, kk, ld0, k0, b0, ld1, k1, b1, *, batch):
    G, L, _ = r.shape
    step = SCAN_CHUNKS * CHUNK
    lc = L // batch
    ns_ctx = lc // step
    ns_lat = L // step
    n_steps = ns_ctx + ns_lat

    def idx(reverse, bi, s):
        in_ctx = s < ns_ctx
        c_ctx = ns_ctx - 1 - s if reverse else s
        c_lat = n_steps - 1 - s if reverse else s - ns_ctx
        return (jnp.where(in_ctx, 0, bi + 1), jnp.where(in_ctx, bi * ns_ctx + c_ctx, c_lat), 0)

    fwd = pl.BlockSpec((1, step, WIDTH), functools.partial(idx, False))
    rev = pl.BlockSpec((1, step, WIDTH), functools.partial(idx, True))
    out = jax.ShapeDtypeStruct((G, L, WIDTH), F32)
    return pl.pallas_call(
        _rw_scan_kernel,
        grid=(batch, n_steps),
        in_specs=[fwd] * 6 + [rev] * 6,
        out_specs=[fwd, rev],
        out_shape=[out, out],
        scratch_shapes=[pltpu.VMEM((2, HEADS, HEAD_DIM, HEAD_DIM), F32)],
        compiler_params=_cparams("parallel", "arbitrary"),
        name="rwkv_scan",
    )(r, ld0, k0, v, kk, b0, r, ld1, k1, v, kk, b1)


def _rw_readout_kernel(y0_ref, y1_ref, r_ref, k0_ref, k1_ref, v_ref, g_ref, rk_ref, gg_ref, gb_ref,
                       ones_ref, o_ref):
    y = y0_ref[0] + y1_ref[0]
    ones = ones_ref[...]
    inv_n = 1.0 / HEAD_DIM
    mu = _dot_split(y, ones) * inv_n
    yc = y - mu
    var = _dot_split(yc * yc, ones) * inv_n
    yn = yc * lax.rsqrt(var + RW_GN_EPS) * gg_ref[...] + gb_ref[...]
    bonus = _dot_split(r_ref[0] * (k0_ref[0] + k1_ref[0]) * rk_ref[...], ones) * v_ref[0]
    o_ref[0] = (yn + bonus) * g_ref[0]


def rwkv_readout(y0, y1, r, k0, k1, v, g, r_k, gn_g, gn_b, ones, *, tm=512):
    G, L, _ = r.shape
    spec = pl.BlockSpec((1, tm, WIDTH), lambda g_, i: (g_, i, 0))
    full = lambda a: pl.BlockSpec(a.shape, lambda g_, i: (0,) * a.ndim)
    params = (r_k, gn_g, gn_b, ones)
    return pl.pallas_call(
        _rw_readout_kernel,
        grid=(G, L // tm),
        in_specs=[spec] * 7 + [full(p) for p in params],
        out_specs=spec,
        out_shape=jax.ShapeDtypeStruct((G, L, WIDTH), F32),
        compiler_params=_cparams("parallel", "parallel"),
        name="rwkv_readout",
    )(y0, y1, r, k0, k1, v, g, *params)


def _short_conv(u, w_ref, b_ref):
    n = u.shape[0]
    row = lax.broadcasted_iota(jnp.int32, u.shape, 0)
    prev = jnp.where(row == 0, 0.0, pltpu.roll(u, 1, 0))
    nxt = jnp.where(row == n - 1, 0.0, pltpu.roll(u, n - 1, 0))
    return b_ref[...] + w_ref[0:1, :] * prev + w_ref[1:2, :] * u + w_ref[2:3, :] * nxt


@functools.lru_cache(maxsize=None)
def _dft_tables(n):
    nn = 2 * n
    nf = n + 256
    kf = np.arange(nf, dtype=np.int64)
    t = np.arange(n, dtype=np.int64)
    ang = 2.0 * np.pi * ((kf[:, None] * t[None, :]) % nn).astype(np.float64) / nn
    valid = (kf <= n)[:, None]
    cosm = np.where(valid, np.cos(ang), 0.0)
    sinm = np.where(valid, np.sin(ang), 0.0)
    ck = np.where((kf == 0) | (kf == n), 1.0, 2.0)[:, None] / nn
    fwd = np.stack([cosm, -sinm])
    inv = np.stack([(ck * cosm).T, (-ck * sinm).T])
    return fwd, inv, nf


def _hy_conv_kernel(*refs, z_is_raw):
    nb = HY_SEQS
    z_refs, x_refs = refs[:nb], refs[nb:2 * nb]
    (zw_ref, zb_ref, xw_ref, xb_ref, fwd_ref, inv_ref, hre_ref, him_ref, skip_ref,
     o_ref, acc_ref, zs_ref) = refs[2 * nb:]
    f = pl.program_id(1)
    z_of = lambda i: _short_conv(z_refs[i][0], zw_ref, zb_ref) if z_is_raw else z_refs[i][0]

    @pl.when(f == 0)
    def _():
        acc_ref[...] = jnp.zeros_like(acc_ref)
        for i in range(nb):
            zs_ref[:, i * HY_WIDTH:(i + 1) * HY_WIDTH] = z_of(i).astype(BF16)

    zb = zs_ref[...]
    zre = jnp.dot(fwd_ref[0], zb, preferred_element_type=F32)
    zim = jnp.dot(fwd_ref[1], zb, preferred_element_type=F32)
    hre = jnp.concatenate([hre_ref[0]] * nb, axis=1)
    him = jnp.concatenate([him_ref[0]] * nb, axis=1)
    yre = zre * hre - zim * him
    yim = zre * him + zim * hre
    acc_ref[...] += (jnp.dot(inv_ref[0], yre.astype(BF16), preferred_element_type=F32)
                     + jnp.dot(inv_ref[1], yim.astype(BF16), preferred_element_type=F32))

    @pl.when(f == pl.num_programs(1) - 1)
    def _():
        for i in range(nb):
            conv = acc_ref[:, i * HY_WIDTH:(i + 1) * HY_WIDTH]
            o_ref[i] = _short_conv(x_refs[i][0], xw_ref, xb_ref) * (conv + skip_ref[0] * z_of(i))


def hyena_long_conv(z_arr, z_blk, x_arr, x_blk, conv_w, conv_b, hre, him, skip, order, fwd, inv, *, n_seq, n, tf):
    nf = fwd.shape[1]
    nb = HY_SEQS
    zo, zc = z_blk
    xo, xc = x_blk
    seq = lambda off, cb, i: pl.BlockSpec((1, n, HY_WIDTH), lambda s, f: (s * nb + i + off, 0, cb))
    taps = lambda cb: [pl.BlockSpec((conv_w.shape[0], HY_WIDTH), lambda s, f: (0, cb)),
                       pl.BlockSpec((1, HY_WIDTH), lambda s, f: (0, cb))]
    z_is_raw = order == 0
    return pl.pallas_call(
        functools.partial(_hy_conv_kernel, z_is_raw=z_is_raw),
        grid=(n_seq // nb, nf // tf),
        in_specs=[seq(zo, zc, i) for i in range(nb)] + [seq(xo, xc, i) for i in range(nb)]
                 + taps(zc if z_is_raw else 0) + taps(xc)
                 + [pl.BlockSpec((2, tf, n), lambda s, f: (0, f, 0)),
                    pl.BlockSpec((2, n, tf), lambda s, f: (0, 0, f)),
                    pl.BlockSpec((1, tf, HY_WIDTH), lambda s, f: (order, f, 0)),
                    pl.BlockSpec((1, tf, HY_WIDTH), lambda s, f: (order, f, 0)),
                    pl.BlockSpec((1, 1, HY_WIDTH), lambda s, f: (order, 0, 0))],
        out_specs=pl.BlockSpec((nb, n, HY_WIDTH), lambda s, f: (s, 0, 0)),
        out_shape=jax.ShapeDtypeStruct((n_seq, n, HY_WIDTH), F32),
        scratch_shapes=[pltpu.VMEM((n, nb * HY_WIDTH), F32), pltpu.VMEM((n, nb * HY_WIDTH), BF16)],
        compiler_params=_cparams("parallel", "arbitrary"),
        name=f"hyena_long_conv_n{n}_o{order}",
    )(*([z_arr] * nb), *([x_arr] * nb), conv_w, conv_b, conv_w, conv_b, fwd, inv, hre, him, skip)


def hyena_filter_spectrum(n, w1, b1, w2, b2, w3, freq, fwd):
    hp = lax.Precision.HIGHEST
    pos = jnp.arange(n, dtype=F32)
    t = pos / max(n - 1, 1)
    bands = (HY_EMB_DIM - 1) // 2
    fr = jnp.linspace(1e-4, bands - 1, bands, dtype=F32)
    ang = (2 * math.pi / n) * pos[:, None] * fr[None, :]
    z = jnp.concatenate([t[:, None], jnp.cos(ang), -jnp.sin(ang)], axis=-1)
    h = jnp.sin(freq * (jnp.dot(z, w1, precision=hp) + b1))
    h = jnp.sin(freq * (jnp.dot(h, w2, precision=hp) + b2))
    h = matmul(h, w3, tn=w3.shape[1], split=True, name="hyena_filter_out")
    h = h.reshape(n, HY_ORDER, 2, HY_WIDTH)
    max_decay = math.log(HY_DECAY_TARGET) / HY_FAST_DECAY
    min_decay = math.log(HY_DECAY_TARGET) / HY_SLOW_DECAY
    deltas = jnp.abs(jnp.linspace(min_decay, max_decay, HY_WIDTH, dtype=F32))
    h = h * jnp.exp(-t[:, None] * deltas)[:, None, None, :]
    h_fwd = h[:, :, 0]
    h_bwd = h[:, :, 1] * (pos > 0).astype(F32)[:, None, None]
    l1 = jnp.sum(jnp.abs(h_fwd), axis=0) + jnp.sum(jnp.abs(h_bwd), axis=0)
    even = ((h_fwd + h_bwd) / l1).reshape(n, HY_ORDER * HY_WIDTH)
    odd = ((h_fwd - h_bwd) / l1).reshape(n, HY_ORDER * HY_WIDTH)
    nf = fwd.shape[1]
    hre = matmul(fwd[0], even, tn=HY_ORDER * HY_WIDTH, name="hyena_filter_dft_re")
    him = matmul(fwd[1], odd, tn=HY_ORDER * HY_WIDTH, name="hyena_filter_dft_im")
    to_ofc = lambda a: jnp.moveaxis(a.reshape(nf, HY_ORDER, HY_WIDTH), 1, 0)
    return to_ofc(hre), to_ofc(him)


def _na_lat_kernel(q_ref, k_ref, v_ref, kc_ref, vc_ref, bias_ref, o_ref, *, rows, wr):
    kc = kc_ref[0]
    vc = vc_ref[0]
    jobs = []
    for rr in range(NA_ROWS):
        i = pl.program_id(1) * NA_ROWS + rr
        r0 = jnp.clip(i - wr // 2, 0, rows - wr)
        off = r0 - i + (NA_WIN_ROWS - 1)
        start = pl.multiple_of(r0 * GRID_W, GRID_W)
        qs = slice(rr * GRID_W, (rr + 1) * GRID_W)
        q = q_ref[0, qs, :]
        kw = k_ref[0, pl.ds(start, wr * GRID_W), :]
        vw = v_ref[0, pl.ds(start, wr * GRID_W), :]
        for h in range(HEADS):
            sl = slice(h * HEAD_DIM, (h + 1) * HEAD_DIM)
            jobs.append(dict(q=q[:, sl], kw=kw[:, sl], vw=vw[:, sl], h=h, off=off, qs=qs, sl=sl))
    s_loc = [_dot_nt(j["q"], j["kw"]) + bias_ref[j["h"], j["off"]] for j in jobs]
    s_ctx = [_dot_nt(j["q"], kc[:, j["sl"]]) for j in jobs]
    m = [jnp.maximum(jnp.max(a, axis=-1, keepdims=True), jnp.max(b, axis=-1, keepdims=True))
         for a, b in zip(s_loc, s_ctx)]
    p_loc = [jnp.exp(a - mm) for a, mm in zip(s_loc, m)]
    p_ctx = [jnp.exp(b - mm) for b, mm in zip(s_ctx, m)]
    den = [jnp.sum(a, axis=-1, keepdims=True) + jnp.sum(b, axis=-1, keepdims=True) for a, b in zip(p_loc, p_ctx)]
    o = [_dot(a, j["vw"]) + _dot(b, vc[:, j["sl"]]) for a, b, j in zip(p_loc, p_ctx, jobs)]
    for j, oo, dd in zip(jobs, o, den):
        o_ref[0, j["qs"], j["sl"]] = oo / dd


def _na_bias_table(rpb, rows, wr):
    hp = lax.Precision.HIGHEST
    offs = np.arange(NA_WIN_ROWS) - (NA_WIN_ROWS - 1)
    dr = offs[:, None] + np.arange(wr)[None, :] + NA_WIN_ROWS - 1
    c_ar = np.arange(GRID_W)
    dc = np.clip(c_ar[None, :] - c_ar[:, None] + NA_WIN_COLS - 1, 0, 2 * NA_WIN_COLS - 2)
    c0 = np.clip(c_ar - NA_WIN_COLS // 2, 0, GRID_W - NA_WIN_COLS)
    in_win = (c_ar[None, :] >= c0[:, None]) & (c_ar[None, :] < c0[:, None] + NA_WIN_COLS)
    oh_r = jnp.asarray(dr[..., None] == np.arange(2 * NA_WIN_ROWS - 1), dtype=F32)
    oh_c = jnp.asarray(dc[..., None] == np.arange(2 * NA_WIN_COLS - 1), dtype=F32)
    by_row = jnp.einsum("hab,ora->horb", rpb.astype(F32), oh_r, precision=hp)
    bias = jnp.einsum("horb,qkb->hoqrk", by_row, oh_c, precision=hp)
    bias = jnp.where(in_win[None, None, :, None, :], bias, NEG_BIG)
    return bias.reshape(HEADS, NA_WIN_ROWS, GRID_W, wr * GRID_W)


def na_latent(qn, kn, vb, bias, *, batch):
    G, L, _ = qn.shape
    lc = L // batch
    rows = L // GRID_W
    wr = min(NA_WIN_ROWS, rows)
    assert rows % NA_ROWS == 0
    qspec = pl.BlockSpec((1, NA_ROWS * GRID_W, WIDTH), lambda b_, i: (b_ + 1, i, 0))
    seq = pl.BlockSpec((1, L, WIDTH), lambda b_, i: (b_ + 1, 0, 0))
    ctx = pl.BlockSpec((1, lc, WIDTH), lambda b_, i: (0, b_, 0))
    return pl.pallas_call(
        functools.partial(_na_lat_kernel, rows=rows, wr=wr),
        grid=(batch, rows // NA_ROWS),
        in_specs=[qspec, seq, seq, ctx, ctx, pl.BlockSpec(bias.shape, lambda b_, i: (0, 0, 0, 0))],
        out_specs=pl.BlockSpec((1, NA_ROWS * GRID_W, WIDTH), lambda b_, i: (b_, i, 0)),
        out_shape=jax.ShapeDtypeStruct((batch, L, WIDTH), F32),
        compiler_params=_cparams("parallel", "arbitrary"),
        name="na_latent",
    )(qn, kn, vb, kn, vb, bias)


def _na_ctx_kernel(q_ref, k_ref, v_ref, o_ref):
    q = q_ref[0]
    k = k_ref[0]
    v = v_ref[0]
    for h in range(HEADS):
        sl = slice(h * HEAD_DIM, (h + 1) * HEAD_DIM)
        s = _dot_nt(q[:, sl], k[:, sl])
        p = jnp.exp(s - jnp.max(s, axis=-1, keepdims=True))
        o_ref[0, :, sl] = _dot(p, v[:, sl]) / jnp.sum(p, axis=-1, keepdims=True)


def na_context(qn, kn, vb, *, batch):
    G, L, _ = qn.shape
    lc = L // batch
    spec = pl.BlockSpec((1, lc, WIDTH), lambda b_: (0, b_, 0))
    return pl.pallas_call(
        _na_ctx_kernel,
        grid=(batch,),
        in_specs=[spec] * 3,
        out_specs=spec,
        out_shape=jax.ShapeDtypeStruct((1, L, WIDTH), F32),
        compiler_params=_cparams("parallel"),
        name="na_context",
    )(qn, kn, vb)


def _merge_kernel(x_ref, gl_ref, rw_ref, hy_ref, na_ref, g1_ref, wr_ref, wh_ref, wn_ref, wo_ref, o_ref):
    D = x_ref.shape[-1]
    gl = gl_ref[0].astype(F32)
    m = (_sigmoid(gl[:, 0:D]) * _dot(rw_ref[0], wr_ref[...])
         + _sigmoid(gl[:, D:2 * D]) * _dot(hy_ref[0], wh_ref[...])
         + _sigmoid(gl[:, 2 * D:3 * D]) * _dot(na_ref[0], wn_ref[...]))
    o_ref[0] = x_ref[0] + g1_ref[0] * _dot(m, wo_ref[...])


def merge_branches(xs, u_gate, y_rw, y_hy, y_na, gate1, w_rw, w_hy, w_na, w_o, *, g_off, tm=512):
    G, L, D = xs.shape
    tok = lambda width: pl.BlockSpec((1, tm, width), lambda g_, i: (g_ + g_off, i, 0))
    own = lambda width: pl.BlockSpec((1, tm, width), lambda g_, i: (g_, i, 0))
    full = lambda a: pl.BlockSpec(a.shape, lambda g_, i: (0,) * a.ndim, pipeline_mode=pl.Buffered(1))
    weights = (w_rw, w_hy, w_na, w_o)
    return pl.pallas_call(
        _merge_kernel,
        grid=(G - g_off, L // tm),
        in_specs=[tok(D), tok(3 * D), tok(WIDTH), own(HY_WIDTH), own(WIDTH),
                  pl.BlockSpec((1, 1, D), lambda g_, i: (g_ + g_off, 0, 0))] + [full(w) for w in weights],
        out_specs=pl.BlockSpec((1, tm, D), lambda g_, i: (g_, i, 0)),
        out_shape=jax.ShapeDtypeStruct((G - g_off, L, D), F32),
        compiler_params=_cparams("parallel", "parallel"),
        name="merge_branches",
    )(xs, u_gate, y_rw, y_hy, y_na, gate1, *weights)


def _ffn_kernel(x_ref, sh_ref, sc_ref, g_ref, g2_ref, w1_ref, w3_ref, w2_ref, o_ref, h_ref, acc_ref):
    f = pl.program_id(2)

    @pl.when(f == 0)
    def _():
        h_ref[...] = _norm_mod(x_ref[0], g_ref[...], sh_ref[0], sc_ref[0]).astype(BF16)
        acc_ref[...] = jnp.zeros_like(acc_ref)

    h = h_ref[...]
    a = jnp.dot(h, w1_ref[...], preferred_element_type=F32)
    b = jnp.dot(h, w3_ref[...], preferred_element_type=F32)
    acc_ref[...] += _dot(_silu(a) * b, w2_ref[...])

    @pl.when(f == pl.num_programs(2) - 1)
    def _():
        o_ref[0] = x_ref[0] + g2_ref[0] * acc_ref[...]


def ffn_dense(xs, shift, scale, g, gate2, w1, w3, w2, *, tm=1024, tf):
    G, L, D = xs.shape
    FF = w1.shape[1]
    assert L % tm == 0 and FF % tf == 0
    mod = pl.BlockSpec((1, 1, D), lambda g_, i, f: (g_, 0, 0))
    return pl.pallas_call(
        _ffn_kernel,
        grid=(G, L // tm, FF // tf),
        in_specs=[pl.BlockSpec((1, tm, D), lambda g_, i, f: (g_, i, 0)), mod, mod,
                  pl.BlockSpec((1, D), lambda g_, i, f: (0, 0)), mod,
                  pl.BlockSpec((D, tf), lambda g_, i, f: (0, f)),
                  pl.BlockSpec((D, tf), lambda g_, i, f: (0, f)),
                  pl.BlockSpec((tf, D), lambda g_, i, f: (f, 0))],
        out_specs=pl.BlockSpec((1, tm, D), lambda g_, i, f: (g_, i, 0)),
        out_shape=jax.ShapeDtypeStruct((G, L, D), F32),
        scratch_shapes=[pltpu.VMEM((tm, D), BF16), pltpu.VMEM((tm, D), F32)],
        compiler_params=_cparams("parallel", "parallel", "arbitrary"),
        name="ffn_dense",
    )(xs, shift, scale, g, gate2, w1, w3, w2)


def _moe_route_kernel(x_ref, sh_ref, sc_ref, g_ref, rt_ref, h_o, gates_o, rank_o, rank_t_o, cnt_o,
                      cnt_row, cnt_col):
    i = pl.program_id(1)

    @pl.when(i == 0)
    def _():
        cnt_row[...] = jnp.zeros_like(cnt_row)
        cnt_col[...] = jnp.zeros_like(cnt_col)

    h = _norm_mod(x_ref[0], g_ref[...], sh_ref[0], sc_ref[0])
    h_o[0] = h.astype(BF16)
    rt = rt_ref[...]
    hh = h.astype(BF16)
    hl = (h - hh.astype(F32)).astype(BF16)
    rh = rt.astype(BF16)
    rl = (rt - rh.astype(F32)).astype(BF16)
    logits = (jnp.dot(hh, rh, preferred_element_type=F32) + jnp.dot(hl, rh, preferred_element_type=F32)
              + jnp.dot(hh, rl, preferred_element_type=F32))
    lane = lax.broadcasted_iota(jnp.int32, logits.shape, 1)
    logits = jnp.where(lane < N_EXPERTS, logits, NEG_BIG)
    m1 = jnp.max(logits, axis=-1, keepdims=True)
    i1 = jnp.min(jnp.where(logits == m1, lane, LANES), axis=-1, keepdims=True)
    rest = jnp.where(lane == i1, NEG_BIG, logits)
    m2 = jnp.max(rest, axis=-1, keepdims=True)
    i2 = jnp.min(jnp.where(rest == m2, lane, LANES), axis=-1, keepdims=True)
    e2 = jnp.exp(m2 - m1)
    gates = jnp.where(lane == i1, 1.0 / (1.0 + e2), 0.0) + jnp.where(lane == i2, e2 / (1.0 + e2), 0.0)
    gates_o[0] = gates

    tm = gates.shape[0]
    sel = gates > 0.0
    m_tok = jnp.where(sel, 1.0, 0.0)
    m_exp = m_tok.T[:2 * HALO]
    row = lax.broadcasted_iota(jnp.int32, (tm, tm), 0)
    col = lax.broadcasted_iota(jnp.int32, (tm, tm), 1)
    before = jnp.where(col < row, 1.0, 0.0).astype(BF16)
    after = jnp.where(row < col, 1.0, 0.0).astype(BF16)
    rank = jnp.dot(before, m_tok.astype(BF16), preferred_element_type=F32) + cnt_row[...]
    rank_t = jnp.dot(m_exp.astype(BF16), after, preferred_element_type=F32)[:HALO]
    rank_t = rank_t + jnp.tile(cnt_col[...], (1, tm // LANES))
    rank_o[0] = jnp.where(sel, rank, NEG_BIG)
    rank_t_o[0] = jnp.where(m_exp[:HALO] > 0.0, rank_t, NEG_BIG)
    cnt_row[...] += jnp.sum(m_tok, axis=0, keepdims=True)
    cnt_col[...] += jnp.sum(m_exp[:HALO], axis=1, keepdims=True)
    cnt_o[0] = cnt_row[...]


def _moe_expert_kernel(cnt_ref, h_ref, gates_ref, rank_ref, rank_t_ref, x_ref, g2_ref, w1_ref, w3_ref, w2_ref,
                       o_ref, hs_ref, ys_ref, *, rb):
    gi = pl.program_id(0)
    e = pl.program_id(1)
    f = pl.program_id(2)
    n_e = cnt_ref[gi * pl.num_programs(1) + e]
    n_half = (n_e + rb // 2 - 1) // (rb // 2)
    n_full = n_half // 2
    tm = h_ref.shape[1]

    def for_blocks(fn):
        def body(j, carry):
            fn(pl.multiple_of(j * rb, rb), rb)
            return carry

        lax.fori_loop(0, n_full, body, 0)

        @pl.when(n_half % 2 == 1)
        def _():
            fn(pl.multiple_of(n_full * rb, rb), rb // 2)

    @pl.when(jnp.logical_and(e == 0, f == 0))
    def _():
        o_ref[...] = jnp.zeros_like(o_ref)

    @pl.when(f == 0)
    def _():
        rank_t = rank_t_ref[0, pl.ds(e, 1), :]

        def gather(start, size):
            sub = lax.broadcasted_iota(jnp.int32, (size, tm), 0).astype(F32) + start.astype(F32)
            pick = jnp.where(rank_t == sub, 1.0, 0.0).astype(BF16)
            hs_ref[pl.ds(start, size), :] = jnp.dot(pick, h_ref[0], preferred_element_type=F32).astype(BF16)

        for_blocks(gather)

    def expert_block(start, size):
        hs = hs_ref[pl.ds(start, size), :]
        a = jnp.dot(hs, w1_ref[0], preferred_element_type=F32)
        b = jnp.dot(hs, w3_ref[0], preferred_element_type=F32)
        return _dot(_silu(a) * b, w2_ref[0])

    @pl.when(f == 0)
    def _():
        def first(start, size):
            ys_ref[pl.ds(start, size), :] = expert_block(start, size)

        for_blocks(first)

    @pl.when(f > 0)
    def _():
        def more(start, size):
            ys_ref[pl.ds(start, size), :] += expert_block(start, size)

        for_blocks(more)

    last_f = f == pl.num_programs(2) - 1

    @pl.when(last_f)
    def _():
        lane = lax.broadcasted_iota(jnp.int32, (tm, LANES), 1)
        gate = jnp.sum(jnp.where(lane == e, gates_ref[0], 0.0), axis=-1, keepdims=True)
        rank = jnp.sum(jnp.where(lane == e, rank_ref[0], 0.0), axis=-1, keepdims=True)

        def scatter(start, size):
            slot = lax.broadcasted_iota(jnp.int32, (tm, size), 1).astype(F32) + start.astype(F32)
            put = jnp.where(rank == slot, gate, 0.0)
            o_ref[0] += _dot(put, ys_ref[pl.ds(start, size), :])

        for_blocks(scatter)

    @pl.when(jnp.logical_and(e == pl.num_programs(1) - 1, last_f))
    def _():
        o_ref[0] = x_ref[0] + g2_ref[0] * o_ref[0]


def moe_ffn(xs, shift, scale, g, gate2, router, w1, w3, w2, *, g_off, tm_route=512, rb=256, tf):
    G, L, D = xs.shape
    E, _, FF = w1.shape
    n_g = G - g_off
    mod = pl.BlockSpec((1, 1, D), lambda g_, i: (g_ + g_off, 0, 0))
    tok = lambda width: pl.BlockSpec((1, tm_route, width), lambda g_, i: (g_, i, 0))
    h, gates, rank, rank_t, cnt = pl.pallas_call(
        _moe_route_kernel,
        grid=(n_g, L // tm_route),
        in_specs=[pl.BlockSpec((1, tm_route, D), lambda g_, i: (g_ + g_off, i, 0)), mod, mod,
                  pl.BlockSpec((1, D), lambda g_, i: (0, 0)),
                  pl.BlockSpec((D, LANES), lambda g_, i: (0, 0))],
        out_specs=[tok(D), tok(LANES), tok(LANES),
                   pl.BlockSpec((1, HALO, tm_route), lambda g_, i: (g_, 0, i)),
                   pl.BlockSpec((1, 1, LANES), lambda g_, i: (g_, 0, 0))],
        out_shape=[jax.ShapeDtypeStruct((n_g, L, D), BF16),
                   jax.ShapeDtypeStruct((n_g, L, LANES), F32),
                   jax.ShapeDtypeStruct((n_g, L, LANES), F32),
                   jax.ShapeDtypeStruct((n_g, HALO, L), F32),
                   jax.ShapeDtypeStruct((n_g, 1, LANES), F32)],
        scratch_shapes=[pltpu.VMEM((1, LANES), F32), pltpu.VMEM((HALO, LANES), F32)],
        compiler_params=_cparams("parallel", "arbitrary"),
        name="moe_route",
    )(xs, shift, scale, g, router)
    counts = cnt[:, 0, :E].astype(jnp.int32).reshape(-1)

    once = pl.Buffered(1)
    grp = lambda width: pl.BlockSpec((1, L, width), lambda g_, e, f, c: (g_, 0, 0), pipeline_mode=once)
    return pl.pallas_call(
        functools.partial(_moe_expert_kernel, rb=rb),
        grid_spec=pltpu.PrefetchScalarGridSpec(
            num_scalar_prefetch=1,
            grid=(n_g, E, FF // tf),
            in_specs=[grp(D), grp(LANES), grp(LANES),
                      pl.BlockSpec((1, HALO, L), lambda g_, e, f, c: (g_, 0, 0), pipeline_mode=once),
                      pl.BlockSpec((1, L, D), lambda g_, e, f, c: (g_ + g_off, 0, 0), pipeline_mode=once),
                      pl.BlockSpec((1, 1, D), lambda g_, e, f, c: (g_ + g_off, 0, 0)),
                      pl.BlockSpec((1, D, tf), lambda g_, e, f, c: (e, 0, f)),
                      pl.BlockSpec((1, D, tf), lambda g_, e, f, c: (e, 0, f)),
                      pl.BlockSpec((1, tf, D), lambda g_, e, f, c: (e, f, 0))],
            out_specs=pl.BlockSpec((1, L, D), lambda g_, e, f, c: (g_, 0, 0), pipeline_mode=once),
            scratch_shapes=[pltpu.VMEM((L + rb, D), BF16), pltpu.VMEM((L + rb, D), F32)]),
        out_shape=jax.ShapeDtypeStruct((n_g, L, D), F32),
        compiler_params=pltpu.CompilerParams(dimension_semantics=("parallel", "arbitrary", "arbitrary"),
                                             vmem_limit_bytes=60 * 1024 * 1024),
        name="moe_experts",
    )(counts, h, gates, rank, rank_t, xs, gate2, w1, w3, w2)


def _rope_tables(n):
    half = HEAD_DIM // 2
    nf = half // 2
    t = np.arange(n)
    inv = ROPE_BASE ** (-np.arange(nf, dtype=np.float64) / nf)
    d = np.arange(HEAD_DIM)
    pos = np.where(d[None, :] < half, (t // GRID_W)[:, None], (t % GRID_W)[:, None]).astype(np.float64)
    ang = pos * inv[d % nf][None, :]
    sign = np.where(d % half < nf, -1.0, 1.0)
    cos = np.tile(np.cos(ang), (1, HEADS))
    sin = np.tile(np.sin(ang) * sign[None, :], (1, HEADS))
    return jnp.asarray(cos, dtype=F32), jnp.asarray(sin, dtype=F32)


def kernel(x, c, ctx, c_ctx, ada_w, ada_b, norm1_g, norm2_g, w_in, rw_shift, rw_w0, rw_w2, rw_a0, rw_a2, rw_g2, rw_kk, rw_ka, rw_rk, rw_gn_g, rw_gn_b, hy_conv_w, hy_conv_b, hy_f_w1, hy_f_b1, hy_f_w2, hy_f_b2, hy_f_w3, hy_f_freq, hy_skip, na_q_gain, na_k_gain, na_rpb, w_br_rw, w_br_hy, w_br_na, w_out, ff_w1, ff_w3, ff_w2, moe_router, moe_w1, moe_w3, moe_w2):
    B, L, D = x.shape
    lc = ctx.shape[1]
    depth = ada_w.shape[0]
    assert B * lc == L and lc == SEQ_TILE and L % GRID_W == 0
    G = B + 1
    rows = L // GRID_W
    wr = min(NA_WIN_ROWS, rows)
    rw_in = 3 * WIDTH + 4 * RW_LORA + RW_GATE_LORA
    hy_in = (HY_ORDER + 1) * HY_WIDTH
    na_in = 3 * WIDTH
    splits = (rw_in, rw_in + hy_in, rw_in + hy_in + na_in)

    xs = jnp.concatenate([ctx.reshape(1, L, D), x], axis=0)
    act = jnp.concatenate([c_ctx[None, :], c], axis=0)
    act = act * _sigmoid(act)
    act = jnp.pad(act, ((0, (-G) % 16), (0, 0)))
    ones = _head_ones()
    cos, sin = _rope_tables(L)
    fwd_lat, inv_lat, _ = _dft_tables(L)
    fwd_ctx, inv_ctx, _ = _dft_tables(lc)
    fwd_lat, inv_lat, fwd_ctx, inv_ctx = (jnp.asarray(a, dtype=F32).astype(BF16)
                                          for a in (fwd_lat, inv_lat, fwd_ctx, inv_ctx))
    row2 = lambda a: a.reshape(1, -1)

    for li in range(depth):
        last = li == depth - 1
        mods = matmul(act, ada_w[li], tn=1024, split=True, name="adaln")[:G] + ada_b[li]
        sh1, sc1, g1, sh2, sc2, g2 = (m.reshape(G, 1, D) for m in jnp.split(mods, 6, axis=-1))

        u_rw, u_hy, qn, kn, vb, u_gate = branch_projections(
            xs, sh1, sc1, row2(norm1_g[li]), w_in[li].astype(BF16), row2(jnp.tile(na_q_gain[li], HEADS)),
            row2(jnp.tile(na_k_gain[li], HEADS)), ones, splits)

        r, v, kk, g, ld0, ld1, k0, k1, b0, b1 = rwkv_features(
            u_rw, cos, sin, rw_shift[li], rw_w0[li], rw_w2[li].astype(BF16), rw_a0[li], rw_a2[li].astype(BF16),
            rw_g2[li].astype(BF16), row2(rw_kk[li]), row2(rw_ka[li]), ones)
        y0, y1 = rwkv_scan(r, v, kk, ld0, k0, b0, ld1, k1, b1, batch=B)
        y_rw = rwkv_readout(y0, y1, r, k0, k1, v, g, row2(rw_rk[li]), row2(rw_gn_g[li]), row2(rw_gn_b[li]), ones)

        f_args = (hy_f_w1[li], hy_f_b1[li], hy_f_w2[li], hy_f_b2[li], hy_f_w3[li], hy_f_freq[li])
        taps = (hy_conv_w[li], row2(hy_conv_b[li]))
        skip = hy_skip[li].reshape(HY_ORDER, 1, HY_WIDTH)
        hre, him = hyena_filter_spectrum(L, *f_args, fwd_lat)
        conv = functools.partial(hyena_long_conv, skip=skip, fwd=fwd_lat, inv=inv_lat, n_seq=B, n=L, tf=384)
        z1 = conv(u_hy, (1, 0), u_hy, (1, 1), *taps, hre, him, order=0)
        y_hy = conv(z1, (0, 0), u_hy, (1, 2), *taps, hre, him, order=1)
        if not last:
            hy_c = u_hy[0].reshape(B, lc, hy_in)
            hre_c, him_c = hyena_filter_spectrum(lc, *f_args, fwd_ctx)
            conv_c = functools.partial(hyena_long_conv, skip=skip, fwd=fwd_ctx, inv=inv_ctx, n_seq=B, n=lc,
                                       tf=fwd_ctx.shape[1])
            z1c = conv_c(hy_c, (0, 0), hy_c, (0, 1), *taps, hre_c, him_c, order=0)
            y_hy_c = conv_c(z1c, (0, 0), hy_c, (0, 2), *taps, hre_c, him_c, order=1)
            y_hy = jnp.concatenate([y_hy_c.reshape(1, L, HY_WIDTH), y_hy], axis=0)

        bias = _na_bias_table(na_rpb[li], rows, wr)
        y_na = na_latent(qn, kn, vb, bias, batch=B)
        if not last:
            y_na = jnp.concatenate([na_context(qn, kn, vb, batch=B), y_na], axis=0)

        g_off = 1 if last else 0
        xs = merge_branches(xs, u_gate, y_rw, y_hy, y_na, g1, w_br_rw[li].astype(BF16), w_br_hy[li].astype(BF16),
                            w_br_na[li].astype(BF16), w_out[li].astype(BF16), g_off=g_off)
        if last:
            sh2, sc2, g2 = sh2[1:], sc2[1:], g2[1:]

        n2 = row2(norm2_g[li])
        if li % 2 == 0:
            j = li // 2
            xs = ffn_dense(xs, sh2, sc2, n2, g2, ff_w1[j].astype(BF16), ff_w3[j].astype(BF16), ff_w2[j].astype(BF16),
                           tf=ff_w1.shape[2] // 2)
        else:
            j = li // 2
            router = jnp.pad(moe_router[j], ((0, 0), (0, LANES - N_EXPERTS)))
            xs = moe_ffn(xs, sh2, sc2, n2, g2, router, moe_w1[j].astype(BF16), moe_w3[j].astype(BF16),
                         moe_w2[j].astype(BF16), g_off=0, tf=moe_w1.shape[3] // 4)
    return xs if depth == 0 else xs[-B:]
```

```python
import functools
import math

import numpy as np
import jax
import jax.numpy as jnp
from jax import lax
from jax.experimental import pallas as pl
from jax.experimental.pallas import tpu as pltpu

F32 = jnp.float32
BF16 = jnp.bfloat16

NORM_EPS = 1e-6
ROPE_BASE = 10000.0
GRID_W = 64
HEADS = 6
HEAD_DIM = 64
WIDTH = HEADS * HEAD_DIM
RW_LORA = 64
RW_GATE_LORA = 128
RW_GN_EPS = 64e-5
HY_WIDTH = 256
HY_ORDER = 2
HY_EMB_DIM = 33
HY_DECAY_TARGET = 1e-2
HY_FAST_DECAY = 0.3
HY_SLOW_DECAY = 1.5
NA_WIN_ROWS = 8
NA_WIN_COLS = 16
N_EXPERTS = 8
LANES = 128
SEQ_TILE = 256
HALO = 8
CHUNK = 64
SCAN_CHUNKS = 4
HY_SEQS = 2
NA_ROWS = 4
VMEM_LIMIT = 56 * 1024 * 1024
NEG_BIG = -1e30


def _cparams(*sem):
    return pltpu.CompilerParams(dimension_semantics=sem, vmem_limit_bytes=VMEM_LIMIT)


def _dot(a, b):
    return jnp.dot(a.astype(BF16), b.astype(BF16), preferred_element_type=F32)


def _dot_nt(a, b):
    return lax.dot_general(a.astype(BF16), b.astype(BF16), (((1,), (1,)), ((), ())),
                           preferred_element_type=F32)


def _dot_tn(a, b):
    return lax.dot_general(a.astype(BF16), b.astype(BF16), (((0,), (0,)), ((), ())),
                           preferred_element_type=F32)


def _dot_split(a, b_exact):
    hi = a.astype(BF16)
    lo = (a - hi.astype(F32)).astype(BF16)
    return (jnp.dot(hi, b_exact, preferred_element_type=F32)
            + jnp.dot(lo, b_exact, preferred_element_type=F32))


def _sigmoid(x):
    return 1.0 / (1.0 + jnp.exp(-x))


def _silu(x):
    return x * _sigmoid(x)


def _head_ones():
    h = np.arange(WIDTH) // HEAD_DIM
    return jnp.asarray((h[:, None] == h[None, :]).astype(np.float32), dtype=BF16)


def _mm_kernel(x_ref, w_ref, o_ref, *, split):
    x = x_ref[...]
    w = w_ref[...]
    if split:
        xh = x.astype(BF16)
        xl = (x - xh.astype(F32)).astype(BF16)
        wh = w.astype(BF16)
        wl = (w - wh.astype(F32)).astype(BF16)
        o_ref[...] = (jnp.dot(xh, wh, preferred_element_type=F32)
                      + jnp.dot(xl, wh, preferred_element_type=F32)
                      + jnp.dot(xh, wl, preferred_element_type=F32))
    else:
        o_ref[...] = jnp.dot(x.astype(BF16), w.astype(BF16), preferred_element_type=F32)


def matmul(x, w, *, tn, split=False, name="matmul"):
    m, k = x.shape
    n = w.shape[1]
    assert n % tn == 0
    return pl.pallas_call(
        functools.partial(_mm_kernel, split=split),
        grid=(n // tn,),
        in_specs=[pl.BlockSpec((m, k), lambda j: (0, 0)),
                  pl.BlockSpec((k, tn), lambda j: (0, j))],
        out_specs=pl.BlockSpec((m, tn), lambda j: (0, j)),
        out_shape=jax.ShapeDtypeStruct((m, n), F32),
        compiler_params=_cparams("parallel"),
        name=name,
    )(x, w)


def _norm_mod(x, g, shift, scale):
    ms = jnp.mean(x * x, axis=-1, keepdims=True)
    h = x * lax.rsqrt(ms + NORM_EPS) * g
    return h * (1.0 + scale) + shift


def _proj_kernel(x_ref, sh_ref, sc_ref, g_ref, w_ref, qg_ref, kg_ref, ones_ref,
                 rw_o, hy_o, q_o, k_o, v_o, gate_o, *, splits):
    h = _norm_mod(x_ref[0], g_ref[...], sh_ref[0], sc_ref[0]).astype(BF16)
    s_rw, s_hy, s_na = splits
    proj = lambda lo, hi: jnp.dot(h, w_ref[:, lo:hi], preferred_element_type=F32)
    rw_o[0] = proj(0, s_rw)
    hy_o[0] = proj(s_rw, s_hy)
    gate_o[0] = proj(s_na, w_ref.shape[1]).astype(BF16)
    ones = ones_ref[...]
    inv_n = 1.0 / HEAD_DIM
    q = proj(s_hy, s_hy + WIDTH)
    k = proj(s_hy + WIDTH, s_hy + 2 * WIDTH)
    qn = q * lax.rsqrt(_dot_split(q * q, ones) * inv_n + NORM_EPS) * qg_ref[...]
    kn = k * lax.rsqrt(_dot_split(k * k, ones) * inv_n + NORM_EPS) * kg_ref[...]
    q_o[0] = (qn * (HEAD_DIM ** -0.5)).astype(BF16)
    k_o[0] = kn.astype(BF16)
    v_o[0] = proj(s_hy + 2 * WIDTH, s_na).astype(BF16)


def branch_projections(xs, shift, scale, g, w, q_gain, k_gain, ones, splits, *, tm=512):
    G, L, D = xs.shape
    s_rw, s_hy, s_na = splits
    widths = (s_rw, s_hy - s_rw, WIDTH, WIDTH, WIDTH, w.shape[1] - s_na)
    dtypes = (F32, F32, BF16, BF16, BF16, BF16)
    full = lambda a: pl.BlockSpec(a.shape, lambda g_, i: (0,) * a.ndim, pipeline_mode=pl.Buffered(1))
    return pl.pallas_call(
        functools.partial(_proj_kernel, splits=splits),
        grid=(G, L // tm),
        in_specs=[pl.BlockSpec((1, tm, D), lambda g_, i: (g_, i, 0)),
                  pl.BlockSpec((1, 1, D), lambda g_, i: (g_, 0, 0)),
                  pl.BlockSpec((1, 1, D), lambda g_, i: (g_, 0, 0)),
                  full(g), full(w), full(q_gain), full(k_gain), full(ones)],
        out_specs=[pl.BlockSpec((1, tm, n), lambda g_, i: (g_, i, 0)) for n in widths],
        out_shape=[jax.ShapeDtypeStruct((G, L, n), dt) for n, dt in zip(widths, dtypes)],
        compiler_params=_cparams("parallel", "parallel"),
        name="branch_projections",
    )(xs, shift, scale, g, w, q_gain, k_gain, ones)


def _halo_specs(width, n_tiles):
    per = SEQ_TILE // HALO
    last = n_tiles * per - 1
    return [pl.BlockSpec((1, SEQ_TILE, width), lambda g_, i: (g_, i, 0)),
            pl.BlockSpec((1, HALO, width), lambda g_, i: (g_, jnp.maximum(i * per - 1, 0), 0)),
            pl.BlockSpec((1, HALO, width), lambda g_, i: (g_, jnp.minimum((i + 1) * per, last), 0))]


def _neighbours(u, prev_ref, next_ref):
    g_ = pl.program_id(0)
    i = pl.program_id(1)
    n_tiles = pl.num_programs(1)
    has_prev = jnp.logical_and(g_ > 0, i > 0)
    has_next = jnp.logical_and(g_ > 0, i < n_tiles - 1)
    row = lax.broadcasted_iota(jnp.int32, u.shape, 0)
    halo_p = jnp.where(has_prev, prev_ref[0, HALO - 1:HALO, :], 0.0)
    halo_n = jnp.where(has_next, next_ref[0, 0:1, :], 0.0)
    prev = jnp.where(row == 0, halo_p, pltpu.roll(u, 1, 0))
    nxt = jnp.where(row == SEQ_TILE - 1, halo_n, pltpu.roll(u, SEQ_TILE - 1, 0))
    return prev, nxt


def _rope(z, cos, sin_signed):
    lane = lax.broadcasted_iota(jnp.int32, z.shape, 1)
    partner = jnp.where(lane % 32 < 16, pltpu.roll(z, WIDTH - 16, 1), pltpu.roll(z, 16, 1))
    return z * cos + partner * sin_signed


def _rw_feat_kernel(u_ref, up_ref, un_ref, cos_ref, sin_ref, mu_ref, w0_ref, w2_ref, a0_ref, a2_ref,
                    g2_ref, kk_ref, ka_ref, ones_ref,
                    r_o, v_o, kkn_o, g_o, ld0_o, ld1_o, k0_o, k1_o, b0_o, b1_o):
    u = u_ref[0]
    prev, nxt = _neighbours(u, up_ref, un_ref)
    u = u + mu_ref[0:1, :] * (prev - u) + mu_ref[1:2, :] * (nxt - u)
    r = u[:, 0:WIDTH]
    k = u[:, WIDTH:2 * WIDTH]
    v = u[:, 2 * WIDTH:3 * WIDTH]
    o = 3 * WIDTH
    lw = u[:, o:o + 2 * RW_LORA]
    la = u[:, o + 2 * RW_LORA:o + 4 * RW_LORA]
    lg = u[:, o + 4 * RW_LORA:o + 4 * RW_LORA + RW_GATE_LORA]
    is_lat = pl.program_id(0) > 0
    cos = jnp.where(is_lat, cos_ref[...], 1.0)
    sin = jnp.where(is_lat, sin_ref[...], 0.0)
    r = _rope(r, cos, sin)
    k = _rope(k, cos, sin)
    g = _dot(_sigmoid(lg), g2_ref[...])
    kk = k * kk_ref[...]
    nrm = jnp.sqrt(_dot_split(kk * kk, ones_ref[...]))
    kk = kk / jnp.maximum(nrm, 1e-12)
    r_o[0] = r
    v_o[0] = v
    kkn_o[0] = kk
    g_o[0] = g
    tanh_lw = jnp.tanh(lw)
    for d, (ld_o, k_o, b_o) in enumerate(((ld0_o, k0_o, b0_o), (ld1_o, k1_o, b1_o))):
        zw = w0_ref[d:d + 1, :] + _dot(tanh_lw[:, d * RW_LORA:(d + 1) * RW_LORA], w2_ref[d])
        softplus = jnp.maximum(-zw, 0.0) + jnp.log(1.0 + jnp.exp(-jnp.abs(zw)))
        ld_o[0] = -jnp.exp(-softplus - 0.5)
        iclr = _sigmoid(a0_ref[d:d + 1, :] + _dot(la[:, d * RW_LORA:(d + 1) * RW_LORA], a2_ref[d]))
        k_o[0] = k * (1.0 + (iclr - 1.0) * ka_ref[...])
        b_o[0] = kk * iclr


def rwkv_features(u_rw, cos, sin, mu, w0, w2, a0, a2, g2, k_k, k_a, ones):
    G, L, W_IN = u_rw.shape
    n_tiles = L // SEQ_TILE
    full = lambda a: pl.BlockSpec(a.shape, lambda g_, i: (0,) * a.ndim)
    tab = pl.BlockSpec((SEQ_TILE, WIDTH), lambda g_, i: (i, 0))
    params = (mu, w0, w2, a0, a2, g2, k_k, k_a, ones)
    out = jax.ShapeDtypeStruct((G, L, WIDTH), F32)
    return pl.pallas_call(
        _rw_feat_kernel,
        grid=(G, n_tiles),
        in_specs=_halo_specs(W_IN, n_tiles) + [tab, tab] + [full(p) for p in params],
        out_specs=[pl.BlockSpec((1, SEQ_TILE, WIDTH), lambda g_, i: (g_, i, 0))] * 10,
        out_shape=[out] * 10,
        compiler_params=_cparams("parallel", "parallel"),
        name="rwkv_features",
    )(u_rw, u_rw, u_rw, cos, sin, *params)


def _rw_chunk_operands(r, ld, k, v, kk, b, incl, reverse):
    C = CHUNK
    ld_hi = ld.astype(BF16)
    ld_lo = (ld - ld_hi.astype(F32)).astype(BF16)
    tri = jnp.where(incl, 1.0, 0.0).astype(BF16)
    cum = (jnp.dot(tri, ld_hi, preferred_element_type=F32)
           + jnp.dot(tri, ld_lo, preferred_element_type=F32))
    tot = cum[0:1, :] if reverse else cum[C - 1:C, :]
    e_neg = jnp.exp(-cum)
    e_rem = jnp.exp(tot - cum)
    return dict(
        a_t=(-kk * jnp.exp(cum - ld)).astype(BF16),
        r_t=(r * jnp.exp(cum)).astype(BF16),
        b_t=(b * e_neg).astype(BF16),
        k_t=(k * e_neg).astype(BF16),
        b_p=(b * e_rem).astype(BF16),
        k_p=(k * e_rem).astype(BF16),
        p_c=jnp.exp(tot),
        v=v.astype(BF16))


def _rw_scan_kernel(*refs):
    in_refs, y_refs, s_ref = refs[:12], refs[12:14], refs[14]
    C = CHUNK

    @pl.when(pl.program_id(1) == 0)
    def _():
        s_ref[...] = jnp.zeros_like(s_ref)

    row = lax.broadcasted_iota(jnp.int32, (C, C), 0)
    col = lax.broadcasted_iota(jnp.int32, (C, C), 1)
    eye = row == col
    chains = []
    for d in range(2):
        incl, strict = (row <= col, row < col) if d == 1 else (row >= col, row > col)
        for order in range(SCAN_CHUNKS):
            sub = SCAN_CHUNKS - 1 - order if d == 1 else order
            rs = slice(sub * C, (sub + 1) * C)
            ops = _rw_chunk_operands(*(ref[0, rs, :] for ref in in_refs[6 * d:6 * d + 6]), incl, d == 1)
            for h in range(HEADS):
                sl = slice(h * HEAD_DIM, (h + 1) * HEAD_DIM)
                ch = {name: val[:, sl] for name, val in ops.items()}
                ch.update(d=d, h=h, sl=sl, rs=rs, order=order, incl=incl, strict=strict)
                chains.append(ch)

    each = lambda fn: [fn(ch) for ch in chains]
    both = lambda fn, xs: [fn(ch, x) for ch, x in zip(chains, xs)]
    ar = each(lambda ch: jnp.concatenate([ch["a_t"], ch["r_t"]], axis=0))
    bk = each(lambda ch: jnp.concatenate([ch["b_t"], ch["k_t"]], axis=0))
    g = [_dot_nt(x, y) for x, y in zip(ar, bk)]
    a_ab = both(lambda ch, x: jnp.where(ch["strict"], x[:C, :C], 0.0), g)
    a_rb = both(lambda ch, x: jnp.where(ch["incl"], x[C:, :C], 0.0), g)
    a_k = both(lambda ch, x: jnp.concatenate([jnp.where(ch["strict"], x[:C, C:], 0.0),
                                              jnp.where(ch["incl"], x[C:, C:], 0.0)], axis=0), g)
    av = both(lambda ch, a: _dot(a, ch["v"]), a_k)
    t_inv = [jnp.where(eye, 1.0, a) for a in a_ab]
    pw = [_dot(a, a) for a in a_ab]
    for _ in range(int(math.log2(C)) - 2):
        nxt = [_dot(jnp.concatenate([p, t], axis=0), p) for p, t in zip(pw, t_inv)]
        t_inv = [t + x[C:] for t, x in zip(t_inv, nxt)]
        pw = [x[:C] for x in nxt]
    t_inv = [t + _dot(t, p) for t, p in zip(t_inv, pw)]
    w = [_dot(t, jnp.concatenate([ch["a_t"], x[:C].astype(BF16)], axis=1)) for ch, t, x in zip(chains, t_inv, av)]
    qy = [_dot(a, x) for a, x in zip(a_rb, w)]
    q_p = [ch["r_t"].astype(F32) + x[:, :HEAD_DIM] for ch, x in zip(chains, qy)]
    y_p = [x[C:] + z[:, HEAD_DIM:] for x, z in zip(av, qy)]
    mn = both(lambda ch, x: _dot_tn(x, ch["b_p"]), w)
    m_mat = both(lambda ch, x: jnp.where(eye, ch["p_c"], 0.0) + x[:HEAD_DIM], mn)
    n_mat = both(lambda ch, x: x[HEAD_DIM:] + _dot_tn(ch["v"], ch["k_p"]), mn)
    state = {(d, h): s_ref[d, h] for d in range(2) for h in range(HEADS)}
    for order in range(SCAN_CHUNKS):
        now = [i for i, ch in enumerate(chains) if ch["order"] == order]
        s0 = [state[chains[i]["d"], chains[i]["h"]] for i in now]
        y = [_dot_nt(q_p[i], s) + y_p[i] for i, s in zip(now, s0)]
        s1 = [_dot(s, m_mat[i]) + n_mat[i] for i, s in zip(now, s0)]
        for i, y_c, s_c in zip(now, y, s1):
            ch = chains[i]
            y_refs[ch["d"]][0, ch["rs"], ch["sl"]] = y_c
            state[ch["d"], ch["h"]] = s_c
    for (d, h), s_c in state.items():
        s_ref[d, h] = s_c


def rwkv_scan(r, v, kk, ld0, k0, b0, ld1, k1, b1, *, batch):
    G, L, _ = r.shape
    step = SCAN_CHUNKS * CHUNK
    lc = L // batch
    ns_ctx = lc // step
    ns_lat = L // step
    n_steps = ns_ctx + ns_lat

    def idx(reverse, bi, s):
        in_ctx = s < ns_ctx
        c_ctx = ns_ctx - 1 - s if reverse else s
        c_lat = n_steps - 1 - s if reverse else s - ns_ctx
        return (jnp.where(in_ctx, 0, bi + 1), jnp.where(in_ctx, bi * ns_ctx + c_ctx, c_lat), 0)

    fwd = pl.BlockSpec((1, step, WIDTH), functools.partial(idx, False))
    rev = pl.BlockSpec((1, step, WIDTH), functools.partial(idx, True))
    out = jax.ShapeDtypeStruct((G, L, WIDTH), F32)
    return pl.pallas_call(
        _rw_scan_kernel,
        grid=(batch, n_steps),
        in_specs=[fwd] * 6 + [rev] * 6,
        out_specs=[fwd, rev],
        out_shape=[out, out],
        scratch_shapes=[pltpu.VMEM((2, HEADS, HEAD_DIM, HEAD_DIM), F32)],
        compiler_params=_cparams("parallel", "arbitrary"),
        name="rwkv_scan",
    )(r, ld0, k0, v, kk, b0, r, ld1, k1, v, kk, b1)


def _rw_readout(y0, y1, r, k0, k1, v, g, r_k, gn_g, gn_b, ones):
    y = y0 + y1
    inv_n = 1.0 / HEAD_DIM
    mu = _dot_split(y, ones) * inv_n
    yc = y - mu
    var = _dot_split(yc * yc, ones) * inv_n
    yn = yc * lax.rsqrt(var + RW_GN_EPS) * gn_g + gn_b
    bonus = _dot_split(r * (k0 + k1) * r_k, ones) * v
    return (yn + bonus) * g


def _short_conv(u, w_ref, b_ref):
    n = u.shape[0]
    row = lax.broadcasted_iota(jnp.int32, u.shape, 0)
    prev = jnp.where(row == 0, 0.0, pltpu.roll(u, 1, 0))
    nxt = jnp.where(row == n - 1, 0.0, pltpu.roll(u, n - 1, 0))
    return b_ref[...] + w_ref[0:1, :] * prev + w_ref[1:2, :] * u + w_ref[2:3, :] * nxt


@functools.lru_cache(maxsize=None)
def _dft_tables(n):
    nn = 2 * n
    nf = n + 256
    kf = np.arange(nf, dtype=np.int64)
    t = np.arange(n, dtype=np.int64)
    ang = 2.0 * np.pi * ((kf[:, None] * t[None, :]) % nn).astype(np.float64) / nn
    valid = (kf <= n)[:, None]
    cosm = np.where(valid, np.cos(ang), 0.0)
    sinm = np.where(valid, np.sin(ang), 0.0)
    ck = np.where((kf == 0) | (kf == n), 1.0, 2.0)[:, None] / nn
    fwd = np.stack([cosm, -sinm])
    inv = np.stack([(ck * cosm).T, (-ck * sinm).T])
    return fwd, inv, nf


def _hy_conv_kernel(*refs, z_is_raw):
    nb = HY_SEQS
    z_refs, x_refs = refs[:nb], refs[nb:2 * nb]
    (zw_ref, zb_ref, xw_ref, xb_ref, fwd_ref, inv_ref, hre_ref, him_ref, skip_ref,
     o_ref, acc_ref, zs_ref) = refs[2 * nb:]
    f = pl.program_id(1)
    z_of = lambda i: _short_conv(z_refs[i][0], zw_ref, zb_ref) if z_is_raw else z_refs[i][0]

    @pl.when(f == 0)
    def _():
        acc_ref[...] = jnp.zeros_like(acc_ref)
        for i in range(nb):
            zs_ref[:, i * HY_WIDTH:(i + 1) * HY_WIDTH] = z_of(i).astype(BF16)

    zb = zs_ref[...]
    zre = jnp.dot(fwd_ref[0], zb, preferred_element_type=F32)
    zim = jnp.dot(fwd_ref[1], zb, preferred_element_type=F32)
    hre = jnp.concatenate([hre_ref[0]] * nb, axis=1)
    him = jnp.concatenate([him_ref[0]] * nb, axis=1)
    yre = zre * hre - zim * him
    yim = zre * him + zim * hre
    acc_ref[...] += (jnp.dot(inv_ref[0], yre.astype(BF16), preferred_element_type=F32)
                     + jnp.dot(inv_ref[1], yim.astype(BF16), preferred_element_type=F32))

    @pl.when(f == pl.num_programs(1) - 1)
    def _():
        for i in range(nb):
            conv = acc_ref[:, i * HY_WIDTH:(i + 1) * HY_WIDTH]
            o_ref[i] = _short_conv(x_refs[i][0], xw_ref, xb_ref) * (conv + skip_ref[0] * z_of(i))


def hyena_long_conv(z_arr, z_blk, x_arr, x_blk, conv_w, conv_b, hre, him, skip, order, fwd, inv, *, n_seq, n, tf):
    nf = fwd.shape[1]
    nb = HY_SEQS
    zo, zc = z_blk
    xo, xc = x_blk
    seq = lambda off, cb, i: pl.BlockSpec((1, n, HY_WIDTH), lambda s, f: (s * nb + i + off, 0, cb))
    taps = lambda cb: [pl.BlockSpec((conv_w.shape[0], HY_WIDTH), lambda s, f: (0, cb)),
                       pl.BlockSpec((1, HY_WIDTH), lambda s, f: (0, cb))]
    z_is_raw = order == 0
    return pl.pallas_call(
        functools.partial(_hy_conv_kernel, z_is_raw=z_is_raw),
        grid=(n_seq // nb, nf // tf),
        in_specs=[seq(zo, zc, i) for i in range(nb)] + [seq(xo, xc, i) for i in range(nb)]
                 + taps(zc if z_is_raw else 0) + taps(xc)
                 + [pl.BlockSpec((2, tf, n), lambda s, f: (0, f, 0)),
                    pl.BlockSpec((2, n, tf), lambda s, f: (0, 0, f)),
                    pl.BlockSpec((1, tf, HY_WIDTH), lambda s, f: (order, f, 0)),
                    pl.BlockSpec((1, tf, HY_WIDTH), lambda s, f: (order, f, 0)),
                    pl.BlockSpec((1, 1, HY_WIDTH), lambda s, f: (order, 0, 0))],
        out_specs=pl.BlockSpec((nb, n, HY_WIDTH), lambda s, f: (s, 0, 0)),
        out_shape=jax.ShapeDtypeStruct((n_seq, n, HY_WIDTH), F32),
        scratch_shapes=[pltpu.VMEM((n, nb * HY_WIDTH), F32), pltpu.VMEM((n, nb * HY_WIDTH), BF16)],
        compiler_params=_cparams("parallel", "arbitrary"),
        name=f"hyena_long_conv_n{n}_o{order}",
    )(*([z_arr] * nb), *([x_arr] * nb), conv_w, conv_b, conv_w, conv_b, fwd, inv, hre, him, skip)


def hyena_filter_spectrum(n, w1, b1, w2, b2, w3, freq, fwd):
    hp = lax.Precision.HIGHEST
    pos = jnp.arange(n, dtype=F32)
    t = pos / max(n - 1, 1)
    bands = (HY_EMB_DIM - 1) // 2
    fr = jnp.linspace(1e-4, bands - 1, bands, dtype=F32)
    ang = (2 * math.pi / n) * pos[:, None] * fr[None, :]
    z = jnp.concatenate([t[:, None], jnp.cos(ang), -jnp.sin(ang)], axis=-1)
    h = jnp.sin(freq * (jnp.dot(z, w1, precision=hp) + b1))
    h = jnp.sin(freq * (jnp.dot(h, w2, precision=hp) + b2))
    h = matmul(h, w3, tn=w3.shape[1], split=True, name="hyena_filter_out")
    h = h.reshape(n, HY_ORDER, 2, HY_WIDTH)
    max_decay = math.log(HY_DECAY_TARGET) / HY_FAST_DECAY
    min_decay = math.log(HY_DECAY_TARGET) / HY_SLOW_DECAY
    deltas = jnp.abs(jnp.linspace(min_decay, max_decay, HY_WIDTH, dtype=F32))
    h = h * jnp.exp(-t[:, None] * deltas)[:, None, None, :]
    h_fwd = h[:, :, 0]
    h_bwd = h[:, :, 1] * (pos > 0).astype(F32)[:, None, None]
    l1 = jnp.sum(jnp.abs(h_fwd), axis=0) + jnp.sum(jnp.abs(h_bwd), axis=0)
    even = ((h_fwd + h_bwd) / l1).reshape(n, HY_ORDER * HY_WIDTH)
    odd = ((h_fwd - h_bwd) / l1).reshape(n, HY_ORDER * HY_WIDTH)
    nf = fwd.shape[1]
    hre = matmul(fwd[0], even, tn=HY_ORDER * HY_WIDTH, name="hyena_filter_dft_re")
    him = matmul(fwd[1], odd, tn=HY_ORDER * HY_WIDTH, name="hyena_filter_dft_im")
    to_ofc = lambda a: jnp.moveaxis(a.reshape(nf, HY_ORDER, HY_WIDTH), 1, 0)
    return to_ofc(hre), to_ofc(him)


def _na_lat_kernel(q_ref, k_ref, v_ref, kc_ref, vc_ref, bias_ref, o_ref, *, rows, wr):
    kc = kc_ref[0]
    vc = vc_ref[0]
    jobs = []
    for rr in range(NA_ROWS):
        i = pl.program_id(1) * NA_ROWS + rr
        r0 = jnp.clip(i - wr // 2, 0, rows - wr)
        off = r0 - i + (NA_WIN_ROWS - 1)
        start = pl.multiple_of(r0 * GRID_W, GRID_W)
        qs = slice(rr * GRID_W, (rr + 1) * GRID_W)
        q = q_ref[0, qs, :]
        kw = k_ref[0, pl.ds(start, wr * GRID_W), :]
        vw = v_ref[0, pl.ds(start, wr * GRID_W), :]
        for h in range(HEADS):
            sl = slice(h * HEAD_DIM, (h + 1) * HEAD_DIM)
            jobs.append(dict(q=q[:, sl], kw=kw[:, sl], vw=vw[:, sl], h=h, off=off, qs=qs, sl=sl))
    s_loc = [_dot_nt(j["q"], j["kw"]) + bias_ref[j["h"], j["off"]] for j in jobs]
    s_ctx = [_dot_nt(j["q"], kc[:, j["sl"]]) for j in jobs]
    m = [jnp.maximum(jnp.max(a, axis=-1, keepdims=True), jnp.max(b, axis=-1, keepdims=True))
         for a, b in zip(s_loc, s_ctx)]
    p_loc = [jnp.exp(a - mm) for a, mm in zip(s_loc, m)]
    p_ctx = [jnp.exp(b - mm) for b, mm in zip(s_ctx, m)]
    den = [jnp.sum(a, axis=-1, keepdims=True) + jnp.sum(b, axis=-1, keepdims=True) for a, b in zip(p_loc, p_ctx)]
    o = [_dot(a, j["vw"]) + _dot(b, vc[:, j["sl"]]) for a, b, j in zip(p_loc, p_ctx, jobs)]
    for j, oo, dd in zip(jobs, o, den):
        o_ref[0, j["qs"], j["sl"]] = oo / dd


def _na_bias_table(rpb, rows, wr):
    hp = lax.Precision.HIGHEST
    offs = np.arange(NA_WIN_ROWS) - (NA_WIN_ROWS - 1)
    dr = offs[:, None] + np.arange(wr)[None, :] + NA_WIN_ROWS - 1
    c_ar = np.arange(GRID_W)
    dc = np.clip(c_ar[None, :] - c_ar[:, None] + NA_WIN_COLS - 1, 0, 2 * NA_WIN_COLS - 2)
    c0 = np.clip(c_ar - NA_WIN_COLS // 2, 0, GRID_W - NA_WIN_COLS)
    in_win = (c_ar[None, :] >= c0[:, None]) & (c_ar[None, :] < c0[:, None] + NA_WIN_COLS)
    oh_r = jnp.asarray(dr[..., None] == np.arange(2 * NA_WIN_ROWS - 1), dtype=F32)
    oh_c = jnp.asarray(dc[..., None] == np.arange(2 * NA_WIN_COLS - 1), dtype=F32)
    by_row = jnp.einsum("hab,ora->horb", rpb.astype(F32), oh_r, precision=hp)
    bias = jnp.einsum("horb,qkb->hoqrk", by_row, oh_c, precision=hp)
    bias = jnp.where(in_win[None, None, :, None, :], bias, NEG_BIG)
    return bias.reshape(HEADS, NA_WIN_ROWS, GRID_W, wr * GRID_W)


def na_latent(qn, kn, vb, bias, *, batch):
    G, L, _ = qn.shape
    lc = L // batch
    rows = L // GRID_W
    wr = min(NA_WIN_ROWS, rows)
    assert rows % NA_ROWS == 0
    qspec = pl.BlockSpec((1, NA_ROWS * GRID_W, WIDTH), lambda b_, i: (b_ + 1, i, 0))
    seq = pl.BlockSpec((1, L, WIDTH), lambda b_, i: (b_ + 1, 0, 0))
    ctx = pl.BlockSpec((1, lc, WIDTH), lambda b_, i: (0, b_, 0))
    return pl.pallas_call(
        functools.partial(_na_lat_kernel, rows=rows, wr=wr),
        grid=(batch, rows // NA_ROWS),
        in_specs=[qspec, seq, seq, ctx, ctx, pl.BlockSpec(bias.shape, lambda b_, i: (0, 0, 0, 0))],
        out_specs=pl.BlockSpec((1, NA_ROWS * GRID_W, WIDTH), lambda b_, i: (b_, i, 0)),
        out_shape=jax.ShapeDtypeStruct((batch, L, WIDTH), F32),
        compiler_params=_cparams("parallel", "arbitrary"),
        name="na_latent",
    )(qn, kn, vb, kn, vb, bias)


def _na_ctx_kernel(q_ref, k_ref, v_ref, o_ref):
    q = q_ref[0]
    k = k_ref[0]
    v = v_ref[0]
    for h in range(HEADS):
        sl = slice(h * HEAD_DIM, (h + 1) * HEAD_DIM)
        s = _dot_nt(q[:, sl], k[:, sl])
        p = jnp.exp(s - jnp.max(s, axis=-1, keepdims=True))
        o_ref[0, :, sl] = _dot(p, v[:, sl]) / jnp.sum(p, axis=-1, keepdims=True)


def na_context(qn, kn, vb, *, batch):
    G, L, _ = qn.shape
    lc = L // batch
    spec = pl.BlockSpec((1, lc, WIDTH), lambda b_: (0, b_, 0))
    return pl.pallas_call(
        _na_ctx_kernel,
        grid=(batch,),
        in_specs=[spec] * 3,
        out_specs=spec,
        out_shape=jax.ShapeDtypeStruct((1, L, WIDTH), F32),
        compiler_params=_cparams("parallel"),
        name="na_context",
    )(qn, kn, vb)


def _merge_kernel(x_ref, gl_ref, y0_ref, y1_ref, r_ref, k0_ref, k1_ref, v_ref, g_ref, hy_ref, na_ref, g1_ref,
                  rk_ref, gg_ref, gb_ref, ones_ref, wr_ref, wh_ref, wn_ref, wo_ref, o_ref):
    D = x_ref.shape[-1]
    gl = gl_ref[0].astype(F32)
    y_rw = _rw_readout(y0_ref[0], y1_ref[0], r_ref[0], k0_ref[0], k1_ref[0], v_ref[0], g_ref[0],
                       rk_ref[...], gg_ref[...], gb_ref[...], ones_ref[...])
    m = (_sigmoid(gl[:, 0:D]) * _dot(y_rw, wr_ref[...])
         + _sigmoid(gl[:, D:2 * D]) * _dot(hy_ref[0], wh_ref[...])
         + _sigmoid(gl[:, 2 * D:3 * D]) * _dot(na_ref[0], wn_ref[...]))
    o_ref[0] = x_ref[0] + g1_ref[0] * _dot(m, wo_ref[...])


def merge_branches(xs, u_gate, rw_parts, y_hy, y_na, gate1, rw_params, w_rw, w_hy, w_na, w_o, *, g_off, tm=512):
    G, L, D = xs.shape
    tok = lambda width: pl.BlockSpec((1, tm, width), lambda g_, i: (g_ + g_off, i, 0))
    own = lambda width: pl.BlockSpec((1, tm, width), lambda g_, i: (g_, i, 0))
    full = lambda a: pl.BlockSpec(a.shape, lambda g_, i: (0,) * a.ndim, pipeline_mode=pl.Buffered(1))
    consts = (*rw_params, w_rw, w_hy, w_na, w_o)
    return pl.pallas_call(
        _merge_kernel,
        grid=(G - g_off, L // tm),
        in_specs=[tok(D), tok(3 * D)] + [tok(WIDTH)] * len(rw_parts) + [own(HY_WIDTH), own(WIDTH),
                  pl.BlockSpec((1, 1, D), lambda g_, i: (g_ + g_off, 0, 0))] + [full(c) for c in consts],
        out_specs=pl.BlockSpec((1, tm, D), lambda g_, i: (g_, i, 0)),
        out_shape=jax.ShapeDtypeStruct((G - g_off, L, D), F32),
        compiler_params=_cparams("parallel", "parallel"),
        name="merge_branches",
    )(xs, u_gate, *rw_parts, y_hy, y_na, gate1, *consts)


def _ffn_kernel(x_ref, sh_ref, sc_ref, g_ref, g2_ref, w1_ref, w3_ref, w2_ref, o_ref, h_ref, acc_ref):
    f = pl.program_id(2)

    @pl.when(f == 0)
    def _():
        h_ref[...] = _norm_mod(x_ref[0], g_ref[...], sh_ref[0], sc_ref[0]).astype(BF16)
        acc_ref[...] = jnp.zeros_like(acc_ref)

    h = h_ref[...]
    a = jnp.dot(h, w1_ref[...], preferred_element_type=F32)
    b = jnp.dot(h, w3_ref[...], preferred_element_type=F32)
    acc_ref[...] += _dot(_silu(a) * b, w2_ref[...])

    @pl.when(f == pl.num_programs(2) - 1)
    def _():
        o_ref[0] = x_ref[0] + g2_ref[0] * acc_ref[...]


def ffn_dense(xs, shift, scale, g, gate2, w1, w3, w2, *, tm=1024, tf):
    G, L, D = xs.shape
    FF = w1.shape[1]
    assert L % tm == 0 and FF % tf == 0
    mod = pl.BlockSpec((1, 1, D), lambda g_, i, f: (g_, 0, 0))
    return pl.pallas_call(
        _ffn_kernel,
        grid=(G, L // tm, FF // tf),
        in_specs=[pl.BlockSpec((1, tm, D), lambda g_, i, f: (g_, i, 0)), mod, mod,
                  pl.BlockSpec((1, D), lambda g_, i, f: (0, 0)), mod,
                  pl.BlockSpec((D, tf), lambda g_, i, f: (0, f)),
                  pl.BlockSpec((D, tf), lambda g_, i, f: (0, f)),
                  pl.BlockSpec((tf, D), lambda g_, i, f: (f, 0))],
        out_specs=pl.BlockSpec((1, tm, D), lambda g_, i, f: (g_, i, 0)),
        out_shape=jax.ShapeDtypeStruct((G, L, D), F32),
        scratch_shapes=[pltpu.VMEM((tm, D), BF16), pltpu.VMEM((tm, D), F32)],
        compiler_params=_cparams("parallel", "parallel", "arbitrary"),
        name="ffn_dense",
    )(xs, shift, scale, g, gate2, w1, w3, w2)


def _moe_route_kernel(x_ref, sh_ref, sc_ref, g_ref, rt_ref, h_o, gates_o, rank_o, rank_t_o, start_o, cnt_o,
                      cnt_row, cnt_col):
    i = pl.program_id(1)

    @pl.when(i == 0)
    def _():
        cnt_row[...] = jnp.zeros_like(cnt_row)
        cnt_col[...] = jnp.zeros_like(cnt_col)

    h = _norm_mod(x_ref[0], g_ref[...], sh_ref[0], sc_ref[0])
    h_o[0] = h.astype(BF16)
    rt = rt_ref[...]
    hh = h.astype(BF16)
    hl = (h - hh.astype(F32)).astype(BF16)
    rh = rt.astype(BF16)
    rl = (rt - rh.astype(F32)).astype(BF16)
    logits = (jnp.dot(hh, rh, preferred_element_type=F32) + jnp.dot(hl, rh, preferred_element_type=F32)
              + jnp.dot(hh, rl, preferred_element_type=F32))
    lane = lax.broadcasted_iota(jnp.int32, logits.shape, 1)
    logits = jnp.where(lane < N_EXPERTS, logits, NEG_BIG)
    m1 = jnp.max(logits, axis=-1, keepdims=True)
    i1 = jnp.min(jnp.where(logits == m1, lane, LANES), axis=-1, keepdims=True)
    rest = jnp.where(lane == i1, NEG_BIG, logits)
    m2 = jnp.max(rest, axis=-1, keepdims=True)
    i2 = jnp.min(jnp.where(rest == m2, lane, LANES), axis=-1, keepdims=True)
    e2 = jnp.exp(m2 - m1)
    gates = jnp.where(lane == i1, 1.0 / (1.0 + e2), 0.0) + jnp.where(lane == i2, e2 / (1.0 + e2), 0.0)
    gates_o[0] = gates

    tm = gates.shape[0]
    sel = gates > 0.0
    m_tok = jnp.where(sel, 1.0, 0.0)
    m_exp = m_tok.T[:2 * HALO]
    row = lax.broadcasted_iota(jnp.int32, (tm, tm), 0)
    col = lax.broadcasted_iota(jnp.int32, (tm, tm), 1)
    before = jnp.where(col < row, 1.0, 0.0).astype(BF16)
    after = jnp.where(row < col, 1.0, 0.0).astype(BF16)
    rank = jnp.dot(before, m_tok.astype(BF16), preferred_element_type=F32) + cnt_row[...]
    rank_t = jnp.dot(m_exp.astype(BF16), after, preferred_element_type=F32)[:HALO]
    rank_t = rank_t + jnp.tile(cnt_col[...], (1, tm // LANES))
    start_o[0, 0] = cnt_row[...]
    rank_o[0] = jnp.where(sel, rank, NEG_BIG)
    rank_t_o[0] = jnp.where(m_exp[:HALO] > 0.0, rank_t, NEG_BIG)
    cnt_row[...] += jnp.sum(m_tok, axis=0, keepdims=True)
    cnt_col[...] += jnp.sum(m_exp[:HALO], axis=1, keepdims=True)
    cnt_o[0] = cnt_row[...]


def _moe_expert_kernel(cnt_ref, bnd_ref, h_ref, gates_ref, rank_ref, rank_t_ref, x_ref, g2_ref, w1_ref, w3_ref, w2_ref,
                       o_ref, hs_ref, ys_ref, *, rb, tile):
    gi = pl.program_id(0)
    e = pl.program_id(1)
    f = pl.program_id(2)
    n_e = cnt_ref[gi * pl.num_programs(1) + e]
    n_half = (n_e + rb // 2 - 1) // (rb // 2)
    n_full = n_half // 2
    tm = h_ref.shape[1]
    n_e_all = pl.num_programs(1)

    def for_tiles(start, size, fn):
        for i in range(tm // tile):
            lo = bnd_ref[(gi * (tm // tile + 1) + i) * n_e_all + e]
            hi = bnd_ref[(gi * (tm // tile + 1) + i + 1) * n_e_all + e]

            @pl.when(jnp.logical_and(hi > start, lo < start + size))
            def _():
                fn(slice(i * tile, (i + 1) * tile))

    def for_blocks(fn):
        def body(j, carry):
            fn(pl.multiple_of(j * rb, rb), rb)
            return carry

        lax.fori_loop(0, n_full, body, 0)

        @pl.when(n_half % 2 == 1)
        def _():
            fn(pl.multiple_of(n_full * rb, rb), rb // 2)

    @pl.when(jnp.logical_and(e == 0, f == 0))
    def _():
        o_ref[...] = jnp.zeros_like(o_ref)

    @pl.when(f == 0)
    def _():
        rank_t = rank_t_ref[0, pl.ds(e, 1), :]

        def gather(start, size):
            hs_ref[pl.ds(start, size), :] = jnp.zeros((size, hs_ref.shape[1]), BF16)
            sub = lax.broadcasted_iota(jnp.int32, (size, tile), 0).astype(F32) + start.astype(F32)

            def from_tile(ts):
                pick = jnp.where(rank_t[:, ts] == sub, 1.0, 0.0).astype(BF16)
                got = jnp.dot(pick, h_ref[0, ts, :], preferred_element_type=F32)
                hs_ref[pl.ds(start, size), :] += got.astype(BF16)

            for_tiles(start, size, from_tile)

        for_blocks(gather)

    def expert_block(start, size):
        hs = hs_ref[pl.ds(start, size), :]
        a = jnp.dot(hs, w1_ref[0], preferred_element_type=F32)
        b = jnp.dot(hs, w3_ref[0], preferred_element_type=F32)
        return _dot(_silu(a) * b, w2_ref[0])

    @pl.when(f == 0)
    def _():
        def first(start, size):
            ys_ref[pl.ds(start, size), :] = expert_block(start, size)

        for_blocks(first)

    @pl.when(f > 0)
    def _():
        def more(start, size):
            ys_ref[pl.ds(start, size), :] += expert_block(start, size)

        for_blocks(more)

    last_f = f == pl.num_programs(2) - 1

    @pl.when(last_f)
    def _():
        lane = lax.broadcasted_iota(jnp.int32, (tm, LANES), 1)
        gate = jnp.sum(jnp.where(lane == e, gates_ref[0], 0.0), axis=-1, keepdims=True)
        rank = jnp.sum(jnp.where(lane == e, rank_ref[0], 0.0), axis=-1, keepdims=True)

        def scatter(start, size):
            slot = lax.broadcasted_iota(jnp.int32, (tile, size), 1).astype(F32) + start.astype(F32)
            ys = ys_ref[pl.ds(start, size), :].astype(BF16)

            def to_tile(ts):
                put = jnp.where(rank[ts] == slot, gate[ts], 0.0)
                o_ref[0, ts, :] += _dot(put, ys)

            for_tiles(start, size, to_tile)

        for_blocks(scatter)

    @pl.when(jnp.logical_and(e == pl.num_programs(1) - 1, last_f))
    def _():
        o_ref[0] = x_ref[0] + g2_ref[0] * o_ref[0]


def moe_ffn(xs, shift, scale, g, gate2, router, w1, w3, w2, *, g_off, tm_route=512, rb=256, tf):
    G, L, D = xs.shape
    E, _, FF = w1.shape
    n_g = G - g_off
    mod = pl.BlockSpec((1, 1, D), lambda g_, i: (g_ + g_off, 0, 0))
    tok = lambda width: pl.BlockSpec((1, tm_route, width), lambda g_, i: (g_, i, 0))
    n_tiles = L // tm_route
    h, gates, rank, rank_t, starts, cnt = pl.pallas_call(
        _moe_route_kernel,
        grid=(n_g, L // tm_route),
        in_specs=[pl.BlockSpec((1, tm_route, D), lambda g_, i: (g_ + g_off, i, 0)), mod, mod,
                  pl.BlockSpec((1, D), lambda g_, i: (0, 0)),
                  pl.BlockSpec((D, LANES), lambda g_, i: (0, 0))],
        out_specs=[tok(D), tok(LANES), tok(LANES),
                   pl.BlockSpec((1, HALO, tm_route), lambda g_, i: (g_, 0, i)),
                   pl.BlockSpec((1, 1, 1, LANES), lambda g_, i: (g_, i, 0, 0)),
                   pl.BlockSpec((1, 1, LANES), lambda g_, i: (g_, 0, 0))],
        out_shape=[jax.ShapeDtypeStruct((n_g, L, D), BF16),
                   jax.ShapeDtypeStruct((n_g, L, LANES), F32),
                   jax.ShapeDtypeStruct((n_g, L, LANES), F32),
                   jax.ShapeDtypeStruct((n_g, HALO, L), F32),
                   jax.ShapeDtypeStruct((n_g, n_tiles, 1, LANES), F32),
                   jax.ShapeDtypeStruct((n_g, 1, LANES), F32)],
        scratch_shapes=[pltpu.VMEM((1, LANES), F32), pltpu.VMEM((HALO, LANES), F32)],
        compiler_params=_cparams("parallel", "arbitrary"),
        name="moe_route",
    )(xs, shift, scale, g, router)
    counts = cnt[:, 0, :E].astype(jnp.int32)
    bounds = jnp.concatenate([starts[:, :, 0, :E].astype(jnp.int32), counts[:, None, :]], axis=1)

    once = pl.Buffered(1)
    grp = lambda width: pl.BlockSpec((1, L, width), lambda g_, e, f, c, b: (g_, 0, 0), pipeline_mode=once)
    return pl.pallas_call(
        functools.partial(_moe_expert_kernel, rb=rb, tile=tm_route),
        grid_spec=pltpu.PrefetchScalarGridSpec(
            num_scalar_prefetch=2,
            grid=(n_g, E, FF // tf),
            in_specs=[grp(D), grp(LANES), grp(LANES),
                      pl.BlockSpec((1, HALO, L), lambda g_, e, f, c, b: (g_, 0, 0), pipeline_mode=once),
                      pl.BlockSpec((1, L, D), lambda g_, e, f, c, b: (g_ + g_off, 0, 0), pipeline_mode=once),
                      pl.BlockSpec((1, 1, D), lambda g_, e, f, c, b: (g_ + g_off, 0, 0)),
                      pl.BlockSpec((1, D, tf), lambda g_, e, f, c, b: (e, 0, f)),
                      pl.BlockSpec((1, D, tf), lambda g_, e, f, c, b: (e, 0, f)),
                      pl.BlockSpec((1, tf, D), lambda g_, e, f, c, b: (e, f, 0))],
            out_specs=pl.BlockSpec((1, L, D), lambda g_, e, f, c, b: (g_, 0, 0), pipeline_mode=once),
            scratch_shapes=[pltpu.VMEM((L + rb, D), BF16), pltpu.VMEM((L + rb, D), F32)]),
        out_shape=jax.ShapeDtypeStruct((n_g, L, D), F32),
        compiler_params=pltpu.CompilerParams(dimension_semantics=("parallel", "arbitrary", "arbitrary"),
                                             vmem_limit_bytes=60 * 1024 * 1024),
        name="moe_experts",
    )(counts.reshape(-1), bounds.reshape(-1), h, gates, rank, rank_t, xs, gate2, w1, w3, w2)


def _rope_tables(n):
    half = HEAD_DIM // 2
    nf = half // 2
    t = np.arange(n)
    inv = ROPE_BASE ** (-np.arange(nf, dtype=np.float64) / nf)
    d = np.arange(HEAD_DIM)
    pos = np.where(d[None, :] < half, (t // GRID_W)[:, None], (t % GRID_W)[:, None]).astype(np.float64)
    ang = pos * inv[d % nf][None, :]
    sign = np.where(d % half < nf, -1.0, 1.0)
    cos = np.tile(np.cos(ang), (1, HEADS))
    sin = np.tile(np.sin(ang) * sign[None, :], (1, HEADS))
    return jnp.asarray(cos, dtype=F32), jnp.asarray(sin, dtype=F32)


def kernel(x, c, ctx, c_ctx, ada_w, ada_b, norm1_g, norm2_g, w_in, rw_shift, rw_w0, rw_w2, rw_a0, rw_a2, rw_g2, rw_kk, rw_ka, rw_rk, rw_gn_g, rw_gn_b, hy_conv_w, hy_conv_b, hy_f_w1, hy_f_b1, hy_f_w2, hy_f_b2, hy_f_w3, hy_f_freq, hy_skip, na_q_gain, na_k_gain, na_rpb, w_br_rw, w_br_hy, w_br_na, w_out, ff_w1, ff_w3, ff_w2, moe_router, moe_w1, moe_w3, moe_w2):
    B, L, D = x.shape
    lc = ctx.shape[1]
    depth = ada_w.shape[0]
    assert B * lc == L and lc == SEQ_TILE and L % GRID_W == 0
    G = B + 1
    rows = L // GRID_W
    wr = min(NA_WIN_ROWS, rows)
    rw_in = 3 * WIDTH + 4 * RW_LORA + RW_GATE_LORA
    hy_in = (HY_ORDER + 1) * HY_WIDTH
    na_in = 3 * WIDTH
    splits = (rw_in, rw_in + hy_in, rw_in + hy_in + na_in)

    xs = jnp.concatenate([ctx.reshape(1, L, D), x], axis=0)
    act = jnp.concatenate([c_ctx[None, :], c], axis=0)
    act = act * _sigmoid(act)
    act = jnp.pad(act, ((0, (-G) % 16), (0, 0)))
    ones = _head_ones()
    cos, sin = _rope_tables(L)
    fwd_lat, inv_lat, _ = _dft_tables(L)
    fwd_ctx, inv_ctx, _ = _dft_tables(lc)
    fwd_lat, inv_lat, fwd_ctx, inv_ctx = (jnp.asarray(a, dtype=F32).astype(BF16)
                                          for a in (fwd_lat, inv_lat, fwd_ctx, inv_ctx))
    row2 = lambda a: a.reshape(1, -1)

    for li in range(depth):
        last = li == depth - 1
        mods = matmul(act, ada_w[li], tn=1024, split=True, name="adaln")[:G] + ada_b[li]
        sh1, sc1, g1, sh2, sc2, g2 = (m.reshape(G, 1, D) for m in jnp.split(mods, 6, axis=-1))

        u_rw, u_hy, qn, kn, vb, u_gate = branch_projections(
            xs, sh1, sc1, row2(norm1_g[li]), w_in[li].astype(BF16), row2(jnp.tile(na_q_gain[li], HEADS)),
            row2(jnp.tile(na_k_gain[li], HEADS)), ones, splits)

        r, v, kk, g, ld0, ld1, k0, k1, b0, b1 = rwkv_features(
            u_rw, cos, sin, rw_shift[li], rw_w0[li], rw_w2[li].astype(BF16), rw_a0[li], rw_a2[li].astype(BF16),
            rw_g2[li].astype(BF16), row2(rw_kk[li]), row2(rw_ka[li]), ones)
        y0, y1 = rwkv_scan(r, v, kk, ld0, k0, b0, ld1, k1, b1, batch=B)
        rw_parts = (y0, y1, r, k0, k1, v, g)
        rw_params = (row2(rw_rk[li]), row2(rw_gn_g[li]), row2(rw_gn_b[li]), ones)

        f_args = (hy_f_w1[li], hy_f_b1[li], hy_f_w2[li], hy_f_b2[li], hy_f_w3[li], hy_f_freq[li])
        taps = (hy_conv_w[li], row2(hy_conv_b[li]))
        skip = hy_skip[li].reshape(HY_ORDER, 1, HY_WIDTH)
        hre, him = hyena_filter_spectrum(L, *f_args, fwd_lat)
        conv = functools.partial(hyena_long_conv, skip=skip, fwd=fwd_lat, inv=inv_lat, n_seq=B, n=L, tf=384)
        z1 = conv(u_hy, (1, 0), u_hy, (1, 1), *taps, hre, him, order=0)
        y_hy = conv(z1, (0, 0), u_hy, (1, 2), *taps, hre, him, order=1)
        if not last:
            hy_c = u_hy[0].reshape(B, lc, hy_in)
            hre_c, him_c = hyena_filter_spectrum(lc, *f_args, fwd_ctx)
            conv_c = functools.partial(hyena_long_conv, skip=skip, fwd=fwd_ctx, inv=inv_ctx, n_seq=B, n=lc,
                                       tf=fwd_ctx.shape[1])
            z1c = conv_c(hy_c, (0, 0), hy_c, (0, 1), *taps, hre_c, him_c, order=0)
            y_hy_c = conv_c(z1c, (0, 0), hy_c, (0, 2), *taps, hre_c, him_c, order=1)
            y_hy = jnp.concatenate([y_hy_c.reshape(1, L, HY_WIDTH), y_hy], axis=0)

        bias = _na_bias_table(na_rpb[li], rows, wr)
        y_na = na_latent(qn, kn, vb, bias, batch=B)
        if not last:
            y_na = jnp.concatenate([na_context(qn, kn, vb, batch=B), y_na], axis=0)

        g_off = 1 if last else 0
        xs = merge_branches(xs, u_gate, rw_parts, y_hy, y_na, g1, rw_params, w_br_rw[li].astype(BF16),
                            w_br_hy[li].astype(BF16), w_br_na[li].astype(BF16), w_out[li].astype(BF16), g_off=g_off)
        if last:
            sh2, sc2, g2 = sh2[1:], sc2[1:], g2[1:]

        n2 = row2(norm2_g[li])
        if li % 2 == 0:
            j = li // 2
            xs = ffn_dense(xs, sh2, sc2, n2, g2, ff_w1[j].astype(BF16), ff_w3[j].astype(BF16), ff_w2[j].astype(BF16),
                           tf=ff_w1.shape[2] // 2)
        else:
            j = li // 2
            router = jnp.pad(moe_router[j], ((0, 0), (0, LANES - N_EXPERTS)))
            xs = moe_ffn(xs, sh2, sc2, n2, g2, router, moe_w1[j].astype(BF16), moe_w3[j].astype(BF16),
                         moe_w2[j].astype(BF16), g_off=0, tf=moe_w1.shape[3] // 4)
    return xs if depth == 0 else xs[-B:]
```

```python
import functools
import math

import numpy as np
import jax
import jax.numpy as jnp
from jax import lax
from jax.experimental import pallas as pl
from jax.experimental.pallas import tpu as pltpu

F32 = jnp.float32
BF16 = jnp.bfloat16

NORM_EPS = 1e-6
ROPE_BASE = 10000.0
GRID_W = 64
HEADS = 6
HEAD_DIM = 64
WIDTH = HEADS * HEAD_DIM
RW_LORA = 64
RW_GATE_LORA = 128
RW_GN_EPS = 64e-5
HY_WIDTH = 256
HY_ORDER = 2
HY_EMB_DIM = 33
HY_DECAY_TARGET = 1e-2
HY_FAST_DECAY = 0.3
HY_SLOW_DECAY = 1.5
NA_WIN_ROWS = 8
NA_WIN_COLS = 16
N_EXPERTS = 8
LANES = 128
SEQ_TILE = 256
HALO = 8
CHUNK = 64
SCAN_CHUNKS = 4
HY_SEQS = 2
NA_ROWS = 8
HY_FREQ_TILE = 384
VMEM_LIMIT = 56 * 1024 * 1024
MOE_VMEM_LIMIT = 60 * 1024 * 1024
NEG_BIG = -1e30


def _cparams(*sem):
    return pltpu.CompilerParams(dimension_semantics=sem, vmem_limit_bytes=VMEM_LIMIT)


def _dot(a, b):
    return jnp.dot(a.astype(BF16), b.astype(BF16), preferred_element_type=F32)


def _dot_nt(a, b):
    return lax.dot_general(a.astype(BF16), b.astype(BF16), (((1,), (1,)), ((), ())),
                           preferred_element_type=F32)


def _dot_tn(a, b):
    return lax.dot_general(a.astype(BF16), b.astype(BF16), (((0,), (0,)), ((), ())),
                           preferred_element_type=F32)


def _dot_split(a, b_exact):
    hi = a.astype(BF16)
    lo = (a - hi.astype(F32)).astype(BF16)
    return (jnp.dot(hi, b_exact, preferred_element_type=F32)
            + jnp.dot(lo, b_exact, preferred_element_type=F32))


def _sigmoid(x):
    return 1.0 / (1.0 + jnp.exp(-x))


def _silu(x):
    return x * _sigmoid(x)


def _head_ones():
    h = np.arange(WIDTH) // HEAD_DIM
    return jnp.asarray((h[:, None] == h[None, :]).astype(np.float32), dtype=BF16)


def _mm_kernel(x_ref, w_ref, o_ref, *, split):
    x = x_ref[...]
    w = w_ref[...]
    if split:
        xh = x.astype(BF16)
        xl = (x - xh.astype(F32)).astype(BF16)
        wh = w.astype(BF16)
        wl = (w - wh.astype(F32)).astype(BF16)
        o_ref[...] = (jnp.dot(xh, wh, preferred_element_type=F32)
                      + jnp.dot(xl, wh, preferred_element_type=F32)
                      + jnp.dot(xh, wl, preferred_element_type=F32))
    else:
        o_ref[...] = jnp.dot(x.astype(BF16), w.astype(BF16), preferred_element_type=F32)


def matmul(x, w, *, tn, split=False, name="matmul"):
    m, k = x.shape
    n = w.shape[1]
    assert n % tn == 0
    return pl.pallas_call(
        functools.partial(_mm_kernel, split=split),
        grid=(n // tn,),
        in_specs=[pl.BlockSpec((m, k), lambda j: (0, 0)),
                  pl.BlockSpec((k, tn), lambda j: (0, j))],
        out_specs=pl.BlockSpec((m, tn), lambda j: (0, j)),
        out_shape=jax.ShapeDtypeStruct((m, n), F32),
        compiler_params=_cparams("parallel"),
        name=name,
    )(x, w)


def _norm_mod(x, g, shift, scale):
    ms = jnp.mean(x * x, axis=-1, keepdims=True)
    h = x * lax.rsqrt(ms + NORM_EPS) * g
    return h * (1.0 + scale) + shift


def _proj_kernel(x_ref, sh_ref, sc_ref, g_ref, w_ref, qg_ref, kg_ref, ones_ref,
                 rw_o, hy_o, q_o, k_o, v_o, gate_o, *, splits):
    h = _norm_mod(x_ref[0], g_ref[...], sh_ref[0], sc_ref[0]).astype(BF16)
    s_rw, s_hy, s_na = splits
    proj = lambda lo, hi: jnp.dot(h, w_ref[:, lo:hi], preferred_element_type=F32)
    rw_o[0] = proj(0, s_rw)
    hy_o[0] = proj(s_rw, s_hy)
    gate_o[0] = proj(s_na, w_ref.shape[1]).astype(BF16)
    ones = ones_ref[...]
    inv_n = 1.0 / HEAD_DIM
    q = proj(s_hy, s_hy + WIDTH)
    k = proj(s_hy + WIDTH, s_hy + 2 * WIDTH)
    qn = q * lax.rsqrt(_dot_split(q * q, ones) * inv_n + NORM_EPS) * qg_ref[...]
    kn = k * lax.rsqrt(_dot_split(k * k, ones) * inv_n + NORM_EPS) * kg_ref[...]
    q_o[0] = (qn * (HEAD_DIM ** -0.5)).astype(BF16)
    k_o[0] = kn.astype(BF16)
    v_o[0] = proj(s_hy + 2 * WIDTH, s_na).astype(BF16)


def branch_projections(xs, shift, scale, g, w, q_gain, k_gain, ones, splits, *, tm=512):
    G, L, D = xs.shape
    s_rw, s_hy, s_na = splits
    widths = (s_rw, s_hy - s_rw, WIDTH, WIDTH, WIDTH, w.shape[1] - s_na)
    dtypes = (F32, F32, BF16, BF16, BF16, BF16)
    full = lambda a: pl.BlockSpec(a.shape, lambda g_, i: (0,) * a.ndim, pipeline_mode=pl.Buffered(1))
    return pl.pallas_call(
        functools.partial(_proj_kernel, splits=splits),
        grid=(G, L // tm),
        in_specs=[pl.BlockSpec((1, tm, D), lambda g_, i: (g_, i, 0)),
                  pl.BlockSpec((1, 1, D), lambda g_, i: (g_, 0, 0)),
                  pl.BlockSpec((1, 1, D), lambda g_, i: (g_, 0, 0)),
                  full(g), full(w), full(q_gain), full(k_gain), full(ones)],
        out_specs=[pl.BlockSpec((1, tm, n), lambda g_, i: (g_, i, 0)) for n in widths],
        out_shape=[jax.ShapeDtypeStruct((G, L, n), dt) for n, dt in zip(widths, dtypes)],
        compiler_params=_cparams("parallel", "parallel"),
        name="branch_projections",
    )(xs, shift, scale, g, w, q_gain, k_gain, ones)


def _halo_specs(width, n_tiles):
    per = SEQ_TILE // HALO
    last = n_tiles * per - 1
    return [pl.BlockSpec((1, SEQ_TILE, width), lambda g_, i: (g_, i, 0)),
            pl.BlockSpec((1, HALO, width), lambda g_, i: (g_, jnp.maximum(i * per - 1, 0), 0)),
            pl.BlockSpec((1, HALO, width), lambda g_, i: (g_, jnp.minimum((i + 1) * per, last), 0))]


def _neighbours(u, prev_ref, next_ref):
    g_ = pl.program_id(0)
    i = pl.program_id(1)
    n_tiles = pl.num_programs(1)
    has_prev = jnp.logical_and(g_ > 0, i > 0)
    has_next = jnp.logical_and(g_ > 0, i < n_tiles - 1)
    row = lax.broadcasted_iota(jnp.int32, u.shape, 0)
    halo_p = jnp.where(has_prev, prev_ref[0, HALO - 1:HALO, :], 0.0)
    halo_n = jnp.where(has_next, next_ref[0, 0:1, :], 0.0)
    prev = jnp.where(row == 0, halo_p, pltpu.roll(u, 1, 0))
    nxt = jnp.where(row == SEQ_TILE - 1, halo_n, pltpu.roll(u, SEQ_TILE - 1, 0))
    return prev, nxt


def _rope(z, cos, sin_signed):
    lane = lax.broadcasted_iota(jnp.int32, z.shape, 1)
    partner = jnp.where(lane % 32 < 16, pltpu.roll(z, WIDTH - 16, 1), pltpu.roll(z, 16, 1))
    return z * cos + partner * sin_signed


def _rw_feat_kernel(u_ref, up_ref, un_ref, cos_ref, sin_ref, mu_ref, w0_ref, w2_ref, a0_ref, a2_ref,
                    g2_ref, kk_ref, ka_ref, ones_ref,
                    r_o, v_o, kkn_o, g_o, ld0_o, ld1_o, k0_o, k1_o, b0_o, b1_o):
    u = u_ref[0]
    prev, nxt = _neighbours(u, up_ref, un_ref)
    u = u + mu_ref[0:1, :] * (prev - u) + mu_ref[1:2, :] * (nxt - u)
    r = u[:, 0:WIDTH]
    k = u[:, WIDTH:2 * WIDTH]
    v = u[:, 2 * WIDTH:3 * WIDTH]
    o = 3 * WIDTH
    lw = u[:, o:o + 2 * RW_LORA]
    la = u[:, o + 2 * RW_LORA:o + 4 * RW_LORA]
    lg = u[:, o + 4 * RW_LORA:o + 4 * RW_LORA + RW_GATE_LORA]
    is_lat = pl.program_id(0) > 0
    cos = jnp.where(is_lat, cos_ref[...], 1.0)
    sin = jnp.where(is_lat, sin_ref[...], 0.0)
    r = _rope(r, cos, sin)
    k = _rope(k, cos, sin)
    g = _dot(_sigmoid(lg), g2_ref[...])
    kk = k * kk_ref[...]
    nrm = jnp.sqrt(_dot_split(kk * kk, ones_ref[...]))
    kk = kk / jnp.maximum(nrm, 1e-12)
    r_o[0] = r
    v_o[0] = v
    kkn_o[0] = kk
    g_o[0] = g
    tanh_lw = jnp.tanh(lw)
    for d, (ld_o, k_o, b_o) in enumerate(((ld0_o, k0_o, b0_o), (ld1_o, k1_o, b1_o))):
        zw = w0_ref[d:d + 1, :] + _dot(tanh_lw[:, d * RW_LORA:(d + 1) * RW_LORA], w2_ref[d])
        softplus = jnp.maximum(-zw, 0.0) + jnp.log(1.0 + jnp.exp(-jnp.abs(zw)))
        ld_o[0] = -jnp.exp(-softplus - 0.5)
        iclr = _sigmoid(a0_ref[d:d + 1, :] + _dot(la[:, d * RW_LORA:(d + 1) * RW_LORA], a2_ref[d]))
        k_o[0] = k * (1.0 + (iclr - 1.0) * ka_ref[...])
        b_o[0] = kk * iclr


def rwkv_features(u_rw, cos, sin, mu, w0, w2, a0, a2, g2, k_k, k_a, ones):
    G, L, W_IN = u_rw.shape
    n_tiles = L // SEQ_TILE
    full = lambda a: pl.BlockSpec(a.shape, lambda g_, i: (0,) * a.ndim)
    tab = pl.BlockSpec((SEQ_TILE, WIDTH), lambda g_, i: (i, 0))
    params = (mu, w0, w2, a0, a2, g2, k_k, k_a, ones)
    out = jax.ShapeDtypeStruct((G, L, WIDTH), F32)
    return pl.pallas_call(
        _rw_feat_kernel,
        grid=(G, n_tiles),
        in_specs=_halo_specs(W_IN, n_tiles) + [tab, tab] + [full(p) for p in params],
        out_specs=[pl.BlockSpec((1, SEQ_TILE, WIDTH), lambda g_, i: (g_, i, 0))] * 10,
        out_shape=[out] * 10,
        compiler_params=_cparams("parallel", "parallel"),
        name="rwkv_features",
    )(u_rw, u_rw, u_rw, cos, sin, *params)


def _rw_chunk_operands(r, ld, k, v, kk, b, incl, reverse):
    C = CHUNK
    ld_hi = ld.astype(BF16)
    ld_lo = (ld - ld_hi.astype(F32)).astype(BF16)
    tri = jnp.where(incl, 1.0, 0.0).astype(BF16)
    cum = (jnp.dot(tri, ld_hi, preferred_element_type=F32)
           + jnp.dot(tri, ld_lo, preferred_element_type=F32))
    tot = cum[0:1, :] if reverse else cum[C - 1:C, :]
    e_neg = jnp.exp(-cum)
    e_rem = jnp.exp(tot - cum)
    return dict(
        a_t=(-kk * jnp.exp(cum - ld)).astype(BF16),
        r_t=(r * jnp.exp(cum)).astype(BF16),
        b_t=(b * e_neg).astype(BF16),
        k_t=(k * e_neg).astype(BF16),
        b_p=(b * e_rem).astype(BF16),
        k_p=(k * e_rem).astype(BF16),
        p_c=jnp.exp(tot),
        v=v.astype(BF16))


def _rw_scan_kernel(*refs):
    in_refs, y_refs, s_ref = refs[:12], refs[12:14], refs[14]
    C = CHUNK

    @pl.when(pl.program_id(1) == 0)
    def _():
        s_ref[...] = jnp.zeros_like(s_ref)

    row = lax.broadcasted_iota(jnp.int32, (C, C), 0)
    col = lax.broadcasted_iota(jnp.int32, (C, C), 1)
    eye = row == col
    chains = []
    for d in range(2):
        incl, strict = (row <= col, row < col) if d == 1 else (row >= col, row > col)
        for order in range(SCAN_CHUNKS):
            sub = SCAN_CHUNKS - 1 - order if d == 1 else order
            rs = slice(sub * C, (sub + 1) * C)
            ops = _rw_chunk_operands(*(ref[0, rs, :] for ref in in_refs[6 * d:6 * d + 6]), incl, d == 1)
            for h in range(HEADS):
                sl = slice(h * HEAD_DIM, (h + 1) * HEAD_DIM)
                ch = {name: val[:, sl] for name, val in ops.items()}
                ch.update(d=d, h=h, sl=sl, rs=rs, order=order, incl=incl, strict=strict)
                chains.append(ch)

    each = lambda fn: [fn(ch) for ch in chains]
    both = lambda fn, xs: [fn(ch, x) for ch, x in zip(chains, xs)]
    ar = each(lambda ch: jnp.concatenate([ch["a_t"], ch["r_t"]], axis=0))
    bk = each(lambda ch: jnp.concatenate([ch["b_t"], ch["k_t"]], axis=0))
    g = [_dot_nt(x, y) for x, y in zip(ar, bk)]
    a_ab = both(lambda ch, x: jnp.where(ch["strict"], x[:C, :C], 0.0), g)
    a_rb = both(lambda ch, x: jnp.where(ch["incl"], x[C:, :C], 0.0), g)
    a_k = both(lambda ch, x: jnp.concatenate([jnp.where(ch["strict"], x[:C, C:], 0.0),
                                              jnp.where(ch["incl"], x[C:, C:], 0.0)], axis=0), g)
    av = both(lambda ch, a: _dot(a, ch["v"]), a_k)
    t_inv = [jnp.where(eye, 1.0, a) for a in a_ab]
    pw = [_dot(a, a) for a in a_ab]
    for _ in range(int(math.log2(C)) - 2):
        nxt = [_dot(jnp.concatenate([p, t], axis=0), p) for p, t in zip(pw, t_inv)]
        t_inv = [t + x[C:] for t, x in zip(t_inv, nxt)]
        pw = [x[:C] for x in nxt]
    t_inv = [t + _dot(t, p) for t, p in zip(t_inv, pw)]
    w = [_dot(t, jnp.concatenate([ch["a_t"], x[:C].astype(BF16)], axis=1)) for ch, t, x in zip(chains, t_inv, av)]
    qy = [_dot(a, x) for a, x in zip(a_rb, w)]
    q_p = [ch["r_t"].astype(F32) + x[:, :HEAD_DIM] for ch, x in zip(chains, qy)]
    y_p = [x[C:] + z[:, HEAD_DIM:] for x, z in zip(av, qy)]
    mn = both(lambda ch, x: _dot_tn(x, ch["b_p"]), w)
    m_mat = both(lambda ch, x: jnp.where(eye, ch["p_c"], 0.0) + x[:HEAD_DIM], mn)
    n_mat = both(lambda ch, x: x[HEAD_DIM:] + _dot_tn(ch["v"], ch["k_p"]), mn)
    state = {(d, h): s_ref[d, h] for d in range(2) for h in range(HEADS)}
    for order in range(SCAN_CHUNKS):
        now = [i for i, ch in enumerate(chains) if ch["order"] == order]
        s0 = [state[chains[i]["d"], chains[i]["h"]] for i in now]
        y = [_dot_nt(q_p[i], s) + y_p[i] for i, s in zip(now, s0)]
        s1 = [_dot(s, m_mat[i]) + n_mat[i] for i, s in zip(now, s0)]
        for i, y_c, s_c in zip(now, y, s1):
            ch = chains[i]
            y_refs[ch["d"]][0, ch["rs"], ch["sl"]] = y_c
            state[ch["d"], ch["h"]] = s_c
    for (d, h), s_c in state.items():
        s_ref[d, h] = s_c


def rwkv_scan(r, v, kk, ld0, k0, b0, ld1, k1, b1, *, batch):
    G, L, _ = r.shape
    step = SCAN_CHUNKS * CHUNK
    lc = L // batch
    ns_ctx = lc // step
    ns_lat = L // step
    n_steps = ns_ctx + ns_lat

    def idx(reverse, bi, s):
        in_ctx = s < ns_ctx
        c_ctx = ns_ctx - 1 - s if reverse else s
        c_lat = n_steps - 1 - s if reverse else s - ns_ctx
        return (jnp.where(in_ctx, 0, bi + 1), jnp.where(in_ctx, bi * ns_ctx + c_ctx, c_lat), 0)

    fwd = pl.BlockSpec((1, step, WIDTH), functools.partial(idx, False))
    rev = pl.BlockSpec((1, step, WIDTH), functools.partial(idx, True))
    out = jax.ShapeDtypeStruct((G, L, WIDTH), F32)
    return pl.pallas_call(
        _rw_scan_kernel,
        grid=(batch, n_steps),
        in_specs=[fwd] * 6 + [rev] * 6,
        out_specs=[fwd, rev],
        out_shape=[out, out],
        scratch_shapes=[pltpu.VMEM((2, HEADS, HEAD_DIM, HEAD_DIM), F32)],
        compiler_params=_cparams("parallel", "arbitrary"),
        name="rwkv_scan",
    )(r, ld0, k0, v, kk, b0, r, ld1, k1, v, kk, b1)


def _rw_readout(y0, y1, r, k0, k1, v, g, r_k, gn_g, gn_b, ones):
    y = y0 + y1
    inv_n = 1.0 / HEAD_DIM
    mu = _dot_split(y, ones) * inv_n
    yc = y - mu
    var = _dot_split(yc * yc, ones) * inv_n
    yn = yc * lax.rsqrt(var + RW_GN_EPS) * gn_g + gn_b
    bonus = _dot_split(r * (k0 + k1) * r_k, ones) * v
    return (yn + bonus) * g


def _short_conv(u, w_ref, b_ref):
    n = u.shape[0]
    row = lax.broadcasted_iota(jnp.int32, u.shape, 0)
    prev = jnp.where(row == 0, 0.0, pltpu.roll(u, 1, 0))
    nxt = jnp.where(row == n - 1, 0.0, pltpu.roll(u, n - 1, 0))
    return b_ref[...] + w_ref[0:1, :] * prev + w_ref[1:2, :] * u + w_ref[2:3, :] * nxt


@functools.lru_cache(maxsize=None)
def _dft_tables(n):
    nn = 2 * n
    nf = n + 256
    kf = np.arange(nf, dtype=np.int64)
    t = np.arange(n, dtype=np.int64)
    ang = 2.0 * np.pi * ((kf[:, None] * t[None, :]) % nn).astype(np.float64) / nn
    valid = (kf <= n)[:, None]
    cosm = np.where(valid, np.cos(ang), 0.0)
    sinm = np.where(valid, np.sin(ang), 0.0)
    ck = np.where((kf == 0) | (kf == n), 1.0, 2.0)[:, None] / nn
    fwd = np.stack([cosm, -sinm])
    inv = np.stack([(ck * cosm).T, (-ck * sinm).T])
    return fwd, inv, nf


def _hy_conv_kernel(*refs, z_is_raw):
    nb = HY_SEQS
    z_refs, x_refs = refs[:nb], refs[nb:2 * nb]
    (zw_ref, zb_ref, xw_ref, xb_ref, fwd_ref, inv_ref, hre_ref, him_ref, skip_ref,
     o_ref, acc_ref, zs_ref) = refs[2 * nb:]
    f = pl.program_id(1)
    z_of = lambda i: _short_conv(z_refs[i][0], zw_ref, zb_ref) if z_is_raw else z_refs[i][0]

    @pl.when(f == 0)
    def _():
        acc_ref[...] = jnp.zeros_like(acc_ref)
        for i in range(nb):
            zs_ref[:, i * HY_WIDTH:(i + 1) * HY_WIDTH] = z_of(i).astype(BF16)

    zb = zs_ref[...]
    zre = jnp.dot(fwd_ref[0], zb, preferred_element_type=F32)
    zim = jnp.dot(fwd_ref[1], zb, preferred_element_type=F32)
    hre = jnp.concatenate([hre_ref[0]] * nb, axis=1)
    him = jnp.concatenate([him_ref[0]] * nb, axis=1)
    yre = zre * hre - zim * him
    yim = zre * him + zim * hre
    acc_ref[...] += (jnp.dot(inv_ref[0], yre.astype(BF16), preferred_element_type=F32)
                     + jnp.dot(inv_ref[1], yim.astype(BF16), preferred_element_type=F32))

    @pl.when(f == pl.num_programs(1) - 1)
    def _():
        for i in range(nb):
            conv = acc_ref[:, i * HY_WIDTH:(i + 1) * HY_WIDTH]
            o_ref[i] = _short_conv(x_refs[i][0], xw_ref, xb_ref) * (conv + skip_ref[0] * z_of(i))


def hyena_long_conv(z_arr, z_blk, x_arr, x_blk, conv_w, conv_b, hre, him, skip, order, fwd, inv, *, n_seq, n, tf):
    nf = fwd.shape[1]
    nb = HY_SEQS
    zo, zc = z_blk
    xo, xc = x_blk
    seq = lambda off, cb, i: pl.BlockSpec((1, n, HY_WIDTH), lambda s, f: (s * nb + i + off, 0, cb))
    taps = lambda cb: [pl.BlockSpec((conv_w.shape[0], HY_WIDTH), lambda s, f: (0, cb)),
                       pl.BlockSpec((1, HY_WIDTH), lambda s, f: (0, cb))]
    z_is_raw = order == 0
    return pl.pallas_call(
        functools.partial(_hy_conv_kernel, z_is_raw=z_is_raw),
        grid=(n_seq // nb, nf // tf),
        in_specs=[seq(zo, zc, i) for i in range(nb)] + [seq(xo, xc, i) for i in range(nb)]
                 + taps(zc if z_is_raw else 0) + taps(xc)
                 + [pl.BlockSpec((2, tf, n), lambda s, f: (0, f, 0)),
                    pl.BlockSpec((2, n, tf), lambda s, f: (0, 0, f)),
                    pl.BlockSpec((1, tf, HY_WIDTH), lambda s, f: (order, f, 0)),
                    pl.BlockSpec((1, tf, HY_WIDTH), lambda s, f: (order, f, 0)),
                    pl.BlockSpec((1, 1, HY_WIDTH), lambda s, f: (order, 0, 0))],
        out_specs=pl.BlockSpec((nb, n, HY_WIDTH), lambda s, f: (s, 0, 0)),
        out_shape=jax.ShapeDtypeStruct((n_seq, n, HY_WIDTH), F32),
        scratch_shapes=[pltpu.VMEM((n, nb * HY_WIDTH), F32), pltpu.VMEM((n, nb * HY_WIDTH), BF16)],
        compiler_params=_cparams("parallel", "arbitrary"),
        name=f"hyena_long_conv_n{n}_o{order}",
    )(*([z_arr] * nb), *([x_arr] * nb), conv_w, conv_b, conv_w, conv_b, fwd, inv, hre, him, skip)


def hyena_filter_spectrum(n, w1, b1, w2, b2, w3, freq, fwd):
    hp = lax.Precision.HIGHEST
    pos = jnp.arange(n, dtype=F32)
    t = pos / max(n - 1, 1)
    bands = (HY_EMB_DIM - 1) // 2
    fr = jnp.linspace(1e-4, bands - 1, bands, dtype=F32)
    ang = (2 * math.pi / n) * pos[:, None] * fr[None, :]
    z = jnp.concatenate([t[:, None], jnp.cos(ang), -jnp.sin(ang)], axis=-1)
    h = jnp.sin(freq * (jnp.dot(z, w1, precision=hp) + b1))
    h = jnp.sin(freq * (jnp.dot(h, w2, precision=hp) + b2))
    h = matmul(h, w3, tn=w3.shape[1], split=True, name="hyena_filter_out")
    h = h.reshape(n, HY_ORDER, 2, HY_WIDTH)
    max_decay = math.log(HY_DECAY_TARGET) / HY_FAST_DECAY
    min_decay = math.log(HY_DECAY_TARGET) / HY_SLOW_DECAY
    deltas = jnp.abs(jnp.linspace(min_decay, max_decay, HY_WIDTH, dtype=F32))
    h = h * jnp.exp(-t[:, None] * deltas)[:, None, None, :]
    h_fwd = h[:, :, 0]
    h_bwd = h[:, :, 1] * (pos > 0).astype(F32)[:, None, None]
    l1 = jnp.sum(jnp.abs(h_fwd), axis=0) + jnp.sum(jnp.abs(h_bwd), axis=0)
    even = ((h_fwd + h_bwd) / l1).reshape(n, HY_ORDER * HY_WIDTH)
    odd = ((h_fwd - h_bwd) / l1).reshape(n, HY_ORDER * HY_WIDTH)
    nf = fwd.shape[1]
    hre = matmul(fwd[0], even, tn=HY_ORDER * HY_WIDTH, name="hyena_filter_dft_re")
    him = matmul(fwd[1], odd, tn=HY_ORDER * HY_WIDTH, name="hyena_filter_dft_im")
    to_ofc = lambda a: jnp.moveaxis(a.reshape(nf, HY_ORDER, HY_WIDTH), 1, 0)
    return to_ofc(hre), to_ofc(him)


def _na_lat_kernel(q_ref, k_ref, v_ref, kc_ref, vc_ref, bias_ref, o_ref, *, rows, wr):
    kc = kc_ref[0]
    vc = vc_ref[0]
    jobs = []
    for rr in range(NA_ROWS):
        i = pl.program_id(1) * NA_ROWS + rr
        r0 = jnp.clip(i - wr // 2, 0, rows - wr)
        off = r0 - i + (NA_WIN_ROWS - 1)
        start = pl.multiple_of(r0 * GRID_W, GRID_W)
        qs = slice(rr * GRID_W, (rr + 1) * GRID_W)
        q = q_ref[0, qs, :]
        kw = k_ref[0, pl.ds(start, wr * GRID_W), :]
        vw = v_ref[0, pl.ds(start, wr * GRID_W), :]
        for h in range(HEADS):
            sl = slice(h * HEAD_DIM, (h + 1) * HEAD_DIM)
            jobs.append(dict(q=q[:, sl], kw=kw[:, sl], vw=vw[:, sl], h=h, off=off, qs=qs, sl=sl))
    s_loc = [_dot_nt(j["q"], j["kw"]) + bias_ref[j["h"], j["off"]] for j in jobs]
    s_ctx = [_dot_nt(j["q"], kc[:, j["sl"]]) for j in jobs]
    m = [jnp.maximum(jnp.max(a, axis=-1, keepdims=True), jnp.max(b, axis=-1, keepdims=True))
         for a, b in zip(s_loc, s_ctx)]
    p_loc = [jnp.exp(a - mm) for a, mm in zip(s_loc, m)]
    p_ctx = [jnp.exp(b - mm) for b, mm in zip(s_ctx, m)]
    den = [jnp.sum(a, axis=-1, keepdims=True) + jnp.sum(b, axis=-1, keepdims=True) for a, b in zip(p_loc, p_ctx)]
    o = [_dot(a, j["vw"]) + _dot(b, vc[:, j["sl"]]) for a, b, j in zip(p_loc, p_ctx, jobs)]
    for j, oo, dd in zip(jobs, o, den):
        o_ref[0, j["qs"], j["sl"]] = oo / dd


def _na_bias_table(rpb, rows, wr):
    hp = lax.Precision.HIGHEST
    offs = np.arange(NA_WIN_ROWS) - (NA_WIN_ROWS - 1)
    dr = offs[:, None] + np.arange(wr)[None, :] + NA_WIN_ROWS - 1
    c_ar = np.arange(GRID_W)
    dc = np.clip(c_ar[None, :] - c_ar[:, None] + NA_WIN_COLS - 1, 0, 2 * NA_WIN_COLS - 2)
    c0 = np.clip(c_ar - NA_WIN_COLS // 2, 0, GRID_W - NA_WIN_COLS)
    in_win = (c_ar[None, :] >= c0[:, None]) & (c_ar[None, :] < c0[:, None] + NA_WIN_COLS)
    oh_r = jnp.asarray(dr[..., None] == np.arange(2 * NA_WIN_ROWS - 1), dtype=F32)
    oh_c = jnp.asarray(dc[..., None] == np.arange(2 * NA_WIN_COLS - 1), dtype=F32)
    by_row = jnp.einsum("hab,ora->horb", rpb.astype(F32), oh_r, precision=hp)
    bias = jnp.einsum("horb,qkb->hoqrk", by_row, oh_c, precision=hp)
    bias = jnp.where(in_win[None, None, :, None, :], bias, NEG_BIG)
    return bias.reshape(HEADS, NA_WIN_ROWS, GRID_W, wr * GRID_W)


def na_latent(qn, kn, vb, bias, *, batch):
    G, L, _ = qn.shape
    lc = L // batch
    rows = L // GRID_W
    wr = min(NA_WIN_ROWS, rows)
    assert rows % NA_ROWS == 0
    qspec = pl.BlockSpec((1, NA_ROWS * GRID_W, WIDTH), lambda b_, i: (b_ + 1, i, 0))
    seq = pl.BlockSpec((1, L, WIDTH), lambda b_, i: (b_ + 1, 0, 0))
    ctx = pl.BlockSpec((1, lc, WIDTH), lambda b_, i: (0, b_, 0))
    return pl.pallas_call(
        functools.partial(_na_lat_kernel, rows=rows, wr=wr),
        grid=(batch, rows // NA_ROWS),
        in_specs=[qspec, seq, seq, ctx, ctx, pl.BlockSpec(bias.shape, lambda b_, i: (0, 0, 0, 0))],
        out_specs=pl.BlockSpec((1, NA_ROWS * GRID_W, WIDTH), lambda b_, i: (b_, i, 0)),
        out_shape=jax.ShapeDtypeStruct((batch, L, WIDTH), F32),
        compiler_params=_cparams("parallel", "arbitrary"),
        name="na_latent",
    )(qn, kn, vb, kn, vb, bias)


def _na_ctx_kernel(q_ref, k_ref, v_ref, o_ref):
    q = q_ref[0]
    k = k_ref[0]
    v = v_ref[0]
    for h in range(HEADS):
        sl = slice(h * HEAD_DIM, (h + 1) * HEAD_DIM)
        s = _dot_nt(q[:, sl], k[:, sl])
        p = jnp.exp(s - jnp.max(s, axis=-1, keepdims=True))
        o_ref[0, :, sl] = _dot(p, v[:, sl]) / jnp.sum(p, axis=-1, keepdims=True)


def na_context(qn, kn, vb, *, batch):
    G, L, _ = qn.shape
    lc = L // batch
    spec = pl.BlockSpec((1, lc, WIDTH), lambda b_: (0, b_, 0))
    return pl.pallas_call(
        _na_ctx_kernel,
        grid=(batch,),
        in_specs=[spec] * 3,
        out_specs=spec,
        out_shape=jax.ShapeDtypeStruct((1, L, WIDTH), F32),
        compiler_params=_cparams("parallel"),
        name="na_context",
    )(qn, kn, vb)


def _merge_kernel(x_ref, gl_ref, y0_ref, y1_ref, r_ref, k0_ref, k1_ref, v_ref, g_ref, hy_ref, na_ref, g1_ref,
                  rk_ref, gg_ref, gb_ref, ones_ref, wr_ref, wh_ref, wn_ref, wo_ref, o_ref):
    D = x_ref.shape[-1]
    gl = gl_ref[0].astype(F32)
    y_rw = _rw_readout(y0_ref[0], y1_ref[0], r_ref[0], k0_ref[0], k1_ref[0], v_ref[0], g_ref[0],
                       rk_ref[...], gg_ref[...], gb_ref[...], ones_ref[...])
    m = (_sigmoid(gl[:, 0:D]) * _dot(y_rw, wr_ref[...])
         + _sigmoid(gl[:, D:2 * D]) * _dot(hy_ref[0], wh_ref[...])
         + _sigmoid(gl[:, 2 * D:3 * D]) * _dot(na_ref[0], wn_ref[...]))
    o_ref[0] = x_ref[0] + g1_ref[0] * _dot(m, wo_ref[...])


def merge_branches(xs, u_gate, rw_parts, y_hy, y_na, gate1, rw_params, w_rw, w_hy, w_na, w_o, *, g_off, tm=512):
    G, L, D = xs.shape
    tok = lambda width: pl.BlockSpec((1, tm, width), lambda g_, i: (g_ + g_off, i, 0))
    own = lambda width: pl.BlockSpec((1, tm, width), lambda g_, i: (g_, i, 0))
    full = lambda a: pl.BlockSpec(a.shape, lambda g_, i: (0,) * a.ndim, pipeline_mode=pl.Buffered(1))
    consts = (*rw_params, w_rw, w_hy, w_na, w_o)
    return pl.pallas_call(
        _merge_kernel,
        grid=(G - g_off, L // tm),
        in_specs=[tok(D), tok(3 * D)] + [tok(WIDTH)] * len(rw_parts) + [own(HY_WIDTH), own(WIDTH),
                  pl.BlockSpec((1, 1, D), lambda g_, i: (g_ + g_off, 0, 0))] + [full(c) for c in consts],
        out_specs=pl.BlockSpec((1, tm, D), lambda g_, i: (g_, i, 0)),
        out_shape=jax.ShapeDtypeStruct((G - g_off, L, D), F32),
        compiler_params=_cparams("parallel", "parallel"),
        name="merge_branches",
    )(xs, u_gate, *rw_parts, y_hy, y_na, gate1, *consts)


def _ffn_kernel(x_ref, sh_ref, sc_ref, g_ref, g2_ref, w1_ref, w3_ref, w2_ref, o_ref, h_ref, acc_ref):
    f = pl.program_id(2)

    @pl.when(f == 0)
    def _():
        h_ref[...] = _norm_mod(x_ref[0], g_ref[...], sh_ref[0], sc_ref[0]).astype(BF16)
        acc_ref[...] = jnp.zeros_like(acc_ref)

    h = h_ref[...]
    a = jnp.dot(h, w1_ref[...], preferred_element_type=F32)
    b = jnp.dot(h, w3_ref[...], preferred_element_type=F32)
    acc_ref[...] += _dot(_silu(a) * b, w2_ref[...])

    @pl.when(f == pl.num_programs(2) - 1)
    def _():
        o_ref[0] = x_ref[0] + g2_ref[0] * acc_ref[...]


def ffn_dense(xs, shift, scale, g, gate2, w1, w3, w2, *, tm=1024, tf):
    G, L, D = xs.shape
    FF = w1.shape[1]
    assert L % tm == 0 and FF % tf == 0
    mod = pl.BlockSpec((1, 1, D), lambda g_, i, f: (g_, 0, 0))
    return pl.pallas_call(
        _ffn_kernel,
        grid=(G, L // tm, FF // tf),
        in_specs=[pl.BlockSpec((1, tm, D), lambda g_, i, f: (g_, i, 0)), mod, mod,
                  pl.BlockSpec((1, D), lambda g_, i, f: (0, 0)), mod,
                  pl.BlockSpec((D, tf), lambda g_, i, f: (0, f)),
                  pl.BlockSpec((D, tf), lambda g_, i, f: (0, f)),
                  pl.BlockSpec((tf, D), lambda g_, i, f: (f, 0))],
        out_specs=pl.BlockSpec((1, tm, D), lambda g_, i, f: (g_, i, 0)),
        out_shape=jax.ShapeDtypeStruct((G, L, D), F32),
        scratch_shapes=[pltpu.VMEM((tm, D), BF16), pltpu.VMEM((tm, D), F32)],
        compiler_params=_cparams("parallel", "parallel", "arbitrary"),
        name="ffn_dense",
    )(xs, shift, scale, g, gate2, w1, w3, w2)


def _moe_route_kernel(x_ref, sh_ref, sc_ref, g_ref, rt_ref, h_o, gates_o, rank_o, rank_t_o, start_o, cnt_o,
                      cnt_row, cnt_col):
    i = pl.program_id(1)

    @pl.when(i == 0)
    def _():
        cnt_row[...] = jnp.zeros_like(cnt_row)
        cnt_col[...] = jnp.zeros_like(cnt_col)

    h = _norm_mod(x_ref[0], g_ref[...], sh_ref[0], sc_ref[0])
    h_o[0] = h.astype(BF16)
    rt = rt_ref[...]
    hh = h.astype(BF16)
    hl = (h - hh.astype(F32)).astype(BF16)
    rh = rt.astype(BF16)
    rl = (rt - rh.astype(F32)).astype(BF16)
    logits = (jnp.dot(hh, rh, preferred_element_type=F32) + jnp.dot(hl, rh, preferred_element_type=F32)
              + jnp.dot(hh, rl, preferred_element_type=F32))
    lane = lax.broadcasted_iota(jnp.int32, logits.shape, 1)
    logits = jnp.where(lane < N_EXPERTS, logits, NEG_BIG)
    m1 = jnp.max(logits, axis=-1, keepdims=True)
    i1 = jnp.min(jnp.where(logits == m1, lane, LANES), axis=-1, keepdims=True)
    rest = jnp.where(lane == i1, NEG_BIG, logits)
    m2 = jnp.max(rest, axis=-1, keepdims=True)
    i2 = jnp.min(jnp.where(rest == m2, lane, LANES), axis=-1, keepdims=True)
    e2 = jnp.exp(m2 - m1)
    gates = jnp.where(lane == i1, 1.0 / (1.0 + e2), 0.0) + jnp.where(lane == i2, e2 / (1.0 + e2), 0.0)
    gates_o[0] = gates

    tm = gates.shape[0]
    sel = gates > 0.0
    m_tok = jnp.where(sel, 1.0, 0.0)
    m_exp = m_tok.T[:2 * HALO]
    row = lax.broadcasted_iota(jnp.int32, (tm, tm), 0)
    col = lax.broadcasted_iota(jnp.int32, (tm, tm), 1)
    before = jnp.where(col < row, 1.0, 0.0).astype(BF16)
    after = jnp.where(row < col, 1.0, 0.0).astype(BF16)
    rank = jnp.dot(before, m_tok.astype(BF16), preferred_element_type=F32) + cnt_row[...]
    rank_t = jnp.dot(m_exp.astype(BF16), after, preferred_element_type=F32)[:HALO]
    rank_t = rank_t + jnp.tile(cnt_col[...], (1, tm // LANES))
    start_o[0, 0] = cnt_row[...]
    rank_o[0] = jnp.where(sel, rank, NEG_BIG)
    rank_t_o[0] = jnp.where(m_exp[:HALO] > 0.0, rank_t, NEG_BIG)
    cnt_row[...] += jnp.sum(m_tok, axis=0, keepdims=True)
    cnt_col[...] += jnp.sum(m_exp[:HALO], axis=1, keepdims=True)
    cnt_o[0] = cnt_row[...]


def _moe_expert_kernel(cnt_ref, bnd_ref, h_ref, gates_ref, rank_ref, rank_t_ref, x_ref, g2_ref, w1_ref, w3_ref, w2_ref,
                       o_ref, hs_ref, ys_ref, *, rb, tile):
    gi = pl.program_id(0)
    e = pl.program_id(1)
    f = pl.program_id(2)
    n_e = cnt_ref[gi * pl.num_programs(1) + e]
    unit = rb // 4
    n_unit = (n_e + unit - 1) // unit
    n_full = n_unit // 4
    tm = h_ref.shape[1]
    n_e_all = pl.num_programs(1)

    def for_tiles(start, size, fn):
        for i in range(tm // tile):
            lo = bnd_ref[(gi * (tm // tile + 1) + i) * n_e_all + e]
            hi = bnd_ref[(gi * (tm // tile + 1) + i + 1) * n_e_all + e]

            @pl.when(jnp.logical_and(hi > start, lo < start + size))
            def _():
                fn(slice(i * tile, (i + 1) * tile))

    def for_blocks(fn):
        def body(j, carry):
            fn(pl.multiple_of(j * rb, rb), rb)
            return carry

        lax.fori_loop(0, n_full, body, 0)

        has_half = (n_unit // 2) % 2

        @pl.when(has_half == 1)
        def _():
            fn(pl.multiple_of(n_full * rb, unit), rb // 2)

        @pl.when(n_unit % 2 == 1)
        def _():
            fn(pl.multiple_of(n_full * rb + has_half * (rb // 2), unit), unit)

    @pl.when(jnp.logical_and(e == 0, f == 0))
    def _():
        o_ref[...] = jnp.zeros_like(o_ref)

    @pl.when(f == 0)
    def _():
        rank_t = rank_t_ref[0, pl.ds(e, 1), :]

        def gather(start, size):
            hs_ref[pl.ds(start, size), :] = jnp.zeros((size, hs_ref.shape[1]), BF16)
            sub = lax.broadcasted_iota(jnp.int32, (size, tile), 0).astype(F32) + start.astype(F32)

            def from_tile(ts):
                pick = jnp.where(rank_t[:, ts] == sub, 1.0, 0.0).astype(BF16)
                got = jnp.dot(pick, h_ref[0, ts, :], preferred_element_type=F32)
                hs_ref[pl.ds(start, size), :] += got.astype(BF16)

            for_tiles(start, size, from_tile)

        for_blocks(gather)

    def expert_block(start, size):
        hs = hs_ref[pl.ds(start, size), :]
        a = jnp.dot(hs, w1_ref[0], preferred_element_type=F32)
        b = jnp.dot(hs, w3_ref[0], preferred_element_type=F32)
        return _dot(_silu(a) * b, w2_ref[0])

    @pl.when(f == 0)
    def _():
        def first(start, size):
            ys_ref[pl.ds(start, size), :] = expert_block(start, size)

        for_blocks(first)

    @pl.when(f > 0)
    def _():
        def more(start, size):
            ys_ref[pl.ds(start, size), :] += expert_block(start, size)

        for_blocks(more)

    last_f = f == pl.num_programs(2) - 1

    @pl.when(last_f)
    def _():
        lane = lax.broadcasted_iota(jnp.int32, (tm, LANES), 1)
        gate = jnp.sum(jnp.where(lane == e, gates_ref[0], 0.0), axis=-1, keepdims=True)
        rank = jnp.sum(jnp.where(lane == e, rank_ref[0], 0.0), axis=-1, keepdims=True)

        def scatter(start, size):
            slot = lax.broadcasted_iota(jnp.int32, (tile, size), 1).astype(F32) + start.astype(F32)
            ys = ys_ref[pl.ds(start, size), :].astype(BF16)

            def to_tile(ts):
                put = jnp.where(rank[ts] == slot, gate[ts], 0.0)
                o_ref[0, ts, :] += _dot(put, ys)

            for_tiles(start, size, to_tile)

        for_blocks(scatter)

    @pl.when(jnp.logical_and(e == pl.num_programs(1) - 1, last_f))
    def _():
        o_ref[0] = x_ref[0] + g2_ref[0] * o_ref[0]


def moe_ffn(xs, shift, scale, g, gate2, router, w1, w3, w2, *, g_off, tm_route=512, rb=512, tf):
    G, L, D = xs.shape
    E, _, FF = w1.shape
    n_g = G - g_off
    mod = pl.BlockSpec((1, 1, D), lambda g_, i: (g_ + g_off, 0, 0))
    tok = lambda width: pl.BlockSpec((1, tm_route, width), lambda g_, i: (g_, i, 0))
    n_tiles = L // tm_route
    h, gates, rank, rank_t, starts, cnt = pl.pallas_call(
        _moe_route_kernel,
        grid=(n_g, L // tm_route),
        in_specs=[pl.BlockSpec((1, tm_route, D), lambda g_, i: (g_ + g_off, i, 0)), mod, mod,
                  pl.BlockSpec((1, D), lambda g_, i: (0, 0)),
                  pl.BlockSpec((D, LANES), lambda g_, i: (0, 0))],
        out_specs=[tok(D), tok(LANES), tok(LANES),
                   pl.BlockSpec((1, HALO, tm_route), lambda g_, i: (g_, 0, i)),
                   pl.BlockSpec((1, 1, 1, LANES), lambda g_, i: (g_, i, 0, 0)),
                   pl.BlockSpec((1, 1, LANES), lambda g_, i: (g_, 0, 0))],
        out_shape=[jax.ShapeDtypeStruct((n_g, L, D), BF16),
                   jax.ShapeDtypeStruct((n_g, L, LANES), F32),
                   jax.ShapeDtypeStruct((n_g, L, LANES), F32),
                   jax.ShapeDtypeStruct((n_g, HALO, L), F32),
                   jax.ShapeDtypeStruct((n_g, n_tiles, 1, LANES), F32),
                   jax.ShapeDtypeStruct((n_g, 1, LANES), F32)],
        scratch_shapes=[pltpu.VMEM((1, LANES), F32), pltpu.VMEM((HALO, LANES), F32)],
        compiler_params=_cparams("parallel", "arbitrary"),
        name="moe_route",
    )(xs, shift, scale, g, router)
    counts = cnt[:, 0, :E].astype(jnp.int32)
    bounds = jnp.concatenate([starts[:, :, 0, :E].astype(jnp.int32), counts[:, None, :]], axis=1)

    once = pl.Buffered(1)
    grp = lambda width: pl.BlockSpec((1, L, width), lambda g_, e, f, c, b: (g_, 0, 0), pipeline_mode=once)
    return pl.pallas_call(
        functools.partial(_moe_expert_kernel, rb=rb, tile=tm_route),
        grid_spec=pltpu.PrefetchScalarGridSpec(
            num_scalar_prefetch=2,
            grid=(n_g, E, FF // tf),
            in_specs=[grp(D), grp(LANES), grp(LANES),
                      pl.BlockSpec((1, HALO, L), lambda g_, e, f, c, b: (g_, 0, 0), pipeline_mode=once),
                      pl.BlockSpec((1, L, D), lambda g_, e, f, c, b: (g_ + g_off, 0, 0), pipeline_mode=once),
                      pl.BlockSpec((1, 1, D), lambda g_, e, f, c, b: (g_ + g_off, 0, 0)),
                      pl.BlockSpec((1, D, tf), lambda g_, e, f, c, b: (e, 0, f)),
                      pl.BlockSpec((1, D, tf), lambda g_, e, f, c, b: (e, 0, f)),
                      pl.BlockSpec((1, tf, D), lambda g_, e, f, c, b: (e, f, 0))],
            out_specs=pl.BlockSpec((1, L, D), lambda g_, e, f, c, b: (g_, 0, 0), pipeline_mode=once),
            scratch_shapes=[pltpu.VMEM((L + rb, D), BF16), pltpu.VMEM((L + rb, D), F32)]),
        out_shape=jax.ShapeDtypeStruct((n_g, L, D), F32),
        compiler_params=pltpu.CompilerParams(dimension_semantics=("parallel", "arbitrary", "arbitrary"),
                                             vmem_limit_bytes=MOE_VMEM_LIMIT),
        name="moe_experts",
    )(counts.reshape(-1), bounds.reshape(-1), h, gates, rank, rank_t, xs, gate2, w1, w3, w2)


def _rope_tables(n):
    half = HEAD_DIM // 2
    nf = half // 2
    t = np.arange(n)
    inv = ROPE_BASE ** (-np.arange(nf, dtype=np.float64) / nf)
    d = np.arange(HEAD_DIM)
    pos = np.where(d[None, :] < half, (t // GRID_W)[:, None], (t % GRID_W)[:, None]).astype(np.float64)
    ang = pos * inv[d % nf][None, :]
    sign = np.where(d % half < nf, -1.0, 1.0)
    cos = np.tile(np.cos(ang), (1, HEADS))
    sin = np.tile(np.sin(ang) * sign[None, :], (1, HEADS))
    return jnp.asarray(cos, dtype=F32), jnp.asarray(sin, dtype=F32)


def kernel(x, c, ctx, c_ctx, ada_w, ada_b, norm1_g, norm2_g, w_in, rw_shift, rw_w0, rw_w2, rw_a0, rw_a2, rw_g2, rw_kk, rw_ka, rw_rk, rw_gn_g, rw_gn_b, hy_conv_w, hy_conv_b, hy_f_w1, hy_f_b1, hy_f_w2, hy_f_b2, hy_f_w3, hy_f_freq, hy_skip, na_q_gain, na_k_gain, na_rpb, w_br_rw, w_br_hy, w_br_na, w_out, ff_w1, ff_w3, ff_w2, moe_router, moe_w1, moe_w3, moe_w2):
    B, L, D = x.shape
    lc = ctx.shape[1]
    depth = ada_w.shape[0]
    assert B * lc == L and lc == SEQ_TILE and L % GRID_W == 0
    G = B + 1
    rows = L // GRID_W
    wr = min(NA_WIN_ROWS, rows)
    rw_in = 3 * WIDTH + 4 * RW_LORA + RW_GATE_LORA
    hy_in = (HY_ORDER + 1) * HY_WIDTH
    na_in = 3 * WIDTH
    splits = (rw_in, rw_in + hy_in, rw_in + hy_in + na_in)

    xs = jnp.concatenate([ctx.reshape(1, L, D), x], axis=0)
    act = jnp.concatenate([c_ctx[None, :], c], axis=0)
    act = act * _sigmoid(act)
    act = jnp.pad(act, ((0, (-G) % 16), (0, 0)))
    ones = _head_ones()
    cos, sin = _rope_tables(L)
    fwd_lat, inv_lat, _ = _dft_tables(L)
    fwd_ctx, inv_ctx, _ = _dft_tables(lc)
    fwd_lat, inv_lat, fwd_ctx, inv_ctx = (jnp.asarray(a, dtype=F32).astype(BF16)
                                          for a in (fwd_lat, inv_lat, fwd_ctx, inv_ctx))
    row2 = lambda a: a.reshape(1, -1)

    for li in range(depth):
        last = li == depth - 1
        mods = matmul(act, ada_w[li], tn=1024, split=True, name="adaln")[:G] + ada_b[li]
        sh1, sc1, g1, sh2, sc2, g2 = (m.reshape(G, 1, D) for m in jnp.split(mods, 6, axis=-1))

        u_rw, u_hy, qn, kn, vb, u_gate = branch_projections(
            xs, sh1, sc1, row2(norm1_g[li]), w_in[li].astype(BF16), row2(jnp.tile(na_q_gain[li], HEADS)),
            row2(jnp.tile(na_k_gain[li], HEADS)), ones, splits)

        r, v, kk, g, ld0, ld1, k0, k1, b0, b1 = rwkv_features(
            u_rw, cos, sin, rw_shift[li], rw_w0[li], rw_w2[li].astype(BF16), rw_a0[li], rw_a2[li].astype(BF16),
            rw_g2[li].astype(BF16), row2(rw_kk[li]), row2(rw_ka[li]), ones)
        y0, y1 = rwkv_scan(r, v, kk, ld0, k0, b0, ld1, k1, b1, batch=B)
        rw_parts = (y0, y1, r, k0, k1, v, g)
        rw_params = (row2(rw_rk[li]), row2(rw_gn_g[li]), row2(rw_gn_b[li]), ones)

        f_args = (hy_f_w1[li], hy_f_b1[li], hy_f_w2[li], hy_f_b2[li], hy_f_w3[li], hy_f_freq[li])
        taps = (hy_conv_w[li], row2(hy_conv_b[li]))
        skip = hy_skip[li].reshape(HY_ORDER, 1, HY_WIDTH)
        hre, him = hyena_filter_spectrum(L, *f_args, fwd_lat)
        conv = functools.partial(hyena_long_conv, skip=skip, fwd=fwd_lat, inv=inv_lat, n_seq=B, n=L,
                                 tf=HY_FREQ_TILE)
        z1 = conv(u_hy, (1, 0), u_hy, (1, 1), *taps, hre, him, order=0)
        y_hy = conv(z1, (0, 0), u_hy, (1, 2), *taps, hre, him, order=1)
        if not last:
            hy_c = u_hy[0].reshape(B, lc, hy_in)
            hre_c, him_c = hyena_filter_spectrum(lc, *f_args, fwd_ctx)
            conv_c = functools.partial(hyena_long_conv, skip=skip, fwd=fwd_ctx, inv=inv_ctx, n_seq=B, n=lc,
                                       tf=fwd_ctx.shape[1])
            z1c = conv_c(hy_c, (0, 0), hy_c, (0, 1), *taps, hre_c, him_c, order=0)
            y_hy_c = conv_c(z1c, (0, 0), hy_c, (0, 2), *taps, hre_c, him_c, order=1)
            y_hy = jnp.concatenate([y_hy_c.reshape(1, L, HY_WIDTH), y_hy], axis=0)

        bias = _na_bias_table(na_rpb[li], rows, wr)
        y_na = na_latent(qn, kn, vb, bias, batch=B)
        if not last:
            y_na = jnp.concatenate([na_context(qn, kn, vb, batch=B), y_na], axis=0)

        g_off = 1 if last else 0
        xs = merge_branches(xs, u_gate, rw_parts, y_hy, y_na, g1, rw_params, w_br_rw[li].astype(BF16),
                            w_br_hy[li].astype(BF16), w_br_na[li].astype(BF16), w_out[li].astype(BF16), g_off=g_off)
        if last:
            sh2, sc2, g2 = sh2[1:], sc2[1:], g2[1:]

        n2 = row2(norm2_g[li])
        if li % 2 == 0:
            j = li // 2
            xs = ffn_dense(xs, sh2, sc2, n2, g2, ff_w1[j].astype(BF16), ff_w3[j].astype(BF16), ff_w2[j].astype(BF16),
                           tf=ff_w1.shape[2] // 2)
        else:
            j = li // 2
            router = jnp.pad(moe_router[j], ((0, 0), (0, LANES - N_EXPERTS)))
            xs = moe_ffn(xs, sh2, sc2, n2, g2, router, moe_w1[j].astype(BF16), moe_w3[j].astype(BF16),
                         moe_w2[j].astype(BF16), g_off=0, tf=moe_w1.shape[3] // 4)
    return xs if depth == 0 else xs[-B:]
```

```python
import functools
import math

import numpy as np
import jax
import jax.numpy as jnp
from jax import lax
from jax.experimental import pallas as pl
from jax.experimental.pallas import tpu as pltpu

F32 = jnp.float32
BF16 = jnp.bfloat16

NORM_EPS = 1e-6
ROPE_BASE = 10000.0
GRID_W = 64
HEADS = 6
HEAD_DIM = 64
WIDTH = HEADS * HEAD_DIM
RW_LORA = 64
RW_GATE_LORA = 128
RW_GN_EPS = 64e-5
HY_WIDTH = 256
HY_ORDER = 2
HY_EMB_DIM = 33
HY_DECAY_TARGET = 1e-2
HY_FAST_DECAY = 0.3
HY_SLOW_DECAY = 1.5
NA_WIN_ROWS = 8
NA_WIN_COLS = 16
N_EXPERTS = 8
LANES = 128
SEQ_TILE = 256
HALO = 8
CHUNK = 64
SCAN_CHUNKS = 4
HY_SEQS = 2
NA_ROWS = 8
HY_FREQ_TILE = 384
VMEM_LIMIT = 56 * 1024 * 1024
MOE_VMEM_LIMIT = 60 * 1024 * 1024
NEG_BIG = -1e30


def _cparams(*sem):
    return pltpu.CompilerParams(dimension_semantics=sem, vmem_limit_bytes=VMEM_LIMIT)


def _dot(a, b):
    return jnp.dot(a.astype(BF16), b.astype(BF16), preferred_element_type=F32)


def _dot_nt(a, b):
    return lax.dot_general(a.astype(BF16), b.astype(BF16), (((1,), (1,)), ((), ())),
                           preferred_element_type=F32)


def _dot_tn(a, b):
    return lax.dot_general(a.astype(BF16), b.astype(BF16), (((0,), (0,)), ((), ())),
                           preferred_element_type=F32)


def _dot_split(a, b_exact):
    hi = a.astype(BF16)
    lo = (a - hi.astype(F32)).astype(BF16)
    return (jnp.dot(hi, b_exact, preferred_element_type=F32)
            + jnp.dot(lo, b_exact, preferred_element_type=F32))


def _sigmoid(x):
    return 1.0 / (1.0 + jnp.exp(-x))


def _silu(x):
    return x * _sigmoid(x)


def _head_ones():
    h = np.arange(WIDTH) // HEAD_DIM
    return jnp.asarray((h[:, None] == h[None, :]).astype(np.float32), dtype=BF16)


def _mm_kernel(x_ref, w_ref, o_ref, *, split):
    x = x_ref[...]
    w = w_ref[...]
    if split:
        xh = x.astype(BF16)
        xl = (x - xh.astype(F32)).astype(BF16)
        wh = w.astype(BF16)
        wl = (w - wh.astype(F32)).astype(BF16)
        o_ref[...] = (jnp.dot(xh, wh, preferred_element_type=F32)
                      + jnp.dot(xl, wh, preferred_element_type=F32)
                      + jnp.dot(xh, wl, preferred_element_type=F32))
    else:
        o_ref[...] = jnp.dot(x.astype(BF16), w.astype(BF16), preferred_element_type=F32)


def matmul(x, w, *, tn, split=False, name="matmul"):
    m, k = x.shape
    n = w.shape[1]
    assert n % tn == 0
    return pl.pallas_call(
        functools.partial(_mm_kernel, split=split),
        grid=(n // tn,),
        in_specs=[pl.BlockSpec((m, k), lambda j: (0, 0)),
                  pl.BlockSpec((k, tn), lambda j: (0, j))],
        out_specs=pl.BlockSpec((m, tn), lambda j: (0, j)),
        out_shape=jax.ShapeDtypeStruct((m, n), F32),
        compiler_params=_cparams("parallel"),
        name=name,
    )(x, w)


def _norm_mod(x, g, shift, scale):
    ms = jnp.mean(x * x, axis=-1, keepdims=True)
    h = x * lax.rsqrt(ms + NORM_EPS) * g
    return h * (1.0 + scale) + shift


def _proj_kernel(x_ref, xb_ref, xa_ref, sh_ref, sc_ref, g_ref, w_ref, qg_ref, kg_ref, ones_ref, *rest, splits):
    rw_refs, rw_outs, (hy_o, q_o, k_o, v_o, gate_o) = rest[:10], rest[10:20], rest[20:]
    norm = lambda x: _norm_mod(x, g_ref[...], sh_ref[0], sc_ref[0]).astype(BF16)
    h = norm(x_ref[0])
    s_rw, s_hy, s_na = splits
    proj = lambda lo, hi: jnp.dot(h, w_ref[:, lo:hi], preferred_element_type=F32)
    edge = lambda ref: jnp.dot(norm(ref[0]), w_ref[:, 0:s_rw], preferred_element_type=F32)
    _rw_features(proj(0, s_rw), edge(xb_ref)[HALO - 1:HALO], edge(xa_ref)[0:1], *rw_refs, ones_ref, *rw_outs)
    hy_o[0] = proj(s_rw, s_hy)
    gate_o[0] = proj(s_na, w_ref.shape[1]).astype(BF16)
    ones = ones_ref[...]
    inv_n = 1.0 / HEAD_DIM
    q = proj(s_hy, s_hy + WIDTH)
    k = proj(s_hy + WIDTH, s_hy + 2 * WIDTH)
    qn = q * lax.rsqrt(_dot_split(q * q, ones) * inv_n + NORM_EPS) * qg_ref[...]
    kn = k * lax.rsqrt(_dot_split(k * k, ones) * inv_n + NORM_EPS) * kg_ref[...]
    q_o[0] = (qn * (HEAD_DIM ** -0.5)).astype(BF16)
    k_o[0] = kn.astype(BF16)
    v_o[0] = proj(s_hy + 2 * WIDTH, s_na).astype(BF16)


def branch_projections(xs, shift, scale, g, w, q_gain, k_gain, ones, cos, sin, rw_params, splits, *, tm=512):
    G, L, D = xs.shape
    s_rw, s_hy, s_na = splits
    assert tm % SEQ_TILE == 0 and L % tm == 0
    widths = (WIDTH,) * 10 + (s_hy - s_rw, WIDTH, WIDTH, WIDTH, w.shape[1] - s_na)
    dtypes = (F32,) * 11 + (BF16,) * 4
    per = tm // HALO
    last = L // HALO - 1
    full = lambda a: pl.BlockSpec(a.shape, lambda g_, i: (0,) * a.ndim, pipeline_mode=pl.Buffered(1))
    tab = pl.BlockSpec((tm, WIDTH), lambda g_, i: (i, 0))
    return pl.pallas_call(
        functools.partial(_proj_kernel, splits=splits),
        grid=(G, L // tm),
        in_specs=[pl.BlockSpec((1, tm, D), lambda g_, i: (g_, i, 0)),
                  pl.BlockSpec((1, HALO, D), lambda g_, i: (g_, jnp.maximum(i * per - 1, 0), 0)),
                  pl.BlockSpec((1, HALO, D), lambda g_, i: (g_, jnp.minimum((i + 1) * per, last), 0)),
                  pl.BlockSpec((1, 1, D), lambda g_, i: (g_, 0, 0)),
                  pl.BlockSpec((1, 1, D), lambda g_, i: (g_, 0, 0)),
                  full(g), full(w), full(q_gain), full(k_gain), full(ones), tab, tab]
                 + [full(p) for p in rw_params],
        out_specs=[pl.BlockSpec((1, tm, n), lambda g_, i: (g_, i, 0)) for n in widths],
        out_shape=[jax.ShapeDtypeStruct((G, L, n), dt) for n, dt in zip(widths, dtypes)],
        compiler_params=_cparams("parallel", "parallel"),
        name="branch_projections",
    )(xs, xs, xs, shift, scale, g, w, q_gain, k_gain, ones, cos, sin, *rw_params)


def _neighbours(u, u_before, u_after):
    g_ = pl.program_id(0)
    i = pl.program_id(1)
    tm = u.shape[0]
    is_ctx = g_ == 0
    has_before = jnp.logical_and(g_ > 0, i > 0)
    has_after = jnp.logical_and(g_ > 0, i < pl.num_programs(1) - 1)
    row = lax.broadcasted_iota(jnp.int32, u.shape, 0)
    prev = jnp.where(row == 0, jnp.where(has_before, u_before, 0.0), pltpu.roll(u, 1, 0))
    nxt = jnp.where(row == tm - 1, jnp.where(has_after, u_after, 0.0), pltpu.roll(u, tm - 1, 0))
    prev = jnp.where(jnp.logical_and(is_ctx, row % SEQ_TILE == 0), 0.0, prev)
    nxt = jnp.where(jnp.logical_and(is_ctx, row % SEQ_TILE == SEQ_TILE - 1), 0.0, nxt)
    return prev, nxt


def _rope(z, cos, sin_signed):
    lane = lax.broadcasted_iota(jnp.int32, z.shape, 1)
    partner = jnp.where(lane % 32 < 16, pltpu.roll(z, WIDTH - 16, 1), pltpu.roll(z, 16, 1))
    return z * cos + partner * sin_signed


def _rw_features(u, u_before, u_after, cos_ref, sin_ref, mu_ref, w0_ref, w2_ref, a0_ref, a2_ref,
                 g2_ref, kk_ref, ka_ref, ones_ref,
                 r_o, v_o, kkn_o, g_o, ld0_o, ld1_o, k0_o, k1_o, b0_o, b1_o):
    prev, nxt = _neighbours(u, u_before, u_after)
    u = u + mu_ref[0:1, :] * (prev - u) + mu_ref[1:2, :] * (nxt - u)
    r = u[:, 0:WIDTH]
    k = u[:, WIDTH:2 * WIDTH]
    v = u[:, 2 * WIDTH:3 * WIDTH]
    o = 3 * WIDTH
    lw = u[:, o:o + 2 * RW_LORA]
    la = u[:, o + 2 * RW_LORA:o + 4 * RW_LORA]
    lg = u[:, o + 4 * RW_LORA:o + 4 * RW_LORA + RW_GATE_LORA]
    is_lat = pl.program_id(0) > 0
    cos = jnp.where(is_lat, cos_ref[...], 1.0)
    sin = jnp.where(is_lat, sin_ref[...], 0.0)
    r = _rope(r, cos, sin)
    k = _rope(k, cos, sin)
    g = _dot(_sigmoid(lg), g2_ref[...])
    kk = k * kk_ref[...]
    nrm = jnp.sqrt(_dot_split(kk * kk, ones_ref[...]))
    kk = kk / jnp.maximum(nrm, 1e-12)
    r_o[0] = r
    v_o[0] = v
    kkn_o[0] = kk
    g_o[0] = g
    tanh_lw = jnp.tanh(lw)
    for d, (ld_o, k_o, b_o) in enumerate(((ld0_o, k0_o, b0_o), (ld1_o, k1_o, b1_o))):
        zw = w0_ref[d:d + 1, :] + _dot(tanh_lw[:, d * RW_LORA:(d + 1) * RW_LORA], w2_ref[d])
        softplus = jnp.maximum(-zw, 0.0) + jnp.log(1.0 + jnp.exp(-jnp.abs(zw)))
        ld_o[0] = -jnp.exp(-softplus - 0.5)
        iclr = _sigmoid(a0_ref[d:d + 1, :] + _dot(la[:, d * RW_LORA:(d + 1) * RW_LORA], a2_ref[d]))
        k_o[0] = k * (1.0 + (iclr - 1.0) * ka_ref[...])
        b_o[0] = kk * iclr


def _rw_chunk_operands(r, ld, k, v, kk, b, incl, reverse):
    C = CHUNK
    ld_hi = ld.astype(BF16)
    ld_lo = (ld - ld_hi.astype(F32)).astype(BF16)
    tri = jnp.where(incl, 1.0, 0.0).astype(BF16)
    cum = (jnp.dot(tri, ld_hi, preferred_element_type=F32)
           + jnp.dot(tri, ld_lo, preferred_element_type=F32))
    tot = cum[0:1, :] if reverse else cum[C - 1:C, :]
    e_neg = jnp.exp(-cum)
    e_rem = jnp.exp(tot - cum)
    return dict(
        a_t=(-kk * jnp.exp(cum - ld)).astype(BF16),
        r_t=(r * jnp.exp(cum)).astype(BF16),
        b_t=(b * e_neg).astype(BF16),
        k_t=(k * e_neg).astype(BF16),
        b_p=(b * e_rem).astype(BF16),
        k_p=(k * e_rem).astype(BF16),
        p_c=jnp.exp(tot),
        v=v.astype(BF16))


def _rw_scan_kernel(*refs):
    in_refs, y_refs, s_ref = refs[:12], refs[12:14], refs[14]
    C = CHUNK

    @pl.when(pl.program_id(1) == 0)
    def _():
        s_ref[...] = jnp.zeros_like(s_ref)

    row = lax.broadcasted_iota(jnp.int32, (C, C), 0)
    col = lax.broadcasted_iota(jnp.int32, (C, C), 1)
    eye = row == col
    chains = []
    for d in range(2):
        incl, strict = (row <= col, row < col) if d == 1 else (row >= col, row > col)
        for order in range(SCAN_CHUNKS):
            sub = SCAN_CHUNKS - 1 - order if d == 1 else order
            rs = slice(sub * C, (sub + 1) * C)
            ops = _rw_chunk_operands(*(ref[0, rs, :] for ref in in_refs[6 * d:6 * d + 6]), incl, d == 1)
            for h in range(HEADS):
                sl = slice(h * HEAD_DIM, (h + 1) * HEAD_DIM)
                ch = {name: val[:, sl] for name, val in ops.items()}
                ch.update(d=d, h=h, sl=sl, rs=rs, order=order, incl=incl, strict=strict)
                chains.append(ch)

    each = lambda fn: [fn(ch) for ch in chains]
    both = lambda fn, xs: [fn(ch, x) for ch, x in zip(chains, xs)]
    ar = each(lambda ch: jnp.concatenate([ch["a_t"], ch["r_t"]], axis=0))
    bk = each(lambda ch: jnp.concatenate([ch["b_t"], ch["k_t"]], axis=0))
    g = [_dot_nt(x, y) for x, y in zip(ar, bk)]
    a_ab = both(lambda ch, x: jnp.where(ch["strict"], x[:C, :C], 0.0), g)
    a_rb = both(lambda ch, x: jnp.where(ch["incl"], x[C:, :C], 0.0), g)
    a_k = both(lambda ch, x: jnp.concatenate([jnp.where(ch["strict"], x[:C, C:], 0.0),
                                              jnp.where(ch["incl"], x[C:, C:], 0.0)], axis=0), g)
    av = both(lambda ch, a: _dot(a, ch["v"]), a_k)
    t_inv = [jnp.where(eye, 1.0, a) for a in a_ab]
    pw = [_dot(a, a) for a in a_ab]
    for _ in range(int(math.log2(C)) - 2):
        nxt = [_dot(jnp.concatenate([p, t], axis=0), p) for p, t in zip(pw, t_inv)]
        t_inv = [t + x[C:] for t, x in zip(t_inv, nxt)]
        pw = [x[:C] for x in nxt]
    t_inv = [t + _dot(t, p) for t, p in zip(t_inv, pw)]
    w = [_dot(t, jnp.concatenate([ch["a_t"], x[:C].astype(BF16)], axis=1)) for ch, t, x in zip(chains, t_inv, av)]
    qy = [_dot(a, x) for a, x in zip(a_rb, w)]
    q_p = [ch["r_t"].astype(F32) + x[:, :HEAD_DIM] for ch, x in zip(chains, qy)]
    y_p = [x[C:] + z[:, HEAD_DIM:] for x, z in zip(av, qy)]
    mn = both(lambda ch, x: _dot_tn(x, ch["b_p"]), w)
    m_mat = both(lambda ch, x: jnp.where(eye, ch["p_c"], 0.0) + x[:HEAD_DIM], mn)
    n_mat = both(lambda ch, x: x[HEAD_DIM:] + _dot_tn(ch["v"], ch["k_p"]), mn)
    state = {(d, h): s_ref[d, h] for d in range(2) for h in range(HEADS)}
    for order in range(SCAN_CHUNKS):
        now = [i for i, ch in enumerate(chains) if ch["order"] == order]
        s0 = [state[chains[i]["d"], chains[i]["h"]] for i in now]
        y = [_dot_nt(q_p[i], s) + y_p[i] for i, s in zip(now, s0)]
        s1 = [_dot(s, m_mat[i]) + n_mat[i] for i, s in zip(now, s0)]
        for i, y_c, s_c in zip(now, y, s1):
            ch = chains[i]
            y_refs[ch["d"]][0, ch["rs"], ch["sl"]] = y_c
            state[ch["d"], ch["h"]] = s_c
    for (d, h), s_c in state.items():
        s_ref[d, h] = s_c


def rwkv_scan(r, v, kk, ld0, k0, b0, ld1, k1, b1, *, batch):
    G, L, _ = r.shape
    step = SCAN_CHUNKS * CHUNK
    lc = L // batch
    ns_ctx = lc // step
    ns_lat = L // step
    n_steps = ns_ctx + ns_lat

    def idx(reverse, bi, s):
        in_ctx = s < ns_ctx
        c_ctx = ns_ctx - 1 - s if reverse else s
        c_lat = n_steps - 1 - s if reverse else s - ns_ctx
        return (jnp.where(in_ctx, 0, bi + 1), jnp.where(in_ctx, bi * ns_ctx + c_ctx, c_lat), 0)

    fwd = pl.BlockSpec((1, step, WIDTH), functools.partial(idx, False))
    rev = pl.BlockSpec((1, step, WIDTH), functools.partial(idx, True))
    out = jax.ShapeDtypeStruct((G, L, WIDTH), F32)
    return pl.pallas_call(
        _rw_scan_kernel,
        grid=(batch, n_steps),
        in_specs=[fwd] * 6 + [rev] * 6,
        out_specs=[fwd, rev],
        out_shape=[out, out],
        scratch_shapes=[pltpu.VMEM((2, HEADS, HEAD_DIM, HEAD_DIM), F32)],
        compiler_params=_cparams("parallel", "arbitrary"),
        name="rwkv_scan",
    )(r, ld0, k0, v, kk, b0, r, ld1, k1, v, kk, b1)


def _rw_readout(y0, y1, r, k0, k1, v, g, r_k, gn_g, gn_b, ones):
    y = y0 + y1
    inv_n = 1.0 / HEAD_DIM
    mu = _dot_split(y, ones) * inv_n
    yc = y - mu
    var = _dot_split(yc * yc, ones) * inv_n
    yn = yc * lax.rsqrt(var + RW_GN_EPS) * gn_g + gn_b
    bonus = _dot_split(r * (k0 + k1) * r_k, ones) * v
    return (yn + bonus) * g


def _short_conv(u, w_ref, b_ref):
    n = u.shape[0]
    row = lax.broadcasted_iota(jnp.int32, u.shape, 0)
    prev = jnp.where(row == 0, 0.0, pltpu.roll(u, 1, 0))
    nxt = jnp.where(row == n - 1, 0.0, pltpu.roll(u, n - 1, 0))
    return b_ref[...] + w_ref[0:1, :] * prev + w_ref[1:2, :] * u + w_ref[2:3, :] * nxt


@functools.lru_cache(maxsize=None)
def _dft_tables(n):
    nn = 2 * n
    nf = n + 256
    kf = np.arange(nf, dtype=np.int64)
    t = np.arange(n, dtype=np.int64)
    ang = 2.0 * np.pi * ((kf[:, None] * t[None, :]) % nn).astype(np.float64) / nn
    valid = (kf <= n)[:, None]
    cosm = np.where(valid, np.cos(ang), 0.0)
    sinm = np.where(valid, np.sin(ang), 0.0)
    ck = np.where((kf == 0) | (kf == n), 1.0, 2.0)[:, None] / nn
    fwd = np.stack([cosm, -sinm])
    inv = np.stack([(ck * cosm).T, (-ck * sinm).T])
    return fwd, inv, nf


def _hy_conv_kernel(*refs, z_is_raw):
    nb = HY_SEQS
    z_refs, x_refs = refs[:nb], refs[nb:2 * nb]
    (zw_ref, zb_ref, xw_ref, xb_ref, fwd_ref, inv_ref, hre_ref, him_ref, skip_ref,
     o_ref, acc_ref, zs_ref) = refs[2 * nb:]
    f = pl.program_id(1)
    z_of = lambda i: _short_conv(z_refs[i][0], zw_ref, zb_ref) if z_is_raw else z_refs[i][0]

    @pl.when(f == 0)
    def _():
        acc_ref[...] = jnp.zeros_like(acc_ref)
        for i in range(nb):
            zs_ref[:, i * HY_WIDTH:(i + 1) * HY_WIDTH] = z_of(i).astype(BF16)

    zb = zs_ref[...]
    zre = jnp.dot(fwd_ref[0], zb, preferred_element_type=F32)
    zim = jnp.dot(fwd_ref[1], zb, preferred_element_type=F32)
    hre = jnp.concatenate([hre_ref[0]] * nb, axis=1)
    him = jnp.concatenate([him_ref[0]] * nb, axis=1)
    yre = zre * hre - zim * him
    yim = zre * him + zim * hre
    acc_ref[...] += (jnp.dot(inv_ref[0], yre.astype(BF16), preferred_element_type=F32)
                     + jnp.dot(inv_ref[1], yim.astype(BF16), preferred_element_type=F32))

    @pl.when(f == pl.num_programs(1) - 1)
    def _():
        for i in range(nb):
            conv = acc_ref[:, i * HY_WIDTH:(i + 1) * HY_WIDTH]
            o_ref[i] = _short_conv(x_refs[i][0], xw_ref, xb_ref) * (conv + skip_ref[0] * z_of(i))


def hyena_long_conv(z_arr, z_blk, x_arr, x_blk, conv_w, conv_b, hre, him, skip, order, fwd, inv, *, n_seq, n, tf):
    nf = fwd.shape[1]
    nb = HY_SEQS
    zo, zc = z_blk
    xo, xc = x_blk
    seq = lambda off, cb, i: pl.BlockSpec((1, n, HY_WIDTH), lambda s, f: (s * nb + i + off, 0, cb))
    taps = lambda cb: [pl.BlockSpec((conv_w.shape[0], HY_WIDTH), lambda s, f: (0, cb)),
                       pl.BlockSpec((1, HY_WIDTH), lambda s, f: (0, cb))]
    z_is_raw = order == 0
    return pl.pallas_call(
        functools.partial(_hy_conv_kernel, z_is_raw=z_is_raw),
        grid=(n_seq // nb, nf // tf),
        in_specs=[seq(zo, zc, i) for i in range(nb)] + [seq(xo, xc, i) for i in range(nb)]
                 + taps(zc if z_is_raw else 0) + taps(xc)
                 + [pl.BlockSpec((2, tf, n), lambda s, f: (0, f, 0)),
                    pl.BlockSpec((2, n, tf), lambda s, f: (0, 0, f)),
                    pl.BlockSpec((1, tf, HY_WIDTH), lambda s, f: (order, f, 0)),
                    pl.BlockSpec((1, tf, HY_WIDTH), lambda s, f: (order, f, 0)),
                    pl.BlockSpec((1, 1, HY_WIDTH), lambda s, f: (order, 0, 0))],
        out_specs=pl.BlockSpec((nb, n, HY_WIDTH), lambda s, f: (s, 0, 0)),
        out_shape=jax.ShapeDtypeStruct((n_seq, n, HY_WIDTH), F32),
        scratch_shapes=[pltpu.VMEM((n, nb * HY_WIDTH), F32), pltpu.VMEM((n, nb * HY_WIDTH), BF16)],
        compiler_params=_cparams("parallel", "arbitrary"),
        name=f"hyena_long_conv_n{n}_o{order}",
    )(*([z_arr] * nb), *([x_arr] * nb), conv_w, conv_b, conv_w, conv_b, fwd, inv, hre, him, skip)


def hyena_filter_spectrum(n, w1, b1, w2, b2, w3, freq, fwd):
    hp = lax.Precision.HIGHEST
    pos = jnp.arange(n, dtype=F32)
    t = pos / max(n - 1, 1)
    bands = (HY_EMB_DIM - 1) // 2
    fr = jnp.linspace(1e-4, bands - 1, bands, dtype=F32)
    ang = (2 * math.pi / n) * pos[:, None] * fr[None, :]
    z = jnp.concatenate([t[:, None], jnp.cos(ang), -jnp.sin(ang)], axis=-1)
    h = jnp.sin(freq * (jnp.dot(z, w1, precision=hp) + b1))
    h = jnp.sin(freq * (jnp.dot(h, w2, precision=hp) + b2))
    h = matmul(h, w3, tn=w3.shape[1], split=True, name="hyena_filter_out")
    h = h.reshape(n, HY_ORDER, 2, HY_WIDTH)
    max_decay = math.log(HY_DECAY_TARGET) / HY_FAST_DECAY
    min_decay = math.log(HY_DECAY_TARGET) / HY_SLOW_DECAY
    deltas = jnp.abs(jnp.linspace(min_decay, max_decay, HY_WIDTH, dtype=F32))
    h = h * jnp.exp(-t[:, None] * deltas)[:, None, None, :]
    h_fwd = h[:, :, 0]
    h_bwd = h[:, :, 1] * (pos > 0).astype(F32)[:, None, None]
    l1 = jnp.sum(jnp.abs(h_fwd), axis=0) + jnp.sum(jnp.abs(h_bwd), axis=0)
    even = ((h_fwd + h_bwd) / l1).reshape(n, HY_ORDER * HY_WIDTH)
    odd = ((h_fwd - h_bwd) / l1).reshape(n, HY_ORDER * HY_WIDTH)
    nf = fwd.shape[1]
    hre = matmul(fwd[0], even, tn=HY_ORDER * HY_WIDTH, name="hyena_filter_dft_re")
    him = matmul(fwd[1], odd, tn=HY_ORDER * HY_WIDTH, name="hyena_filter_dft_im")
    to_ofc = lambda a: jnp.moveaxis(a.reshape(nf, HY_ORDER, HY_WIDTH), 1, 0)
    return to_ofc(hre), to_ofc(him)


def _na_lat_kernel(q_ref, k_ref, v_ref, kc_ref, vc_ref, bias_ref, o_ref, *, rows, wr):
    kc = kc_ref[0]
    vc = vc_ref[0]
    jobs = []
    for rr in range(NA_ROWS):
        i = pl.program_id(1) * NA_ROWS + rr
        r0 = jnp.clip(i - wr // 2, 0, rows - wr)
        off = r0 - i + (NA_WIN_ROWS - 1)
        start = pl.multiple_of(r0 * GRID_W, GRID_W)
        qs = slice(rr * GRID_W, (rr + 1) * GRID_W)
        q = q_ref[0, qs, :]
        kw = k_ref[0, pl.ds(start, wr * GRID_W), :]
        vw = v_ref[0, pl.ds(start, wr * GRID_W), :]
        for h in range(HEADS):
            sl = slice(h * HEAD_DIM, (h + 1) * HEAD_DIM)
            jobs.append(dict(q=q[:, sl], kw=kw[:, sl], vw=vw[:, sl], h=h, off=off, qs=qs, sl=sl))
    s_loc = [_dot_nt(j["q"], j["kw"]) + bias_ref[j["h"], j["off"]] for j in jobs]
    s_ctx = [_dot_nt(j["q"], kc[:, j["sl"]]) for j in jobs]
    m = [jnp.maximum(jnp.max(a, axis=-1, keepdims=True), jnp.max(b, axis=-1, keepdims=True))
         for a, b in zip(s_loc, s_ctx)]
    p_loc = [jnp.exp(a - mm) for a, mm in zip(s_loc, m)]
    p_ctx = [jnp.exp(b - mm) for b, mm in zip(s_ctx, m)]
    den = [jnp.sum(a, axis=-1, keepdims=True) + jnp.sum(b, axis=-1, keepdims=True) for a, b in zip(p_loc, p_ctx)]
    o = [_dot(a, j["vw"]) + _dot(b, vc[:, j["sl"]]) for a, b, j in zip(p_loc, p_ctx, jobs)]
    for j, oo, dd in zip(jobs, o, den):
        o_ref[0, j["qs"], j["sl"]] = oo / dd


def _na_bias_table(rpb, rows, wr):
    hp = lax.Precision.HIGHEST
    offs = np.arange(NA_WIN_ROWS) - (NA_WIN_ROWS - 1)
    dr = offs[:, None] + np.arange(wr)[None, :] + NA_WIN_ROWS - 1
    c_ar = np.arange(GRID_W)
    dc = np.clip(c_ar[None, :] - c_ar[:, None] + NA_WIN_COLS - 1, 0, 2 * NA_WIN_COLS - 2)
    c0 = np.clip(c_ar - NA_WIN_COLS // 2, 0, GRID_W - NA_WIN_COLS)
    in_win = (c_ar[None, :] >= c0[:, None]) & (c_ar[None, :] < c0[:, None] + NA_WIN_COLS)
    oh_r = jnp.asarray(dr[..., None] == np.arange(2 * NA_WIN_ROWS - 1), dtype=F32)
    oh_c = jnp.asarray(dc[..., None] == np.arange(2 * NA_WIN_COLS - 1), dtype=F32)
    by_row = jnp.einsum("hab,ora->horb", rpb.astype(F32), oh_r, precision=hp)
    bias = jnp.einsum("horb,qkb->hoqrk", by_row, oh_c, precision=hp)
    bias = jnp.where(in_win[None, None, :, None, :], bias, NEG_BIG)
    return bias.reshape(HEADS, NA_WIN_ROWS, GRID_W, wr * GRID_W)


def na_latent(qn, kn, vb, bias, *, batch):
    G, L, _ = qn.shape
    lc = L // batch
    rows = L // GRID_W
    wr = min(NA_WIN_ROWS, rows)
    assert rows % NA_ROWS == 0
    qspec = pl.BlockSpec((1, NA_ROWS * GRID_W, WIDTH), lambda b_, i: (b_ + 1, i, 0))
    seq = pl.BlockSpec((1, L, WIDTH), lambda b_, i: (b_ + 1, 0, 0))
    ctx = pl.BlockSpec((1, lc, WIDTH), lambda b_, i: (0, b_, 0))
    return pl.pallas_call(
        functools.partial(_na_lat_kernel, rows=rows, wr=wr),
        grid=(batch, rows // NA_ROWS),
        in_specs=[qspec, seq, seq, ctx, ctx, pl.BlockSpec(bias.shape, lambda b_, i: (0, 0, 0, 0))],
        out_specs=pl.BlockSpec((1, NA_ROWS * GRID_W, WIDTH), lambda b_, i: (b_, i, 0)),
        out_shape=jax.ShapeDtypeStruct((batch, L, WIDTH), F32),
        compiler_params=_cparams("parallel", "arbitrary"),
        name="na_latent",
    )(qn, kn, vb, kn, vb, bias)


def _na_ctx_kernel(q_ref, k_ref, v_ref, o_ref):
    q = q_ref[0]
    k = k_ref[0]
    v = v_ref[0]
    for h in range(HEADS):
        sl = slice(h * HEAD_DIM, (h + 1) * HEAD_DIM)
        s = _dot_nt(q[:, sl], k[:, sl])
        p = jnp.exp(s - jnp.max(s, axis=-1, keepdims=True))
        o_ref[0, :, sl] = _dot(p, v[:, sl]) / jnp.sum(p, axis=-1, keepdims=True)


def na_context(qn, kn, vb, *, batch):
    G, L, _ = qn.shape
    lc = L // batch
    spec = pl.BlockSpec((1, lc, WIDTH), lambda b_: (0, b_, 0))
    return pl.pallas_call(
        _na_ctx_kernel,
        grid=(batch,),
        in_specs=[spec] * 3,
        out_specs=spec,
        out_shape=jax.ShapeDtypeStruct((1, L, WIDTH), F32),
        compiler_params=_cparams("parallel"),
        name="na_context",
    )(qn, kn, vb)


def _merge_kernel(x_ref, gl_ref, y0_ref, y1_ref, r_ref, k0_ref, k1_ref, v_ref, g_ref, hy_ref, na_ref, g1_ref,
                  rk_ref, gg_ref, gb_ref, ones_ref, wr_ref, wh_ref, wn_ref, wo_ref, o_ref):
    D = x_ref.shape[-1]
    gl = gl_ref[0].astype(F32)
    y_rw = _rw_readout(y0_ref[0], y1_ref[0], r_ref[0], k0_ref[0], k1_ref[0], v_ref[0], g_ref[0],
                       rk_ref[...], gg_ref[...], gb_ref[...], ones_ref[...])
    m = (_sigmoid(gl[:, 0:D]) * _dot(y_rw, wr_ref[...])
         + _sigmoid(gl[:, D:2 * D]) * _dot(hy_ref[0], wh_ref[...])
         + _sigmoid(gl[:, 2 * D:3 * D]) * _dot(na_ref[0], wn_ref[...]))
    o_ref[0] = x_ref[0] + g1_ref[0] * _dot(m, wo_ref[...])


def merge_branches(xs, u_gate, rw_parts, y_hy, y_na, gate1, rw_params, w_rw, w_hy, w_na, w_o, *, g_off, tm=512):
    G, L, D = xs.shape
    tok = lambda width: pl.BlockSpec((1, tm, width), lambda g_, i: (g_ + g_off, i, 0))
    own = lambda width: pl.BlockSpec((1, tm, width), lambda g_, i: (g_, i, 0))
    full = lambda a: pl.BlockSpec(a.shape, lambda g_, i: (0,) * a.ndim, pipeline_mode=pl.Buffered(1))
    consts = (*rw_params, w_rw, w_hy, w_na, w_o)
    return pl.pallas_call(
        _merge_kernel,
        grid=(G - g_off, L // tm),
        in_specs=[tok(D), tok(3 * D)] + [tok(WIDTH)] * len(rw_parts) + [own(HY_WIDTH), own(WIDTH),
                  pl.BlockSpec((1, 1, D), lambda g_, i: (g_ + g_off, 0, 0))] + [full(c) for c in consts],
        out_specs=pl.BlockSpec((1, tm, D), lambda g_, i: (g_, i, 0)),
        out_shape=jax.ShapeDtypeStruct((G - g_off, L, D), F32),
        compiler_params=_cparams("parallel", "parallel"),
        name="merge_branches",
    )(xs, u_gate, *rw_parts, y_hy, y_na, gate1, *consts)


def _ffn_kernel(x_ref, sh_ref, sc_ref, g_ref, g2_ref, w1_ref, w3_ref, w2_ref, o_ref, h_ref, acc_ref):
    f = pl.program_id(2)

    @pl.when(f == 0)
    def _():
        h_ref[...] = _norm_mod(x_ref[0], g_ref[...], sh_ref[0], sc_ref[0]).astype(BF16)
        acc_ref[...] = jnp.zeros_like(acc_ref)

    h = h_ref[...]
    a = jnp.dot(h, w1_ref[...], preferred_element_type=F32)
    b = jnp.dot(h, w3_ref[...], preferred_element_type=F32)
    acc_ref[...] += _dot(_silu(a) * b, w2_ref[...])

    @pl.when(f == pl.num_programs(2) - 1)
    def _():
        o_ref[0] = x_ref[0] + g2_ref[0] * acc_ref[...]


def ffn_dense(xs, shift, scale, g, gate2, w1, w3, w2, *, tm=1024, tf):
    G, L, D = xs.shape
    FF = w1.shape[1]
    assert L % tm == 0 and FF % tf == 0
    mod = pl.BlockSpec((1, 1, D), lambda g_, i, f: (g_, 0, 0))
    return pl.pallas_call(
        _ffn_kernel,
        grid=(G, L // tm, FF // tf),
        in_specs=[pl.BlockSpec((1, tm, D), lambda g_, i, f: (g_, i, 0)), mod, mod,
                  pl.BlockSpec((1, D), lambda g_, i, f: (0, 0)), mod,
                  pl.BlockSpec((D, tf), lambda g_, i, f: (0, f)),
                  pl.BlockSpec((D, tf), lambda g_, i, f: (0, f)),
                  pl.BlockSpec((tf, D), lambda g_, i, f: (f, 0))],
        out_specs=pl.BlockSpec((1, tm, D), lambda g_, i, f: (g_, i, 0)),
        out_shape=jax.ShapeDtypeStruct((G, L, D), F32),
        scratch_shapes=[pltpu.VMEM((tm, D), BF16), pltpu.VMEM((tm, D), F32)],
        compiler_params=_cparams("parallel", "parallel", "arbitrary"),
        name="ffn_dense",
    )(xs, shift, scale, g, gate2, w1, w3, w2)


def _moe_route_kernel(x_ref, sh_ref, sc_ref, g_ref, rt_ref, h_o, gates_o, rank_o, rank_t_o, start_o, cnt_o,
                      cnt_row, cnt_col):
    i = pl.program_id(1)

    @pl.when(i == 0)
    def _():
        cnt_row[...] = jnp.zeros_like(cnt_row)
        cnt_col[...] = jnp.zeros_like(cnt_col)

    h = _norm_mod(x_ref[0], g_ref[...], sh_ref[0], sc_ref[0])
    h_o[0] = h.astype(BF16)
    rt = rt_ref[...]
    hh = h.astype(BF16)
    hl = (h - hh.astype(F32)).astype(BF16)
    rh = rt.astype(BF16)
    rl = (rt - rh.astype(F32)).astype(BF16)
    logits = (jnp.dot(hh, rh, preferred_element_type=F32) + jnp.dot(hl, rh, preferred_element_type=F32)
              + jnp.dot(hh, rl, preferred_element_type=F32))
    lane = lax.broadcasted_iota(jnp.int32, logits.shape, 1)
    logits = jnp.where(lane < N_EXPERTS, logits, NEG_BIG)
    m1 = jnp.max(logits, axis=-1, keepdims=True)
    i1 = jnp.min(jnp.where(logits == m1, lane, LANES), axis=-1, keepdims=True)
    rest = jnp.where(lane == i1, NEG_BIG, logits)
    m2 = jnp.max(rest, axis=-1, keepdims=True)
    i2 = jnp.min(jnp.where(rest == m2, lane, LANES), axis=-1, keepdims=True)
    e2 = jnp.exp(m2 - m1)
    gates = jnp.where(lane == i1, 1.0 / (1.0 + e2), 0.0) + jnp.where(lane == i2, e2 / (1.0 + e2), 0.0)
    gates_o[0] = gates

    tm = gates.shape[0]
    sel = gates > 0.0
    m_tok = jnp.where(sel, 1.0, 0.0)
    m_exp = m_tok.T[:2 * HALO]
    row = lax.broadcasted_iota(jnp.int32, (tm, tm), 0)
    col = lax.broadcasted_iota(jnp.int32, (tm, tm), 1)
    before = jnp.where(col < row, 1.0, 0.0).astype(BF16)
    after = jnp.where(row < col, 1.0, 0.0).astype(BF16)
    rank = jnp.dot(before, m_tok.astype(BF16), preferred_element_type=F32) + cnt_row[...]
    rank_t = jnp.dot(m_exp.astype(BF16), after, preferred_element_type=F32)[:HALO]
    rank_t = rank_t + jnp.tile(cnt_col[...], (1, tm // LANES))
    start_o[0, 0] = cnt_row[...]
    rank_o[0] = jnp.where(sel, rank, NEG_BIG)
    rank_t_o[0] = jnp.where(m_exp[:HALO] > 0.0, rank_t, NEG_BIG)
    cnt_row[...] += jnp.sum(m_tok, axis=0, keepdims=True)
    cnt_col[...] += jnp.sum(m_exp[:HALO], axis=1, keepdims=True)
    cnt_o[0] = cnt_row[...]


def _moe_expert_kernel(cnt_ref, bnd_ref, h_ref, gates_ref, rank_ref, rank_t_ref, x_ref, g2_ref, w1_ref, w3_ref, w2_ref,
                       o_ref, hs_ref, ys_ref, *, rb, tile):
    gi = pl.program_id(0)
    e = pl.program_id(1)
    f = pl.program_id(2)
    n_e = cnt_ref[gi * pl.num_programs(1) + e]
    unit = rb // 4
    n_unit = (n_e + unit - 1) // unit
    n_full = n_unit // 4
    tm = h_ref.shape[1]
    n_e_all = pl.num_programs(1)

    def for_tiles(start, size, fn):
        for i in range(tm // tile):
            lo = bnd_ref[(gi * (tm // tile + 1) + i) * n_e_all + e]
            hi = bnd_ref[(gi * (tm // tile + 1) + i + 1) * n_e_all + e]

            @pl.when(jnp.logical_and(hi > start, lo < start + size))
            def _():
                fn(slice(i * tile, (i + 1) * tile))

    def for_blocks(fn):
        def body(j, carry):
            fn(pl.multiple_of(j * rb, rb), rb)
            return carry

        lax.fori_loop(0, n_full, body, 0)

        has_half = (n_unit // 2) % 2

        @pl.when(has_half == 1)
        def _():
            fn(pl.multiple_of(n_full * rb, unit), rb // 2)

        @pl.when(n_unit % 2 == 1)
        def _():
            fn(pl.multiple_of(n_full * rb + has_half * (rb // 2), unit), unit)

    @pl.when(jnp.logical_and(e == 0, f == 0))
    def _():
        o_ref[...] = jnp.zeros_like(o_ref)

    @pl.when(f == 0)
    def _():
        rank_t = rank_t_ref[0, pl.ds(e, 1), :]

        def gather(start, size):
            hs_ref[pl.ds(start, size), :] = jnp.zeros((size, hs_ref.shape[1]), BF16)
            sub = lax.broadcasted_iota(jnp.int32, (size, tile), 0).astype(F32) + start.astype(F32)

            def from_tile(ts):
                pick = jnp.where(rank_t[:, ts] == sub, 1.0, 0.0).astype(BF16)
                got = jnp.dot(pick, h_ref[0, ts, :], preferred_element_type=F32)
                hs_ref[pl.ds(start, size), :] += got.astype(BF16)

            for_tiles(start, size, from_tile)

        for_blocks(gather)

    def expert_block(start, size):
        hs = hs_ref[pl.ds(start, size), :]
        a = jnp.dot(hs, w1_ref[0], preferred_element_type=F32)
        b = jnp.dot(hs, w3_ref[0], preferred_element_type=F32)
        return _dot(_silu(a) * b, w2_ref[0])

    @pl.when(f == 0)
    def _():
        def first(start, size):
            ys_ref[pl.ds(start, size), :] = expert_block(start, size)

        for_blocks(first)

    @pl.when(f > 0)
    def _():
        def more(start, size):
            ys_ref[pl.ds(start, size), :] += expert_block(start, size)

        for_blocks(more)

    last_f = f == pl.num_programs(2) - 1

    @pl.when(last_f)
    def _():
        lane = lax.broadcasted_iota(jnp.int32, (tm, LANES), 1)
        gate = jnp.sum(jnp.where(lane == e, gates_ref[0], 0.0), axis=-1, keepdims=True)
        rank = jnp.sum(jnp.where(lane == e, rank_ref[0], 0.0), axis=-1, keepdims=True)

        def scatter(start, size):
            slot = lax.broadcasted_iota(jnp.int32, (tile, size), 1).astype(F32) + start.astype(F32)
            ys = ys_ref[pl.ds(start, size), :].astype(BF16)

            def to_tile(ts):
                put = jnp.where(rank[ts] == slot, gate[ts], 0.0)
                o_ref[0, ts, :] += _dot(put, ys)

            for_tiles(start, size, to_tile)

        for_blocks(scatter)

    @pl.when(jnp.logical_and(e == pl.num_programs(1) - 1, last_f))
    def _():
        o_ref[0] = x_ref[0] + g2_ref[0] * o_ref[0]


def moe_ffn(xs, shift, scale, g, gate2, router, w1, w3, w2, *, g_off, tm_route=512, rb=512, tf):
    G, L, D = xs.shape
    E, _, FF = w1.shape
    n_g = G - g_off
    mod = pl.BlockSpec((1, 1, D), lambda g_, i: (g_ + g_off, 0, 0))
    tok = lambda width: pl.BlockSpec((1, tm_route, width), lambda g_, i: (g_, i, 0))
    n_tiles = L // tm_route
    h, gates, rank, rank_t, starts, cnt = pl.pallas_call(
        _moe_route_kernel,
        grid=(n_g, L // tm_route),
        in_specs=[pl.BlockSpec((1, tm_route, D), lambda g_, i: (g_ + g_off, i, 0)), mod, mod,
                  pl.BlockSpec((1, D), lambda g_, i: (0, 0)),
                  pl.BlockSpec((D, LANES), lambda g_, i: (0, 0))],
        out_specs=[tok(D), tok(LANES), tok(LANES),
                   pl.BlockSpec((1, HALO, tm_route), lambda g_, i: (g_, 0, i)),
                   pl.BlockSpec((1, 1, 1, LANES), lambda g_, i: (g_, i, 0, 0)),
                   pl.BlockSpec((1, 1, LANES), lambda g_, i: (g_, 0, 0))],
        out_shape=[jax.ShapeDtypeStruct((n_g, L, D), BF16),
                   jax.ShapeDtypeStruct((n_g, L, LANES), F32),
                   jax.ShapeDtypeStruct((n_g, L, LANES), F32),
                   jax.ShapeDtypeStruct((n_g, HALO, L), F32),
                   jax.ShapeDtypeStruct((n_g, n_tiles, 1, LANES), F32),
                   jax.ShapeDtypeStruct((n_g, 1, LANES), F32)],
        scratch_shapes=[pltpu.VMEM((1, LANES), F32), pltpu.VMEM((HALO, LANES), F32)],
        compiler_params=_cparams("parallel", "arbitrary"),
        name="moe_route",
    )(xs, shift, scale, g, router)
    counts = cnt[:, 0, :E].astype(jnp.int32)
    bounds = jnp.concatenate([starts[:, :, 0, :E].astype(jnp.int32), counts[:, None, :]], axis=1)

    once = pl.Buffered(1)
    grp = lambda width: pl.BlockSpec((1, L, width), lambda g_, e, f, c, b: (g_, 0, 0), pipeline_mode=once)
    return pl.pallas_call(
        functools.partial(_moe_expert_kernel, rb=rb, tile=tm_route),
        grid_spec=pltpu.PrefetchScalarGridSpec(
            num_scalar_prefetch=2,
            grid=(n_g, E, FF // tf),
            in_specs=[grp(D), grp(LANES), grp(LANES),
                      pl.BlockSpec((1, HALO, L), lambda g_, e, f, c, b: (g_, 0, 0), pipeline_mode=once),
                      pl.BlockSpec((1, L, D), lambda g_, e, f, c, b: (g_ + g_off, 0, 0), pipeline_mode=once),
                      pl.BlockSpec((1, 1, D), lambda g_, e, f, c, b: (g_ + g_off, 0, 0)),
                      pl.BlockSpec((1, D, tf), lambda g_, e, f, c, b: (e, 0, f)),
                      pl.BlockSpec((1, D, tf), lambda g_, e, f, c, b: (e, 0, f)),
                      pl.BlockSpec((1, tf, D), lambda g_, e, f, c, b: (e, f, 0))],
            out_specs=pl.BlockSpec((1, L, D), lambda g_, e, f, c, b: (g_, 0, 0), pipeline_mode=once),
            scratch_shapes=[pltpu.VMEM((L + rb, D), BF16), pltpu.VMEM((L + rb, D), F32)]),
        out_shape=jax.ShapeDtypeStruct((n_g, L, D), F32),
        compiler_params=pltpu.CompilerParams(dimension_semantics=("parallel", "arbitrary", "arbitrary"),
                                             vmem_limit_bytes=MOE_VMEM_LIMIT),
        name="moe_experts",
    )(counts.reshape(-1), bounds.reshape(-1), h, gates, rank, rank_t, xs, gate2, w1, w3, w2)


def _rope_tables(n):
    half = HEAD_DIM // 2
    nf = half // 2
    t = np.arange(n)
    inv = ROPE_BASE ** (-np.arange(nf, dtype=np.float64) / nf)
    d = np.arange(HEAD_DIM)
    pos = np.where(d[None, :] < half, (t // GRID_W)[:, None], (t % GRID_W)[:, None]).astype(np.float64)
    ang = pos * inv[d % nf][None, :]
    sign = np.where(d % half < nf, -1.0, 1.0)
    cos = np.tile(np.cos(ang), (1, HEADS))
    sin = np.tile(np.sin(ang) * sign[None, :], (1, HEADS))
    return jnp.asarray(cos, dtype=F32), jnp.asarray(sin, dtype=F32)


def kernel(x, c, ctx, c_ctx, ada_w, ada_b, norm1_g, norm2_g, w_in, rw_shift, rw_w0, rw_w2, rw_a0, rw_a2, rw_g2, rw_kk, rw_ka, rw_rk, rw_gn_g, rw_gn_b, hy_conv_w, hy_conv_b, hy_f_w1, hy_f_b1, hy_f_w2, hy_f_b2, hy_f_w3, hy_f_freq, hy_skip, na_q_gain, na_k_gain, na_rpb, w_br_rw, w_br_hy, w_br_na, w_out, ff_w1, ff_w3, ff_w2, moe_router, moe_w1, moe_w3, moe_w2):
    B, L, D = x.shape
    lc = ctx.shape[1]
    depth = ada_w.shape[0]
    assert B * lc == L and lc == SEQ_TILE and L % GRID_W == 0
    G = B + 1
    rows = L // GRID_W
    wr = min(NA_WIN_ROWS, rows)
    rw_in = 3 * WIDTH + 4 * RW_LORA + RW_GATE_LORA
    hy_in = (HY_ORDER + 1) * HY_WIDTH
    na_in = 3 * WIDTH
    splits = (rw_in, rw_in + hy_in, rw_in + hy_in + na_in)

    xs = jnp.concatenate([ctx.reshape(1, L, D), x], axis=0)
    act = jnp.concatenate([c_ctx[None, :], c], axis=0)
    act = act * _sigmoid(act)
    act = jnp.pad(act, ((0, (-G) % 16), (0, 0)))
    ones = _head_ones()
    cos, sin = _rope_tables(L)
    fwd_lat, inv_lat, _ = _dft_tables(L)
    fwd_ctx, inv_ctx, _ = _dft_tables(lc)
    fwd_lat, inv_lat, fwd_ctx, inv_ctx = (jnp.asarray(a, dtype=F32).astype(BF16)
                                          for a in (fwd_lat, inv_lat, fwd_ctx, inv_ctx))
    row2 = lambda a: a.reshape(1, -1)

    for li in range(depth):
        last = li == depth - 1
        mods = matmul(act, ada_w[li], tn=1024, split=True, name="adaln")[:G] + ada_b[li]
        sh1, sc1, g1, sh2, sc2, g2 = (m.reshape(G, 1, D) for m in jnp.split(mods, 6, axis=-1))

        rw_feat_params = (rw_shift[li], rw_w0[li], rw_w2[li].astype(BF16), rw_a0[li], rw_a2[li].astype(BF16),
                          rw_g2[li].astype(BF16), row2(rw_kk[li]), row2(rw_ka[li]))
        r, v, kk, g, ld0, ld1, k0, k1, b0, b1, u_hy, qn, kn, vb, u_gate = branch_projections(
            xs, sh1, sc1, row2(norm1_g[li]), w_in[li].astype(BF16), row2(jnp.tile(na_q_gain[li], HEADS)),
            row2(jnp.tile(na_k_gain[li], HEADS)), ones, cos, sin, rw_feat_params, splits)

        y0, y1 = rwkv_scan(r, v, kk, ld0, k0, b0, ld1, k1, b1, batch=B)
        rw_parts = (y0, y1, r, k0, k1, v, g)
        rw_params = (row2(rw_rk[li]), row2(rw_gn_g[li]), row2(rw_gn_b[li]), ones)

        f_args = (hy_f_w1[li], hy_f_b1[li], hy_f_w2[li], hy_f_b2[li], hy_f_w3[li], hy_f_freq[li])
        taps = (hy_conv_w[li], row2(hy_conv_b[li]))
        skip = hy_skip[li].reshape(HY_ORDER, 1, HY_WIDTH)
        hre, him = hyena_filter_spectrum(L, *f_args, fwd_lat)
        conv = functools.partial(hyena_long_conv, skip=skip, fwd=fwd_lat, inv=inv_lat, n_seq=B, n=L,
                                 tf=HY_FREQ_TILE)
        z1 = conv(u_hy, (1, 0), u_hy, (1, 1), *taps, hre, him, order=0)
        y_hy = conv(z1, (0, 0), u_hy, (1, 2), *taps, hre, him, order=1)
        if not last:
            hy_c = u_hy[0].reshape(B, lc, hy_in)
            hre_c, him_c = hyena_filter_spectrum(lc, *f_args, fwd_ctx)
            conv_c = functools.partial(hyena_long_conv, skip=skip, fwd=fwd_ctx, inv=inv_ctx, n_seq=B, n=lc,
                                       tf=fwd_ctx.shape[1])
            z1c = conv_c(hy_c, (0, 0), hy_c, (0, 1), *taps, hre_c, him_c, order=0)
            y_hy_c = conv_c(z1c, (0, 0), hy_c, (0, 2), *taps, hre_c, him_c, order=1)
            y_hy = jnp.concatenate([y_hy_c.reshape(1, L, HY_WIDTH), y_hy], axis=0)

        bias = _na_bias_table(na_rpb[li], rows, wr)
        y_na = na_latent(qn, kn, vb, bias, batch=B)
        if not last:
            y_na = jnp.concatenate([na_context(qn, kn, vb, batch=B), y_na], axis=0)

        g_off = 1 if last else 0
        xs = merge_branches(xs, u_gate, rw_parts, y_hy, y_na, g1, rw_params, w_br_rw[li].astype(BF16),
                            w_br_hy[li].astype(BF16), w_br_na[li].astype(BF16), w_out[li].astype(BF16), g_off=g_off)
        if last:
            sh2, sc2, g2 = sh2[1:], sc2[1:], g2[1:]

        n2 = row2(norm2_g[li])
        if li % 2 == 0:
            j = li // 2
            xs = ffn_dense(xs, sh2, sc2, n2, g2, ff_w1[j].astype(BF16), ff_w3[j].astype(BF16), ff_w2[j].astype(BF16),
                           tf=ff_w1.shape[2] // 2)
        else:
            j = li // 2
            router = jnp.pad(moe_router[j], ((0, 0), (0, LANES - N_EXPERTS)))
            xs = moe_ffn(xs, sh2, sc2, n2, g2, router, moe_w1[j].astype(BF16), moe_w3[j].astype(BF16),
                         moe_w2[j].astype(BF16), g_off=0, tf=moe_w1.shape[3] // 4)
    return xs if depth == 0 else xs[-B:]
```

```python
import functools
import math

import numpy as np
import jax
import jax.numpy as jnp
from jax import lax
from jax.experimental import pallas as pl
from jax.experimental.pallas import tpu as pltpu

F32 = jnp.float32
BF16 = jnp.bfloat16

NORM_EPS = 1e-6
ROPE_BASE = 10000.0
GRID_W = 64
HEADS = 6
HEAD_DIM = 64
WIDTH = HEADS * HEAD_DIM
RW_LORA = 64
RW_GATE_LORA = 128
RW_GN_EPS = 64e-5
HY_WIDTH = 256
HY_ORDER = 2
HY_EMB_DIM = 33
HY_DECAY_TARGET = 1e-2
HY_FAST_DECAY = 0.3
HY_SLOW_DECAY = 1.5
NA_WIN_ROWS = 8
NA_WIN_COLS = 16
N_EXPERTS = 8
LANES = 128
SEQ_TILE = 256
HALO = 8
CHUNK = 64
SCAN_CHUNKS = 4
HY_SEQS = 2
NA_ROWS = 8
MOE_FF_CHUNKS = 7
HY_FREQ_TILE = 384
VMEM_LIMIT = 56 * 1024 * 1024
MOE_VMEM_LIMIT = 60 * 1024 * 1024
NEG_BIG = -1e30


def _cparams(*sem):
    return pltpu.CompilerParams(dimension_semantics=sem, vmem_limit_bytes=VMEM_LIMIT)


def _dot(a, b):
    return jnp.dot(a.astype(BF16), b.astype(BF16), preferred_element_type=F32)


def _dot_nt(a, b):
    return lax.dot_general(a.astype(BF16), b.astype(BF16), (((1,), (1,)), ((), ())),
                           preferred_element_type=F32)


def _dot_tn(a, b):
    return lax.dot_general(a.astype(BF16), b.astype(BF16), (((0,), (0,)), ((), ())),
                           preferred_element_type=F32)


def _dot_split(a, b_exact):
    hi = a.astype(BF16)
    lo = (a - hi.astype(F32)).astype(BF16)
    return (jnp.dot(hi, b_exact, preferred_element_type=F32)
            + jnp.dot(lo, b_exact, preferred_element_type=F32))


def _sigmoid(x):
    return 1.0 / (1.0 + jnp.exp(-x))


def _silu(x):
    return x * _sigmoid(x)


def _head_ones():
    h = np.arange(WIDTH) // HEAD_DIM
    return jnp.asarray((h[:, None] == h[None, :]).astype(np.float32), dtype=BF16)


def _mm_kernel(x_ref, w_ref, o_ref, *, split):
    x = x_ref[...]
    w = w_ref[...]
    if split:
        xh = x.astype(BF16)
        xl = (x - xh.astype(F32)).astype(BF16)
        wh = w.astype(BF16)
        wl = (w - wh.astype(F32)).astype(BF16)
        o_ref[...] = (jnp.dot(xh, wh, preferred_element_type=F32)
                      + jnp.dot(xl, wh, preferred_element_type=F32)
                      + jnp.dot(xh, wl, preferred_element_type=F32))
    else:
        o_ref[...] = jnp.dot(x.astype(BF16), w.astype(BF16), preferred_element_type=F32)


def matmul(x, w, *, tn, split=False, name="matmul"):
    m, k = x.shape
    n = w.shape[1]
    assert n % tn == 0
    return pl.pallas_call(
        functools.partial(_mm_kernel, split=split),
        grid=(n // tn,),
        in_specs=[pl.BlockSpec((m, k), lambda j: (0, 0)),
                  pl.BlockSpec((k, tn), lambda j: (0, j))],
        out_specs=pl.BlockSpec((m, tn), lambda j: (0, j)),
        out_shape=jax.ShapeDtypeStruct((m, n), F32),
        compiler_params=_cparams("parallel"),
        name=name,
    )(x, w)


def _norm_mod(x, g, shift, scale):
    ms = jnp.mean(x * x, axis=-1, keepdims=True)
    h = x * lax.rsqrt(ms + NORM_EPS) * g
    return h * (1.0 + scale) + shift


def _proj_kernel(x_ref, xb_ref, xa_ref, sh_ref, sc_ref, g_ref, w_ref, qg_ref, kg_ref, ones_ref, *rest, splits):
    rw_refs, rw_outs, (hy_o, q_o, k_o, v_o, gate_o) = rest[:10], rest[10:20], rest[20:]
    norm = lambda x: _norm_mod(x, g_ref[...], sh_ref[0], sc_ref[0]).astype(BF16)
    h = norm(x_ref[0])
    s_rw, s_hy, s_na = splits
    proj = lambda lo, hi: jnp.dot(h, w_ref[:, lo:hi], preferred_element_type=F32)
    edge = lambda ref: jnp.dot(norm(ref[0]), w_ref[:, 0:s_rw], preferred_element_type=F32)
    _rw_features(proj(0, s_rw), edge(xb_ref)[HALO - 1:HALO], edge(xa_ref)[0:1], *rw_refs, ones_ref, *rw_outs)
    hy_o[0] = proj(s_rw, s_hy)
    gate_o[0] = proj(s_na, w_ref.shape[1]).astype(BF16)
    ones = ones_ref[...]
    inv_n = 1.0 / HEAD_DIM
    q = proj(s_hy, s_hy + WIDTH)
    k = proj(s_hy + WIDTH, s_hy + 2 * WIDTH)
    qn = q * lax.rsqrt(_dot_split(q * q, ones) * inv_n + NORM_EPS) * qg_ref[...]
    kn = k * lax.rsqrt(_dot_split(k * k, ones) * inv_n + NORM_EPS) * kg_ref[...]
    q_o[0] = (qn * (HEAD_DIM ** -0.5)).astype(BF16)
    k_o[0] = kn.astype(BF16)
    v_o[0] = proj(s_hy + 2 * WIDTH, s_na).astype(BF16)


def branch_projections(xs, shift, scale, g, w, q_gain, k_gain, ones, cos, sin, rw_params, splits, *, tm=512):
    G, L, D = xs.shape
    s_rw, s_hy, s_na = splits
    assert tm % SEQ_TILE == 0 and L % tm == 0
    widths = (WIDTH,) * 10 + (s_hy - s_rw, WIDTH, WIDTH, WIDTH, w.shape[1] - s_na)
    dtypes = (F32,) * 11 + (BF16,) * 4
    per = tm // HALO
    last = L // HALO - 1
    full = lambda a: pl.BlockSpec(a.shape, lambda g_, i: (0,) * a.ndim, pipeline_mode=pl.Buffered(1))
    tab = pl.BlockSpec((tm, WIDTH), lambda g_, i: (i, 0))
    return pl.pallas_call(
        functools.partial(_proj_kernel, splits=splits),
        grid=(G, L // tm),
        in_specs=[pl.BlockSpec((1, tm, D), lambda g_, i: (g_, i, 0)),
                  pl.BlockSpec((1, HALO, D), lambda g_, i: (g_, jnp.maximum(i * per - 1, 0), 0)),
                  pl.BlockSpec((1, HALO, D), lambda g_, i: (g_, jnp.minimum((i + 1) * per, last), 0)),
                  pl.BlockSpec((1, 1, D), lambda g_, i: (g_, 0, 0)),
                  pl.BlockSpec((1, 1, D), lambda g_, i: (g_, 0, 0)),
                  full(g), full(w), full(q_gain), full(k_gain), full(ones), tab, tab]
                 + [full(p) for p in rw_params],
        out_specs=[pl.BlockSpec((1, tm, n), lambda g_, i: (g_, i, 0)) for n in widths],
        out_shape=[jax.ShapeDtypeStruct((G, L, n), dt) for n, dt in zip(widths, dtypes)],
        compiler_params=_cparams("parallel", "parallel"),
        name="branch_projections",
    )(xs, xs, xs, shift, scale, g, w, q_gain, k_gain, ones, cos, sin, *rw_params)


def _neighbours(u, u_before, u_after):
    g_ = pl.program_id(0)
    i = pl.program_id(1)
    tm = u.shape[0]
    is_ctx = g_ == 0
    has_before = jnp.logical_and(g_ > 0, i > 0)
    has_after = jnp.logical_and(g_ > 0, i < pl.num_programs(1) - 1)
    row = lax.broadcasted_iota(jnp.int32, u.shape, 0)
    prev = jnp.where(row == 0, jnp.where(has_before, u_before, 0.0), pltpu.roll(u, 1, 0))
    nxt = jnp.where(row == tm - 1, jnp.where(has_after, u_after, 0.0), pltpu.roll(u, tm - 1, 0))
    prev = jnp.where(jnp.logical_and(is_ctx, row % SEQ_TILE == 0), 0.0, prev)
    nxt = jnp.where(jnp.logical_and(is_ctx, row % SEQ_TILE == SEQ_TILE - 1), 0.0, nxt)
    return prev, nxt


def _rope(z, cos, sin_signed):
    lane = lax.broadcasted_iota(jnp.int32, z.shape, 1)
    partner = jnp.where(lane % 32 < 16, pltpu.roll(z, WIDTH - 16, 1), pltpu.roll(z, 16, 1))
    return z * cos + partner * sin_signed


def _rw_features(u, u_before, u_after, cos_ref, sin_ref, mu_ref, w0_ref, w2_ref, a0_ref, a2_ref,
                 g2_ref, kk_ref, ka_ref, ones_ref,
                 r_o, v_o, kkn_o, g_o, ld0_o, ld1_o, k0_o, k1_o, b0_o, b1_o):
    prev, nxt = _neighbours(u, u_before, u_after)
    u = u + mu_ref[0:1, :] * (prev - u) + mu_ref[1:2, :] * (nxt - u)
    r = u[:, 0:WIDTH]
    k = u[:, WIDTH:2 * WIDTH]
    v = u[:, 2 * WIDTH:3 * WIDTH]
    o = 3 * WIDTH
    lw = u[:, o:o + 2 * RW_LORA]
    la = u[:, o + 2 * RW_LORA:o + 4 * RW_LORA]
    lg = u[:, o + 4 * RW_LORA:o + 4 * RW_LORA + RW_GATE_LORA]
    is_lat = pl.program_id(0) > 0
    cos = jnp.where(is_lat, cos_ref[...], 1.0)
    sin = jnp.where(is_lat, sin_ref[...], 0.0)
    r = _rope(r, cos, sin)
    k = _rope(k, cos, sin)
    g = _dot(_sigmoid(lg), g2_ref[...])
    kk = k * kk_ref[...]
    nrm = jnp.sqrt(_dot_split(kk * kk, ones_ref[...]))
    kk = kk / jnp.maximum(nrm, 1e-12)
    r_o[0] = r
    v_o[0] = v
    kkn_o[0] = kk
    g_o[0] = g
    tanh_lw = jnp.tanh(lw)
    for d, (ld_o, k_o, b_o) in enumerate(((ld0_o, k0_o, b0_o), (ld1_o, k1_o, b1_o))):
        zw = w0_ref[d:d + 1, :] + _dot(tanh_lw[:, d * RW_LORA:(d + 1) * RW_LORA], w2_ref[d])
        softplus = jnp.maximum(-zw, 0.0) + jnp.log(1.0 + jnp.exp(-jnp.abs(zw)))
        ld_o[0] = -jnp.exp(-softplus - 0.5)
        iclr = _sigmoid(a0_ref[d:d + 1, :] + _dot(la[:, d * RW_LORA:(d + 1) * RW_LORA], a2_ref[d]))
        k_o[0] = k * (1.0 + (iclr - 1.0) * ka_ref[...])
        b_o[0] = kk * iclr


def _rw_chunk_operands(r, ld, k, v, kk, b, incl, reverse):
    C = CHUNK
    ld_hi = ld.astype(BF16)
    ld_lo = (ld - ld_hi.astype(F32)).astype(BF16)
    tri = jnp.where(incl, 1.0, 0.0).astype(BF16)
    cum = (jnp.dot(tri, ld_hi, preferred_element_type=F32)
           + jnp.dot(tri, ld_lo, preferred_element_type=F32))
    tot = cum[0:1, :] if reverse else cum[C - 1:C, :]
    e_neg = jnp.exp(-cum)
    e_rem = jnp.exp(tot - cum)
    return dict(
        a_t=(-kk * jnp.exp(cum - ld)).astype(BF16),
        r_t=(r * jnp.exp(cum)).astype(BF16),
        b_t=(b * e_neg).astype(BF16),
        k_t=(k * e_neg).astype(BF16),
        b_p=(b * e_rem).astype(BF16),
        k_p=(k * e_rem).astype(BF16),
        p_c=jnp.exp(tot),
        v=v.astype(BF16))


def _rw_scan_kernel(*refs):
    in_refs, y_refs, s_ref = refs[:12], refs[12:14], refs[14]
    C = CHUNK

    @pl.when(pl.program_id(1) == 0)
    def _():
        s_ref[...] = jnp.zeros_like(s_ref)

    row = lax.broadcasted_iota(jnp.int32, (C, C), 0)
    col = lax.broadcasted_iota(jnp.int32, (C, C), 1)
    eye = row == col
    chains = []
    for d in range(2):
        incl, strict = (row <= col, row < col) if d == 1 else (row >= col, row > col)
        for order in range(SCAN_CHUNKS):
            sub = SCAN_CHUNKS - 1 - order if d == 1 else order
            rs = slice(sub * C, (sub + 1) * C)
            ops = _rw_chunk_operands(*(ref[0, rs, :] for ref in in_refs[6 * d:6 * d + 6]), incl, d == 1)
            for h in range(HEADS):
                sl = slice(h * HEAD_DIM, (h + 1) * HEAD_DIM)
                ch = {name: val[:, sl] for name, val in ops.items()}
                ch.update(d=d, h=h, sl=sl, rs=rs, order=order, incl=incl, strict=strict)
                chains.append(ch)

    each = lambda fn: [fn(ch) for ch in chains]
    both = lambda fn, xs: [fn(ch, x) for ch, x in zip(chains, xs)]
    ar = each(lambda ch: jnp.concatenate([ch["a_t"], ch["r_t"]], axis=0))
    bk = each(lambda ch: jnp.concatenate([ch["b_t"], ch["k_t"]], axis=0))
    g = [_dot_nt(x, y) for x, y in zip(ar, bk)]
    a_ab = both(lambda ch, x: jnp.where(ch["strict"], x[:C, :C], 0.0), g)
    a_rb = both(lambda ch, x: jnp.where(ch["incl"], x[C:, :C], 0.0), g)
    a_k = both(lambda ch, x: jnp.concatenate([jnp.where(ch["strict"], x[:C, C:], 0.0),
                                              jnp.where(ch["incl"], x[C:, C:], 0.0)], axis=0), g)
    av = both(lambda ch, a: _dot(a, ch["v"]), a_k)
    t_inv = [jnp.where(eye, 1.0, a) for a in a_ab]
    pw = [_dot(a, a) for a in a_ab]
    for _ in range(int(math.log2(C)) - 2):
        nxt = [_dot(jnp.concatenate([p, t], axis=0), p) for p, t in zip(pw, t_inv)]
        t_inv = [t + x[C:] for t, x in zip(t_inv, nxt)]
        pw = [x[:C] for x in nxt]
    t_inv = [t + _dot(t, p) for t, p in zip(t_inv, pw)]
    w = [_dot(t, jnp.concatenate([ch["a_t"], x[:C].astype(BF16)], axis=1)) for ch, t, x in zip(chains, t_inv, av)]
    qy = [_dot(a, x) for a, x in zip(a_rb, w)]
    q_p = [ch["r_t"].astype(F32) + x[:, :HEAD_DIM] for ch, x in zip(chains, qy)]
    y_p = [x[C:] + z[:, HEAD_DIM:] for x, z in zip(av, qy)]
    mn = both(lambda ch, x: _dot_tn(x, ch["b_p"]), w)
    m_mat = both(lambda ch, x: jnp.where(eye, ch["p_c"], 0.0) + x[:HEAD_DIM], mn)
    n_mat = both(lambda ch, x: x[HEAD_DIM:] + _dot_tn(ch["v"], ch["k_p"]), mn)
    state = {(d, h): s_ref[d, h] for d in range(2) for h in range(HEADS)}
    for order in range(SCAN_CHUNKS):
        now = [i for i, ch in enumerate(chains) if ch["order"] == order]
        s0 = [state[chains[i]["d"], chains[i]["h"]] for i in now]
        y = [_dot_nt(q_p[i], s) + y_p[i] for i, s in zip(now, s0)]
        s1 = [_dot(s, m_mat[i]) + n_mat[i] for i, s in zip(now, s0)]
        for i, y_c, s_c in zip(now, y, s1):
            ch = chains[i]
            y_refs[ch["d"]][0, ch["rs"], ch["sl"]] = y_c
            state[ch["d"], ch["h"]] = s_c
    for (d, h), s_c in state.items():
        s_ref[d, h] = s_c


def rwkv_scan(r, v, kk, ld0, k0, b0, ld1, k1, b1, *, batch):
    G, L, _ = r.shape
    step = SCAN_CHUNKS * CHUNK
    lc = L // batch
    ns_ctx = lc // step
    ns_lat = L // step
    n_steps = ns_ctx + ns_lat

    def idx(reverse, bi, s):
        in_ctx = s < ns_ctx
        c_ctx = ns_ctx - 1 - s if reverse else s
        c_lat = n_steps - 1 - s if reverse else s - ns_ctx
        return (jnp.where(in_ctx, 0, bi + 1), jnp.where(in_ctx, bi * ns_ctx + c_ctx, c_lat), 0)

    fwd = pl.BlockSpec((1, step, WIDTH), functools.partial(idx, False))
    rev = pl.BlockSpec((1, step, WIDTH), functools.partial(idx, True))
    out = jax.ShapeDtypeStruct((G, L, WIDTH), F32)
    return pl.pallas_call(
        _rw_scan_kernel,
        grid=(batch, n_steps),
        in_specs=[fwd] * 6 + [rev] * 6,
        out_specs=[fwd, rev],
        out_shape=[out, out],
        scratch_shapes=[pltpu.VMEM((2, HEADS, HEAD_DIM, HEAD_DIM), F32)],
        compiler_params=_cparams("parallel", "arbitrary"),
        name="rwkv_scan",
    )(r, ld0, k0, v, kk, b0, r, ld1, k1, v, kk, b1)


def _rw_readout(y0, y1, r, k0, k1, v, g, r_k, gn_g, gn_b, ones):
    y = y0 + y1
    inv_n = 1.0 / HEAD_DIM
    mu = _dot_split(y, ones) * inv_n
    yc = y - mu
    var = _dot_split(yc * yc, ones) * inv_n
    yn = yc * lax.rsqrt(var + RW_GN_EPS) * gn_g + gn_b
    bonus = _dot_split(r * (k0 + k1) * r_k, ones) * v
    return (yn + bonus) * g


def _short_conv(u, w_ref, b_ref):
    n = u.shape[0]
    row = lax.broadcasted_iota(jnp.int32, u.shape, 0)
    prev = jnp.where(row == 0, 0.0, pltpu.roll(u, 1, 0))
    nxt = jnp.where(row == n - 1, 0.0, pltpu.roll(u, n - 1, 0))
    return b_ref[...] + w_ref[0:1, :] * prev + w_ref[1:2, :] * u + w_ref[2:3, :] * nxt


@functools.lru_cache(maxsize=None)
def _dft_tables(n):
    nn = 2 * n
    nf = n + 256
    kf = np.arange(nf, dtype=np.int64)
    t = np.arange(n, dtype=np.int64)
    ang = 2.0 * np.pi * ((kf[:, None] * t[None, :]) % nn).astype(np.float64) / nn
    valid = (kf <= n)[:, None]
    cosm = np.where(valid, np.cos(ang), 0.0)
    sinm = np.where(valid, np.sin(ang), 0.0)
    ck = np.where((kf == 0) | (kf == n), 1.0, 2.0)[:, None] / nn
    fwd = np.stack([cosm, -sinm])
    inv = np.stack([(ck * cosm).T, (-ck * sinm).T])
    return fwd, inv, nf


def _hy_conv_kernel(*refs, z_is_raw):
    nb = HY_SEQS
    z_refs, x_refs = refs[:nb], refs[nb:2 * nb]
    (zw_ref, zb_ref, xw_ref, xb_ref, fwd_ref, inv_ref, hre_ref, him_ref, skip_ref,
     o_ref, acc_ref, zs_ref) = refs[2 * nb:]
    f = pl.program_id(1)
    z_of = lambda i: _short_conv(z_refs[i][0], zw_ref, zb_ref) if z_is_raw else z_refs[i][0]

    @pl.when(f == 0)
    def _():
        acc_ref[...] = jnp.zeros_like(acc_ref)
        for i in range(nb):
            zs_ref[:, i * HY_WIDTH:(i + 1) * HY_WIDTH] = z_of(i).astype(BF16)

    zb = zs_ref[...]
    zre = jnp.dot(fwd_ref[0], zb, preferred_element_type=F32)
    zim = jnp.dot(fwd_ref[1], zb, preferred_element_type=F32)
    hre = jnp.concatenate([hre_ref[0]] * nb, axis=1)
    him = jnp.concatenate([him_ref[0]] * nb, axis=1)
    yre = zre * hre - zim * him
    yim = zre * him + zim * hre
    acc_ref[...] += (jnp.dot(inv_ref[0], yre.astype(BF16), preferred_element_type=F32)
                     + jnp.dot(inv_ref[1], yim.astype(BF16), preferred_element_type=F32))

    @pl.when(f == pl.num_programs(1) - 1)
    def _():
        for i in range(nb):
            conv = acc_ref[:, i * HY_WIDTH:(i + 1) * HY_WIDTH]
            o_ref[i] = _short_conv(x_refs[i][0], xw_ref, xb_ref) * (conv + skip_ref[0] * z_of(i))


def hyena_long_conv(z_arr, z_blk, x_arr, x_blk, conv_w, conv_b, hre, him, skip, order, fwd, inv, *, n_seq, n, tf):
    nf = fwd.shape[1]
    nb = HY_SEQS
    zo, zc = z_blk
    xo, xc = x_blk
    seq = lambda off, cb, i: pl.BlockSpec((1, n, HY_WIDTH), lambda s, f: (s * nb + i + off, 0, cb))
    taps = lambda cb: [pl.BlockSpec((conv_w.shape[0], HY_WIDTH), lambda s, f: (0, cb)),
                       pl.BlockSpec((1, HY_WIDTH), lambda s, f: (0, cb))]
    z_is_raw = order == 0
    return pl.pallas_call(
        functools.partial(_hy_conv_kernel, z_is_raw=z_is_raw),
        grid=(n_seq // nb, nf // tf),
        in_specs=[seq(zo, zc, i) for i in range(nb)] + [seq(xo, xc, i) for i in range(nb)]
                 + taps(zc if z_is_raw else 0) + taps(xc)
                 + [pl.BlockSpec((2, tf, n), lambda s, f: (0, f, 0)),
                    pl.BlockSpec((2, n, tf), lambda s, f: (0, 0, f)),
                    pl.BlockSpec((1, tf, HY_WIDTH), lambda s, f: (order, f, 0)),
                    pl.BlockSpec((1, tf, HY_WIDTH), lambda s, f: (order, f, 0)),
                    pl.BlockSpec((1, 1, HY_WIDTH), lambda s, f: (order, 0, 0))],
        out_specs=pl.BlockSpec((nb, n, HY_WIDTH), lambda s, f: (s, 0, 0)),
        out_shape=jax.ShapeDtypeStruct((n_seq, n, HY_WIDTH), F32),
        scratch_shapes=[pltpu.VMEM((n, nb * HY_WIDTH), F32), pltpu.VMEM((n, nb * HY_WIDTH), BF16)],
        compiler_params=_cparams("parallel", "arbitrary"),
        name=f"hyena_long_conv_n{n}_o{order}",
    )(*([z_arr] * nb), *([x_arr] * nb), conv_w, conv_b, conv_w, conv_b, fwd, inv, hre, him, skip)


def hyena_filter_spectrum(n, w1, b1, w2, b2, w3, freq, fwd):
    hp = lax.Precision.HIGHEST
    pos = jnp.arange(n, dtype=F32)
    t = pos / max(n - 1, 1)
    bands = (HY_EMB_DIM - 1) // 2
    fr = jnp.linspace(1e-4, bands - 1, bands, dtype=F32)
    ang = (2 * math.pi / n) * pos[:, None] * fr[None, :]
    z = jnp.concatenate([t[:, None], jnp.cos(ang), -jnp.sin(ang)], axis=-1)
    h = jnp.sin(freq * (jnp.dot(z, w1, precision=hp) + b1))
    h = jnp.sin(freq * (jnp.dot(h, w2, precision=hp) + b2))
    h = matmul(h, w3, tn=w3.shape[1], split=True, name="hyena_filter_out")
    h = h.reshape(n, HY_ORDER, 2, HY_WIDTH)
    max_decay = math.log(HY_DECAY_TARGET) / HY_FAST_DECAY
    min_decay = math.log(HY_DECAY_TARGET) / HY_SLOW_DECAY
    deltas = jnp.abs(jnp.linspace(min_decay, max_decay, HY_WIDTH, dtype=F32))
    h = h * jnp.exp(-t[:, None] * deltas)[:, None, None, :]
    h_fwd = h[:, :, 0]
    h_bwd = h[:, :, 1] * (pos > 0).astype(F32)[:, None, None]
    l1 = jnp.sum(jnp.abs(h_fwd), axis=0) + jnp.sum(jnp.abs(h_bwd), axis=0)
    even = ((h_fwd + h_bwd) / l1).reshape(n, HY_ORDER * HY_WIDTH)
    odd = ((h_fwd - h_bwd) / l1).reshape(n, HY_ORDER * HY_WIDTH)
    nf = fwd.shape[1]
    hre = matmul(fwd[0], even, tn=HY_ORDER * HY_WIDTH, name="hyena_filter_dft_re")
    him = matmul(fwd[1], odd, tn=HY_ORDER * HY_WIDTH, name="hyena_filter_dft_im")
    to_ofc = lambda a: jnp.moveaxis(a.reshape(nf, HY_ORDER, HY_WIDTH), 1, 0)
    return to_ofc(hre), to_ofc(him)


def _na_lat_kernel(q_ref, k_ref, v_ref, kc_ref, vc_ref, bias_ref, o_ref, *, rows, wr):
    kc = kc_ref[0]
    vc = vc_ref[0]
    jobs = []
    for rr in range(NA_ROWS):
        i = pl.program_id(1) * NA_ROWS + rr
        r0 = jnp.clip(i - wr // 2, 0, rows - wr)
        off = r0 - i + (NA_WIN_ROWS - 1)
        start = pl.multiple_of(r0 * GRID_W, GRID_W)
        qs = slice(rr * GRID_W, (rr + 1) * GRID_W)
        q = q_ref[0, qs, :]
        kw = k_ref[0, pl.ds(start, wr * GRID_W), :]
        vw = v_ref[0, pl.ds(start, wr * GRID_W), :]
        for h in range(HEADS):
            sl = slice(h * HEAD_DIM, (h + 1) * HEAD_DIM)
            jobs.append(dict(q=q[:, sl], kw=kw[:, sl], vw=vw[:, sl], h=h, off=off, qs=qs, sl=sl))
    s_loc = [_dot_nt(j["q"], j["kw"]) + bias_ref[j["h"], j["off"]] for j in jobs]
    s_ctx = [_dot_nt(j["q"], kc[:, j["sl"]]) for j in jobs]
    m = [jnp.maximum(jnp.max(a, axis=-1, keepdims=True), jnp.max(b, axis=-1, keepdims=True))
         for a, b in zip(s_loc, s_ctx)]
    p_loc = [jnp.exp(a - mm) for a, mm in zip(s_loc, m)]
    p_ctx = [jnp.exp(b - mm) for b, mm in zip(s_ctx, m)]
    den = [jnp.sum(a, axis=-1, keepdims=True) + jnp.sum(b, axis=-1, keepdims=True) for a, b in zip(p_loc, p_ctx)]
    o = [_dot(a, j["vw"]) + _dot(b, vc[:, j["sl"]]) for a, b, j in zip(p_loc, p_ctx, jobs)]
    for j, oo, dd in zip(jobs, o, den):
        o_ref[0, j["qs"], j["sl"]] = oo / dd


def _na_bias_table(rpb, rows, wr):
    hp = lax.Precision.HIGHEST
    offs = np.arange(NA_WIN_ROWS) - (NA_WIN_ROWS - 1)
    dr = offs[:, None] + np.arange(wr)[None, :] + NA_WIN_ROWS - 1
    c_ar = np.arange(GRID_W)
    dc = np.clip(c_ar[None, :] - c_ar[:, None] + NA_WIN_COLS - 1, 0, 2 * NA_WIN_COLS - 2)
    c0 = np.clip(c_ar - NA_WIN_COLS // 2, 0, GRID_W - NA_WIN_COLS)
    in_win = (c_ar[None, :] >= c0[:, None]) & (c_ar[None, :] < c0[:, None] + NA_WIN_COLS)
    oh_r = jnp.asarray(dr[..., None] == np.arange(2 * NA_WIN_ROWS - 1), dtype=F32)
    oh_c = jnp.asarray(dc[..., None] == np.arange(2 * NA_WIN_COLS - 1), dtype=F32)
    by_row = jnp.einsum("hab,ora->horb", rpb.astype(F32), oh_r, precision=hp)
    bias = jnp.einsum("horb,qkb->hoqrk", by_row, oh_c, precision=hp)
    bias = jnp.where(in_win[None, None, :, None, :], bias, NEG_BIG)
    return bias.reshape(HEADS, NA_WIN_ROWS, GRID_W, wr * GRID_W)


def na_latent(qn, kn, vb, bias, *, batch):
    G, L, _ = qn.shape
    lc = L // batch
    rows = L // GRID_W
    wr = min(NA_WIN_ROWS, rows)
    assert rows % NA_ROWS == 0
    qspec = pl.BlockSpec((1, NA_ROWS * GRID_W, WIDTH), lambda b_, i: (b_ + 1, i, 0))
    seq = pl.BlockSpec((1, L, WIDTH), lambda b_, i: (b_ + 1, 0, 0))
    ctx = pl.BlockSpec((1, lc, WIDTH), lambda b_, i: (0, b_, 0))
    return pl.pallas_call(
        functools.partial(_na_lat_kernel, rows=rows, wr=wr),
        grid=(batch, rows // NA_ROWS),
        in_specs=[qspec, seq, seq, ctx, ctx, pl.BlockSpec(bias.shape, lambda b_, i: (0, 0, 0, 0))],
        out_specs=pl.BlockSpec((1, NA_ROWS * GRID_W, WIDTH), lambda b_, i: (b_, i, 0)),
        out_shape=jax.ShapeDtypeStruct((batch, L, WIDTH), F32),
        compiler_params=_cparams("parallel", "arbitrary"),
        name="na_latent",
    )(qn, kn, vb, kn, vb, bias)


def _na_ctx_kernel(q_ref, k_ref, v_ref, o_ref):
    q = q_ref[0]
    k = k_ref[0]
    v = v_ref[0]
    for h in range(HEADS):
        sl = slice(h * HEAD_DIM, (h + 1) * HEAD_DIM)
        s = _dot_nt(q[:, sl], k[:, sl])
        p = jnp.exp(s - jnp.max(s, axis=-1, keepdims=True))
        o_ref[0, :, sl] = _dot(p, v[:, sl]) / jnp.sum(p, axis=-1, keepdims=True)


def na_context(qn, kn, vb, *, batch):
    G, L, _ = qn.shape
    lc = L // batch
    spec = pl.BlockSpec((1, lc, WIDTH), lambda b_: (0, b_, 0))
    return pl.pallas_call(
        _na_ctx_kernel,
        grid=(batch,),
        in_specs=[spec] * 3,
        out_specs=spec,
        out_shape=jax.ShapeDtypeStruct((1, L, WIDTH), F32),
        compiler_params=_cparams("parallel"),
        name="na_context",
    )(qn, kn, vb)


def _merge_kernel(x_ref, gl_ref, y0_ref, y1_ref, r_ref, k0_ref, k1_ref, v_ref, g_ref, hy_ref, na_ref, g1_ref,
                  rk_ref, gg_ref, gb_ref, ones_ref, wr_ref, wh_ref, wn_ref, wo_ref, o_ref):
    D = x_ref.shape[-1]
    gl = gl_ref[0].astype(F32)
    y_rw = _rw_readout(y0_ref[0], y1_ref[0], r_ref[0], k0_ref[0], k1_ref[0], v_ref[0], g_ref[0],
                       rk_ref[...], gg_ref[...], gb_ref[...], ones_ref[...])
    m = (_sigmoid(gl[:, 0:D]) * _dot(y_rw, wr_ref[...])
         + _sigmoid(gl[:, D:2 * D]) * _dot(hy_ref[0], wh_ref[...])
         + _sigmoid(gl[:, 2 * D:3 * D]) * _dot(na_ref[0], wn_ref[...]))
    o_ref[0] = x_ref[0] + g1_ref[0] * _dot(m, wo_ref[...])


def merge_branches(xs, u_gate, rw_parts, y_hy, y_na, gate1, rw_params, w_rw, w_hy, w_na, w_o, *, g_off, tm=512):
    G, L, D = xs.shape
    tok = lambda width: pl.BlockSpec((1, tm, width), lambda g_, i: (g_ + g_off, i, 0))
    own = lambda width: pl.BlockSpec((1, tm, width), lambda g_, i: (g_, i, 0))
    full = lambda a: pl.BlockSpec(a.shape, lambda g_, i: (0,) * a.ndim, pipeline_mode=pl.Buffered(1))
    consts = (*rw_params, w_rw, w_hy, w_na, w_o)
    return pl.pallas_call(
        _merge_kernel,
        grid=(G - g_off, L // tm),
        in_specs=[tok(D), tok(3 * D)] + [tok(WIDTH)] * len(rw_parts) + [own(HY_WIDTH), own(WIDTH),
                  pl.BlockSpec((1, 1, D), lambda g_, i: (g_ + g_off, 0, 0))] + [full(c) for c in consts],
        out_specs=pl.BlockSpec((1, tm, D), lambda g_, i: (g_, i, 0)),
        out_shape=jax.ShapeDtypeStruct((G - g_off, L, D), F32),
        compiler_params=_cparams("parallel", "parallel"),
        name="merge_branches",
    )(xs, u_gate, *rw_parts, y_hy, y_na, gate1, *consts)


def _ffn_kernel(x_ref, sh_ref, sc_ref, g_ref, g2_ref, w1_ref, w3_ref, w2_ref, o_ref, h_ref, acc_ref):
    f = pl.program_id(2)

    @pl.when(f == 0)
    def _():
        h_ref[...] = _norm_mod(x_ref[0], g_ref[...], sh_ref[0], sc_ref[0]).astype(BF16)
        acc_ref[...] = jnp.zeros_like(acc_ref)

    h = h_ref[...]
    a = jnp.dot(h, w1_ref[...], preferred_element_type=F32)
    b = jnp.dot(h, w3_ref[...], preferred_element_type=F32)
    acc_ref[...] += _dot(_silu(a) * b, w2_ref[...])

    @pl.when(f == pl.num_programs(2) - 1)
    def _():
        o_ref[0] = x_ref[0] + g2_ref[0] * acc_ref[...]


def ffn_dense(xs, shift, scale, g, gate2, w1, w3, w2, *, tm=1024, tf):
    G, L, D = xs.shape
    FF = w1.shape[1]
    assert L % tm == 0 and FF % tf == 0
    mod = pl.BlockSpec((1, 1, D), lambda g_, i, f: (g_, 0, 0))
    return pl.pallas_call(
        _ffn_kernel,
        grid=(G, L // tm, FF // tf),
        in_specs=[pl.BlockSpec((1, tm, D), lambda g_, i, f: (g_, i, 0)), mod, mod,
                  pl.BlockSpec((1, D), lambda g_, i, f: (0, 0)), mod,
                  pl.BlockSpec((D, tf), lambda g_, i, f: (0, f)),
                  pl.BlockSpec((D, tf), lambda g_, i, f: (0, f)),
                  pl.BlockSpec((tf, D), lambda g_, i, f: (f, 0))],
        out_specs=pl.BlockSpec((1, tm, D), lambda g_, i, f: (g_, i, 0)),
        out_shape=jax.ShapeDtypeStruct((G, L, D), F32),
        scratch_shapes=[pltpu.VMEM((tm, D), BF16), pltpu.VMEM((tm, D), F32)],
        compiler_params=_cparams("parallel", "parallel", "arbitrary"),
        name="ffn_dense",
    )(xs, shift, scale, g, gate2, w1, w3, w2)


def _moe_route_kernel(x_ref, sh_ref, sc_ref, g_ref, rt_ref, h_o, gates_o, rank_o, rank_t_o, start_o, cnt_o,
                      cnt_row, cnt_col):
    i = pl.program_id(1)

    @pl.when(i == 0)
    def _():
        cnt_row[...] = jnp.zeros_like(cnt_row)
        cnt_col[...] = jnp.zeros_like(cnt_col)

    h = _norm_mod(x_ref[0], g_ref[...], sh_ref[0], sc_ref[0])
    h_o[0] = h.astype(BF16)
    rt = rt_ref[...]
    hh = h.astype(BF16)
    hl = (h - hh.astype(F32)).astype(BF16)
    rh = rt.astype(BF16)
    rl = (rt - rh.astype(F32)).astype(BF16)
    logits = (jnp.dot(hh, rh, preferred_element_type=F32) + jnp.dot(hl, rh, preferred_element_type=F32)
              + jnp.dot(hh, rl, preferred_element_type=F32))
    lane = lax.broadcasted_iota(jnp.int32, logits.shape, 1)
    logits = jnp.where(lane < N_EXPERTS, logits, NEG_BIG)
    m1 = jnp.max(logits, axis=-1, keepdims=True)
    i1 = jnp.min(jnp.where(logits == m1, lane, LANES), axis=-1, keepdims=True)
    rest = jnp.where(lane == i1, NEG_BIG, logits)
    m2 = jnp.max(rest, axis=-1, keepdims=True)
    i2 = jnp.min(jnp.where(rest == m2, lane, LANES), axis=-1, keepdims=True)
    e2 = jnp.exp(m2 - m1)
    gates = jnp.where(lane == i1, 1.0 / (1.0 + e2), 0.0) + jnp.where(lane == i2, e2 / (1.0 + e2), 0.0)
    gates_o[0] = gates

    tm = gates.shape[0]
    sel = gates > 0.0
    m_tok = jnp.where(sel, 1.0, 0.0)
    m_exp = m_tok.T[:2 * HALO]
    row = lax.broadcasted_iota(jnp.int32, (tm, tm), 0)
    col = lax.broadcasted_iota(jnp.int32, (tm, tm), 1)
    before = jnp.where(col < row, 1.0, 0.0).astype(BF16)
    after = jnp.where(row < col, 1.0, 0.0).astype(BF16)
    rank = jnp.dot(before, m_tok.astype(BF16), preferred_element_type=F32) + cnt_row[...]
    rank_t = jnp.dot(m_exp.astype(BF16), after, preferred_element_type=F32)[:HALO]
    rank_t = rank_t + jnp.tile(cnt_col[...], (1, tm // LANES))
    start_o[0, 0] = cnt_row[...]
    rank_o[0] = jnp.where(sel, rank, NEG_BIG)
    rank_t_o[0] = jnp.where(m_exp[:HALO] > 0.0, rank_t, NEG_BIG)
    cnt_row[...] += jnp.sum(m_tok, axis=0, keepdims=True)
    cnt_col[...] += jnp.sum(m_exp[:HALO], axis=1, keepdims=True)
    cnt_o[0] = cnt_row[...]


def _moe_expert_kernel(cnt_ref, bnd_ref, h_ref, gates_ref, rank_ref, rank_t_ref, x_ref, g2_ref, w1_ref, w3_ref, w2_ref,
                       o_ref, hs_ref, ys_ref, *, rb, tile):
    gi = pl.program_id(0)
    e = pl.program_id(1)
    f = pl.program_id(2)
    n_e = cnt_ref[gi * pl.num_programs(1) + e]
    unit = rb // 4
    n_unit = (n_e + unit - 1) // unit
    n_full = n_unit // 4
    tm = h_ref.shape[1]
    n_e_all = pl.num_programs(1)

    def for_tiles(start, size, fn):
        for i in range(tm // tile):
            lo = bnd_ref[(gi * (tm // tile + 1) + i) * n_e_all + e]
            hi = bnd_ref[(gi * (tm // tile + 1) + i + 1) * n_e_all + e]

            @pl.when(jnp.logical_and(hi > start, lo < start + size))
            def _():
                fn(slice(i * tile, (i + 1) * tile))

    def for_blocks(fn):
        def body(j, carry):
            fn(pl.multiple_of(j * rb, rb), rb)
            return carry

        lax.fori_loop(0, n_full, body, 0)

        has_half = (n_unit // 2) % 2

        @pl.when(has_half == 1)
        def _():
            fn(pl.multiple_of(n_full * rb, unit), rb // 2)

        @pl.when(n_unit % 2 == 1)
        def _():
            fn(pl.multiple_of(n_full * rb + has_half * (rb // 2), unit), unit)

    @pl.when(jnp.logical_and(e == 0, f == 0))
    def _():
        o_ref[...] = jnp.zeros_like(o_ref)

    @pl.when(f == 0)
    def _():
        rank_t = rank_t_ref[0, pl.ds(e, 1), :]

        def gather(start, size):
            hs_ref[pl.ds(start, size), :] = jnp.zeros((size, hs_ref.shape[1]), BF16)
            sub = lax.broadcasted_iota(jnp.int32, (size, tile), 0).astype(F32) + start.astype(F32)

            def from_tile(ts):
                pick = jnp.where(rank_t[:, ts] == sub, 1.0, 0.0).astype(BF16)
                got = jnp.dot(pick, h_ref[0, ts, :], preferred_element_type=F32)
                hs_ref[pl.ds(start, size), :] += got.astype(BF16)

            for_tiles(start, size, from_tile)

        for_blocks(gather)

    def expert_block(start, size):
        hs = hs_ref[pl.ds(start, size), :]
        a = jnp.dot(hs, w1_ref[0].astype(BF16), preferred_element_type=F32)
        b = jnp.dot(hs, w3_ref[0].astype(BF16), preferred_element_type=F32)
        return _dot(_silu(a) * b, w2_ref[0])

    @pl.when(f == 0)
    def _():
        def first(start, size):
            ys_ref[pl.ds(start, size), :] = expert_block(start, size)

        for_blocks(first)

    @pl.when(f > 0)
    def _():
        def more(start, size):
            ys_ref[pl.ds(start, size), :] += expert_block(start, size)

        for_blocks(more)

    last_f = f == pl.num_programs(2) - 1

    @pl.when(last_f)
    def _():
        lane = lax.broadcasted_iota(jnp.int32, (tm, LANES), 1)
        gate = jnp.sum(jnp.where(lane == e, gates_ref[0], 0.0), axis=-1, keepdims=True)
        rank = jnp.sum(jnp.where(lane == e, rank_ref[0], 0.0), axis=-1, keepdims=True)

        def scatter(start, size):
            slot = lax.broadcasted_iota(jnp.int32, (tile, size), 1).astype(F32) + start.astype(F32)
            ys = ys_ref[pl.ds(start, size), :].astype(BF16)

            def to_tile(ts):
                put = jnp.where(rank[ts] == slot, gate[ts], 0.0)
                o_ref[0, ts, :] += _dot(put, ys)

            for_tiles(start, size, to_tile)

        for_blocks(scatter)

    @pl.when(jnp.logical_and(e == pl.num_programs(1) - 1, last_f))
    def _():
        o_ref[0] = x_ref[0] + g2_ref[0] * o_ref[0]


def moe_ffn(xs, shift, scale, g, gate2, router, w1, w3, w2, *, g_off, tm_route=512, rb=512, tf):
    G, L, D = xs.shape
    E, _, FF = w1.shape
    n_g = G - g_off
    mod = pl.BlockSpec((1, 1, D), lambda g_, i: (g_ + g_off, 0, 0))
    tok = lambda width: pl.BlockSpec((1, tm_route, width), lambda g_, i: (g_, i, 0))
    n_tiles = L // tm_route
    h, gates, rank, rank_t, starts, cnt = pl.pallas_call(
        _moe_route_kernel,
        grid=(n_g, L // tm_route),
        in_specs=[pl.BlockSpec((1, tm_route, D), lambda g_, i: (g_ + g_off, i, 0)), mod, mod,
                  pl.BlockSpec((1, D), lambda g_, i: (0, 0)),
                  pl.BlockSpec((D, LANES), lambda g_, i: (0, 0))],
        out_specs=[tok(D), tok(LANES), tok(LANES),
                   pl.BlockSpec((1, HALO, tm_route), lambda g_, i: (g_, 0, i)),
                   pl.BlockSpec((1, 1, 1, LANES), lambda g_, i: (g_, i, 0, 0)),
                   pl.BlockSpec((1, 1, LANES), lambda g_, i: (g_, 0, 0))],
        out_shape=[jax.ShapeDtypeStruct((n_g, L, D), BF16),
                   jax.ShapeDtypeStruct((n_g, L, LANES), F32),
                   jax.ShapeDtypeStruct((n_g, L, LANES), F32),
                   jax.ShapeDtypeStruct((n_g, HALO, L), F32),
                   jax.ShapeDtypeStruct((n_g, n_tiles, 1, LANES), F32),
                   jax.ShapeDtypeStruct((n_g, 1, LANES), F32)],
        scratch_shapes=[pltpu.VMEM((1, LANES), F32), pltpu.VMEM((HALO, LANES), F32)],
        compiler_params=_cparams("parallel", "arbitrary"),
        name="moe_route",
    )(xs, shift, scale, g, router)
    counts = cnt[:, 0, :E].astype(jnp.int32)
    bounds = jnp.concatenate([starts[:, :, 0, :E].astype(jnp.int32), counts[:, None, :]], axis=1)

    once = pl.Buffered(1)
    grp = lambda width: pl.BlockSpec((1, L, width), lambda g_, e, f, c, b: (g_, 0, 0), pipeline_mode=once)
    return pl.pallas_call(
        functools.partial(_moe_expert_kernel, rb=rb, tile=tm_route),
        grid_spec=pltpu.PrefetchScalarGridSpec(
            num_scalar_prefetch=2,
            grid=(n_g, E, FF // tf),
            in_specs=[grp(D), grp(LANES), grp(LANES),
                      pl.BlockSpec((1, HALO, L), lambda g_, e, f, c, b: (g_, 0, 0), pipeline_mode=once),
                      pl.BlockSpec((1, L, D), lambda g_, e, f, c, b: (g_ + g_off, 0, 0), pipeline_mode=once),
                      pl.BlockSpec((1, 1, D), lambda g_, e, f, c, b: (g_ + g_off, 0, 0)),
                      pl.BlockSpec((1, D, tf), lambda g_, e, f, c, b: (e, 0, f)),
                      pl.BlockSpec((1, D, tf), lambda g_, e, f, c, b: (e, 0, f)),
                      pl.BlockSpec((1, tf, D), lambda g_, e, f, c, b: (e, f, 0))],
            out_specs=pl.BlockSpec((1, L, D), lambda g_, e, f, c, b: (g_, 0, 0), pipeline_mode=once),
            scratch_shapes=[pltpu.VMEM((L + rb, D), BF16), pltpu.VMEM((L + rb, D), F32)]),
        out_shape=jax.ShapeDtypeStruct((n_g, L, D), F32),
        compiler_params=pltpu.CompilerParams(dimension_semantics=("parallel", "arbitrary", "arbitrary"),
                                             vmem_limit_bytes=MOE_VMEM_LIMIT),
        name="moe_experts",
    )(counts.reshape(-1), bounds.reshape(-1), h, gates, rank, rank_t, xs, gate2, w1, w3, w2)


def _rope_tables(n):
    half = HEAD_DIM // 2
    nf = half // 2
    t = np.arange(n)
    inv = ROPE_BASE ** (-np.arange(nf, dtype=np.float64) / nf)
    d = np.arange(HEAD_DIM)
    pos = np.where(d[None, :] < half, (t // GRID_W)[:, None], (t % GRID_W)[:, None]).astype(np.float64)
    ang = pos * inv[d % nf][None, :]
    sign = np.where(d % half < nf, -1.0, 1.0)
    cos = np.tile(np.cos(ang), (1, HEADS))
    sin = np.tile(np.sin(ang) * sign[None, :], (1, HEADS))
    return jnp.asarray(cos, dtype=F32), jnp.asarray(sin, dtype=F32)


def kernel(x, c, ctx, c_ctx, ada_w, ada_b, norm1_g, norm2_g, w_in, rw_shift, rw_w0, rw_w2, rw_a0, rw_a2, rw_g2, rw_kk, rw_ka, rw_rk, rw_gn_g, rw_gn_b, hy_conv_w, hy_conv_b, hy_f_w1, hy_f_b1, hy_f_w2, hy_f_b2, hy_f_w3, hy_f_freq, hy_skip, na_q_gain, na_k_gain, na_rpb, w_br_rw, w_br_hy, w_br_na, w_out, ff_w1, ff_w3, ff_w2, moe_router, moe_w1, moe_w3, moe_w2):
    B, L, D = x.shape
    lc = ctx.shape[1]
    depth = ada_w.shape[0]
    assert B * lc == L and lc == SEQ_TILE and L % GRID_W == 0
    G = B + 1
    rows = L // GRID_W
    wr = min(NA_WIN_ROWS, rows)
    rw_in = 3 * WIDTH + 4 * RW_LORA + RW_GATE_LORA
    hy_in = (HY_ORDER + 1) * HY_WIDTH
    na_in = 3 * WIDTH
    splits = (rw_in, rw_in + hy_in, rw_in + hy_in + na_in)

    xs = jnp.concatenate([ctx.reshape(1, L, D), x], axis=0)
    act = jnp.concatenate([c_ctx[None, :], c], axis=0)
    act = act * _sigmoid(act)
    act = jnp.pad(act, ((0, (-G) % 16), (0, 0)))
    ones = _head_ones()
    cos, sin = _rope_tables(L)
    fwd_lat, inv_lat, _ = _dft_tables(L)
    fwd_ctx, inv_ctx, _ = _dft_tables(lc)
    fwd_lat, inv_lat, fwd_ctx, inv_ctx = (jnp.asarray(a, dtype=F32).astype(BF16)
                                          for a in (fwd_lat, inv_lat, fwd_ctx, inv_ctx))
    row2 = lambda a: a.reshape(1, -1)

    for li in range(depth):
        last = li == depth - 1
        mods = matmul(act, ada_w[li], tn=1024, split=True, name="adaln")[:G] + ada_b[li]
        sh1, sc1, g1, sh2, sc2, g2 = (m.reshape(G, 1, D) for m in jnp.split(mods, 6, axis=-1))

        rw_feat_params = (rw_shift[li], rw_w0[li], rw_w2[li].astype(BF16), rw_a0[li], rw_a2[li].astype(BF16),
                          rw_g2[li].astype(BF16), row2(rw_kk[li]), row2(rw_ka[li]))
        r, v, kk, g, ld0, ld1, k0, k1, b0, b1, u_hy, qn, kn, vb, u_gate = branch_projections(
            xs, sh1, sc1, row2(norm1_g[li]), w_in[li].astype(BF16), row2(jnp.tile(na_q_gain[li], HEADS)),
            row2(jnp.tile(na_k_gain[li], HEADS)), ones, cos, sin, rw_feat_params, splits)

        y0, y1 = rwkv_scan(r, v, kk, ld0, k0, b0, ld1, k1, b1, batch=B)
        rw_parts = (y0, y1, r, k0, k1, v, g)
        rw_params = (row2(rw_rk[li]), row2(rw_gn_g[li]), row2(rw_gn_b[li]), ones)

        f_args = (hy_f_w1[li], hy_f_b1[li], hy_f_w2[li], hy_f_b2[li], hy_f_w3[li], hy_f_freq[li])
        taps = (hy_conv_w[li], row2(hy_conv_b[li]))
        skip = hy_skip[li].reshape(HY_ORDER, 1, HY_WIDTH)
        hre, him = hyena_filter_spectrum(L, *f_args, fwd_lat)
        conv = functools.partial(hyena_long_conv, skip=skip, fwd=fwd_lat, inv=inv_lat, n_seq=B, n=L,
                                 tf=HY_FREQ_TILE)
        z1 = conv(u_hy, (1, 0), u_hy, (1, 1), *taps, hre, him, order=0)
        y_hy = conv(z1, (0, 0), u_hy, (1, 2), *taps, hre, him, order=1)
        if not last:
            hy_c = u_hy[0].reshape(B, lc, hy_in)
            hre_c, him_c = hyena_filter_spectrum(lc, *f_args, fwd_ctx)
            conv_c = functools.partial(hyena_long_conv, skip=skip, fwd=fwd_ctx, inv=inv_ctx, n_seq=B, n=lc,
                                       tf=fwd_ctx.shape[1])
            z1c = conv_c(hy_c, (0, 0), hy_c, (0, 1), *taps, hre_c, him_c, order=0)
            y_hy_c = conv_c(z1c, (0, 0), hy_c, (0, 2), *taps, hre_c, him_c, order=1)
            y_hy = jnp.concatenate([y_hy_c.reshape(1, L, HY_WIDTH), y_hy], axis=0)

        bias = _na_bias_table(na_rpb[li], rows, wr)
        y_na = na_latent(qn, kn, vb, bias, batch=B)
        if not last:
            y_na = jnp.concatenate([na_context(qn, kn, vb, batch=B), y_na], axis=0)

        g_off = 1 if last else 0
        xs = merge_branches(xs, u_gate, rw_parts, y_hy, y_na, g1, rw_params, w_br_rw[li].astype(BF16),
                            w_br_hy[li].astype(BF16), w_br_na[li].astype(BF16), w_out[li].astype(BF16), g_off=g_off)
        if last:
            sh2, sc2, g2 = sh2[1:], sc2[1:], g2[1:]

        n2 = row2(norm2_g[li])
        if li % 2 == 0:
            j = li // 2
            xs = ffn_dense(xs, sh2, sc2, n2, g2, ff_w1[j].astype(BF16), ff_w3[j].astype(BF16), ff_w2[j].astype(BF16),
                           tf=ff_w1.shape[2] // 2)
        else:
            j = li // 2
            router = jnp.pad(moe_router[j], ((0, 0), (0, LANES - N_EXPERTS)))
            xs = moe_ffn(xs, sh2, sc2, n2, g2, router, moe_w1[j], moe_w3[j], moe_w2[j], g_off=0,
                         tf=moe_w1.shape[3] // MOE_FF_CHUNKS)
    return xs if depth == 0 else xs[-B:]
```
